```python
import math
import jax, jax.numpy as jnp
from jax import lax
import numpy as np

D_MODEL = 1024
BATCH = 8
SEQ = 2048
DEPTH = 2
DEC_BATCH = 32
DEC_SEQ = 2048
PAST_LEN = 128

HA = 8
D_NOPE = 64
D_ROPE = 32
D_V = 64
D_CQ = 384
D_C = 256
ROPE_BASE = 10000.0
QB = 128
MLA_SCALE = (D_NOPE + D_ROPE) ** -0.5
HB = 8
KVB = 2
GQ = HB // KVB
HD = 64
WINDOW = 128
WB = 128
WIN_SCALE = HD ** -0.5
NUM_BUCKETS = 32
MAX_DIST = 128
N_GROUPS = 4
EPG = 8
N_EXPERTS = N_GROUPS * EPG
TOP_K = 2
D_EXPERT = 256
MOE_BLK = 256
SPLITS = [D_CQ, D_CQ + D_C, D_CQ + D_C + D_ROPE,
          D_CQ + D_C + D_ROPE + HB * HD,
          D_CQ + D_C + D_ROPE + HB * HD + KVB * HD,
          D_CQ + D_C + D_ROPE + HB * HD + 2 * KVB * HD,
          D_CQ + D_C + D_ROPE + HB * HD + 2 * KVB * HD + D_MODEL]
N_IN = SPLITS[-1] + D_MODEL
EPS = 1e-6
NEG = -1e30

kernel_name = 'hybrid_mla_wingqa_hiermoe_encoder'


def rms_norm(x, g):
    xf = x.astype(jnp.float32)
    y = xf * lax.rsqrt(jnp.mean(xf * xf, axis=-1, keepdims=True) + EPS)
    return (y * g.astype(jnp.float32)).astype(x.dtype)


def rope_tables(seq):
    inv = ROPE_BASE ** (-jnp.arange(0, D_ROPE, 2, dtype=jnp.float32) / D_ROPE)
    ang = jnp.arange(seq, dtype=jnp.float32)[:, None] * inv[None, :]
    return jnp.cos(ang), jnp.sin(ang)


def apply_rope(t, cos, sin):
    half = t.shape[-1] // 2
    t1, t2 = t[..., :half], t[..., half:]
    cos = cos.astype(t.dtype)
    sin = sin.astype(t.dtype)
    return jnp.concatenate([t1 * cos - t2 * sin, t2 * cos + t1 * sin], axis=-1)


def t5_bucket(rel):
    half = NUM_BUCKETS // 2
    max_exact = half // 2
    n = jnp.abs(rel)
    nf = jnp.maximum(n, 1).astype(jnp.float32)
    large = max_exact + (jnp.log(nf / max_exact) / math.log(MAX_DIST / max_exact)
                         * (half - max_exact)).astype(jnp.int32)
    large = jnp.minimum(large, half - 1)
    return jnp.where(rel > 0, half, 0) + jnp.where(n < max_exact, n, large)


def window_bias_mask(rel_bias, seq):
    nb = seq // WB
    qi = jnp.arange(WB)[:, None]
    kj = jnp.arange(3 * WB)[None, :]
    rel = kj - WB - qi
    bias = rel_bias[t5_bucket(rel)]
    bias = jnp.transpose(bias, (2, 0, 1)).reshape(KVB, GQ, WB, 3 * WB).astype(jnp.float32)
    kpos = jnp.arange(nb)[:, None, None] * WB + kj[None] - WB
    mask = (kpos >= 0) & (kpos < seq) & (jnp.abs(rel)[None] <= WINDOW)
    return bias, mask


def mla_attention(q_nope, q_rope, k_nope, k_rope, v):
    B, S = q_nope.shape[0], q_nope.shape[1]
    nq = S // QB

    def to_blocks(t):
        return jnp.moveaxis(t.reshape(B, nq, QB, *t.shape[2:]), 1, 0)

    def one_block(args):
        qn, qr = args
        s = (jnp.einsum('bqhd,bkhd->bhqk', qn, k_nope, preferred_element_type=jnp.float32)
             + jnp.einsum('bqhd,bkd->bhqk', qr, k_rope, preferred_element_type=jnp.float32))
        p = jax.nn.softmax(s * MLA_SCALE, axis=-1).astype(v.dtype)
        return jnp.einsum('bhqk,bkhd->bqhd', p, v)

    o = lax.map(one_block, (to_blocks(q_nope), to_blocks(q_rope)))
    return jnp.moveaxis(o, 0, 1).reshape(B, S, HA * D_V)


def window_attention(q, k, v, sink, bias, mask):
    B, S = q.shape[0], q.shape[1]
    nb = S // WB
    qb = q.reshape(B, nb, WB, KVB, GQ, HD)
    pad = ((0, 0), (WB, WB), (0, 0), (0, 0))

    def band(t):
        tb = jnp.pad(t, pad).reshape(B, nb + 2, WB, KVB, HD)
        return jnp.concatenate([tb[:, :-2], tb[:, 1:-1], tb[:, 2:]], axis=2)

    kb, vb = band(k), band(v)
    s = jnp.einsum('bnqkgd,bnjkd->bnkgqj', qb, kb, preferred_element_type=jnp.float32) * WIN_SCALE + bias
    s = jnp.where(mask[None, :, None, None], s, NEG)
    sink_l = sink.astype(jnp.float32).reshape(1, 1, KVB, GQ, 1, 1)
    m = jnp.maximum(jnp.max(s, axis=-1, keepdims=True), sink_l)
    e = jnp.exp(s - m)
    p = e / (jnp.sum(e, axis=-1, keepdims=True) + jnp.exp(sink_l - m))
    o = jnp.einsum('bnkgqj,bnjkd->bnqkgd', p.astype(v.dtype), vb)
    return o.reshape(B, S, HB * HD)


def token_mixers(h, w_in, ln_cq, ln_ckv, w_uq, w_ukv, w_oa, sink, w_ob, w_out,
                 cos, sin, win_bias, win_mask):
    B, S, _ = h.shape
    z = h @ w_in
    c_q, c_kv, k_r, q_b, k_b, v_b, g_a, g_b = jnp.split(z, SPLITS, axis=-1)
    c_q = rms_norm(c_q, ln_cq)
    c_kv = rms_norm(c_kv, ln_ckv)
    q = (c_q @ w_uq).reshape(B, S, HA, D_NOPE + D_ROPE)
    kv = (c_kv @ w_ukv).reshape(B, S, HA, D_NOPE + D_V)
    q_nope = q[..., :D_NOPE]
    q_rope = apply_rope(q[..., D_NOPE:], cos[:, None, :], sin[:, None, :])
    k_nope, v_a = kv[..., :D_NOPE], kv[..., D_NOPE:]
    k_rope = apply_rope(k_r, cos, sin)
    y_a = mla_attention(q_nope, q_rope, k_nope, k_rope, v_a) @ w_oa
    y_b = window_attention(q_b.reshape(B, S, HB, HD), k_b.reshape(B, S, KVB, HD),
                           v_b.reshape(B, S, KVB, HD), sink, win_bias, win_mask) @ w_ob
    merged = jax.nn.sigmoid(g_a) * y_a + jax.nn.sigmoid(g_b) * y_b
    return merged @ w_out


def hier_moe(h, w_gr, b_gr, w_er, b_er, w_gate, w_up, w_down):
    B, S, D = h.shape
    T = B * S
    xf = h.reshape(T, D)
    g_logits = (xf @ w_gr).astype(jnp.float32) + b_gr.astype(jnp.float32)
    g_prob = jax.nn.softmax(g_logits, axis=-1)
    _, grp = lax.top_k(g_logits, 1)
    grp_w = jnp.take_along_axis(g_prob, grp, axis=-1)
    e_logits = ((xf @ w_er).astype(jnp.float32) + b_er.astype(jnp.float32)).reshape(T, N_GROUPS, EPG)
    e_in = e_logits[jnp.arange(T), grp[:, 0]]
    top_v, top_i = lax.top_k(e_in, TOP_K)
    gate = grp_w * jax.nn.softmax(top_v, axis=-1)
    expert = grp * EPG + top_i
    n_assign = T * TOP_K
    n_blocks = -(-n_assign // MOE_BLK) + N_EXPERTS
    flat_e = expert.reshape(n_assign)
    flat_w = gate.reshape(n_assign)
    flat_t = jnp.repeat(jnp.arange(T, dtype=jnp.int32), TOP_K)
    order = jnp.argsort(flat_e)
    se, st, sw = flat_e[order], flat_t[order], flat_w[order]
    counts = jnp.bincount(flat_e, length=N_EXPERTS)
    padded = ((counts + MOE_BLK - 1) // MOE_BLK) * MOE_BLK
    pad_end = jnp.cumsum(padded)
    pad_start = pad_end - padded
    start = jnp.cumsum(counts) - counts
    dest = pad_start[se] + jnp.arange(n_assign) - start[se]
    row_tok = jnp.full((n_blocks * MOE_BLK,), T, jnp.int32).at[dest].set(st)
    blk_expert = jnp.minimum(jnp.searchsorted(pad_end, jnp.arange(n_blocks) * MOE_BLK, side='right'),
                             N_EXPERTS - 1)
    xpad = jnp.concatenate([xf, jnp.zeros((1, D), xf.dtype)], axis=0)
    xb = xpad[row_tok].reshape(n_blocks, MOE_BLK, D)

    def expert_block(args):
        xblk, e = args
        a = xblk @ w_gate[e]
        u = xblk @ w_up[e]
        return (jax.nn.silu(a) * u) @ w_down[e]

    yb = lax.map(expert_block, (xb, blk_expert)).reshape(n_blocks * MOE_BLK, D)
    out = jnp.zeros((T, D), h.dtype).at[st].add(yb[dest] * sw[:, None].astype(h.dtype))
    return out.reshape(B, S, D)


def encoder_trunk(x, rel_bias, ln_mix, w_in, ln_cq, ln_ckv, w_uq, w_ukv, w_oa, sink, w_ob,
                  w_out, ln_ffn, w_gr, b_gr, w_er, b_er, w_gate, w_up, w_down, ln_final):
    S = x.shape[1]
    cos, sin = rope_tables(S)
    win_bias, win_mask = window_bias_mask(rel_bias, S)
    for l in range(DEPTH):
        h = rms_norm(x, ln_mix[l])
        x = x + token_mixers(h, w_in[l], ln_cq[l], ln_ckv[l], w_uq[l], w_ukv[l], w_oa[l],
                             sink[l], w_ob[l], w_out[l], cos, sin, win_bias, win_mask)
        h = rms_norm(x, ln_ffn[l])
        x = x + hier_moe(h, w_gr[l], b_gr[l], w_er[l], b_er[l], w_gate[l], w_up[l], w_down[l])
    return rms_norm(x, ln_final)


def setup_inputs(seed: int = 0) -> dict:
    key = jax.random.key(seed)
    ks = jax.random.split(key, 24)
    L, D, F, E = DEPTH, D_MODEL, D_EXPERT, N_EXPERTS

    def nrm(k, shape, scale):
        return jax.random.normal(k, shape, jnp.float32) * scale

    return {
        'x_prompt': nrm(ks[0], (BATCH, SEQ, D), 1.0),
        'x_sample': nrm(ks[1], (DEC_BATCH, DEC_SEQ, D), 1.0),
        'rel_bias': nrm(ks[2], (NUM_BUCKETS, HB), 0.5),
        'ln_mix': 1.0 + nrm(ks[3], (L, D), 0.02),
        'w_in': nrm(ks[4], (L, D, N_IN), D ** -0.5),
        'ln_cq': 1.0 + nrm(ks[5], (L, D_CQ), 0.02),
        'ln_ckv': 1.0 + nrm(ks[6], (L, D_C), 0.02),
        'w_uq': nrm(ks[7], (L, D_CQ, HA * (D_NOPE + D_ROPE)), D_CQ ** -0.5),
        'w_ukv': nrm(ks[8], (L, D_C, HA * (D_NOPE + D_V)), D_C ** -0.5),
        'w_oa': nrm(ks[9], (L, HA * D_V, D), (HA * D_V) ** -0.5),
        'sink': nrm(ks[10], (L, HB), 0.5),
        'w_ob': nrm(ks[11], (L, HB * HD, D), (HB * HD) ** -0.5),
        'w_out': nrm(ks[12], (L, D, D), D ** -0.5),
        'ln_ffn': 1.0 + nrm(ks[13], (L, D), 0.02),
        'w_gr': nrm(ks[14], (L, D, N_GROUPS), D ** -0.5),
        'b_gr': nrm(ks[15], (L, N_GROUPS), 0.01),
        'w_er': nrm(ks[16], (L, D, E), D ** -0.5),
        'b_er': nrm(ks[17], (L, E), 0.01),
        'w_gate': nrm(ks[18], (L, E, D, F), D ** -0.5),
        'w_up': nrm(ks[19], (L, E, D, F), D ** -0.5),
        'w_down': nrm(ks[20], (L, E, F, D), F ** -0.5),
        'ln_final': 1.0 + nrm(ks[21], (D,), 0.02),
    }


def reference(x_prompt, x_sample, rel_bias, ln_mix, w_in, ln_cq, ln_ckv, w_uq, w_ukv, w_oa,
              sink, w_ob, w_out, ln_ffn, w_gr, b_gr, w_er, b_er, w_gate, w_up, w_down, ln_final):
    y_prompt = encoder_trunk(x_prompt, rel_bias, ln_mix, w_in, ln_cq, ln_ckv, w_uq, w_ukv, w_oa,
                             sink, w_ob, w_out, ln_ffn, w_gr, b_gr, w_er, b_er, w_gate, w_up,
                             w_down, ln_final)
    y_sample = encoder_trunk(x_sample, rel_bias, ln_mix, w_in, ln_cq, ln_ckv, w_uq, w_ukv, w_oa,
                             sink, w_ob, w_out, ln_ffn, w_gr, b_gr, w_er, b_er, w_gate, w_up,
                             w_down, ln_final)
    return (y_prompt, y_sample)
```

```python
import functools
import math

import jax
import jax.numpy as jnp
from jax import lax
from jax.experimental import pallas as pl
from jax.experimental.pallas import tpu as pltpu

D_MODEL = 1024
DEPTH = 2
HA = 8
D_NOPE = 64
D_ROPE = 32
D_V = 64
D_CQ = 384
D_C = 256
ROPE_BASE = 10000.0
MLA_SCALE = (D_NOPE + D_ROPE) ** -0.5
HB = 8
KVB = 2
GQ = HB // KVB
HD = 64
WINDOW = 128
WB = 128
WIN_SCALE = HD ** -0.5
NUM_BUCKETS = 32
MAX_DIST = 128
N_GROUPS = 4
EPG = 8
N_EXPERTS = N_GROUPS * EPG
TOP_K = 2
D_EXPERT = 256
EPS = 1e-6
NEG = -1e30

LANES = 128
SUBLANES = 8
HALF_ROPE = D_ROPE // 2
MOE_ROWS = 256
VMEM_LIMIT = 48 * 1024 * 1024

C_CQ = 0
C_CKV = C_CQ + D_CQ
C_KR = C_CKV + D_C
C_QB = C_KR + LANES
C_KB = C_QB + HB * HD
C_VB = C_KB + 2 * KVB * HD
N_MIX = C_VB + 2 * KVB * HD

F32 = jnp.float32
BF16 = jnp.bfloat16


def _rms(x, g):
    ms = jnp.mean(x * x, axis=-1, keepdims=True)
    return x * lax.rsqrt(ms + EPS) * g


def _dot(a, b):
    return jnp.dot(a, b, preferred_element_type=F32)


def _dot_nt(a, b):
    return lax.dot_general(a, b, (((1,), (1,)), ((), ())), preferred_element_type=F32)


def _rope(t, cos, sin_a, sin_b):
    return (t * cos + pltpu.roll(t, LANES - HALF_ROPE, 1) * sin_a
            + pltpu.roll(t, HALF_ROPE, 1) * sin_b)


def _mixer_in_kernel(x_ref, lnmix_ref, win_ref, lncq_ref, lnckv_ref, wuq_ref, wuk_ref, wuv_ref,
                     cos_ref, sina_ref, sinb_ref,
                     qa_ref, ka_ref, va_ref, qb_ref, kb_ref, vb_ref):
    h = _rms(x_ref[...], lnmix_ref[...]).astype(BF16)
    cq = _dot(h, win_ref[:, C_CQ:C_CKV])
    ckv = _dot(h, win_ref[:, C_CKV:C_KR])
    kr = _dot(h, win_ref[:, C_KR:C_QB])
    qb_ref[...] = _dot(h, win_ref[:, C_QB:C_KB]).astype(BF16)
    kb_ref[...] = _dot(h, win_ref[:, C_KB:C_VB]).astype(BF16)
    vb_ref[...] = _dot(h, win_ref[:, C_VB:N_MIX]).astype(BF16)
    cqn = _rms(cq, lncq_ref[...]).astype(BF16)
    ckvn = _rms(ckv, lnckv_ref[...]).astype(BF16)
    q = _dot(cqn, wuq_ref[...])
    k = _dot(ckvn, wuk_ref[...])
    va_ref[...] = _dot(ckvn, wuv_ref[...]).astype(BF16)
    cos, sin_a, sin_b = cos_ref[...], sina_ref[...], sinb_ref[...]
    kr_rot = _rope(kr, cos, sin_a, sin_b)
    for hd in range(HA):
        sl = slice(hd * LANES, (hd + 1) * LANES)
        qa_ref[:, sl] = (_rope(q[:, sl], cos, sin_a, sin_b) * MLA_SCALE).astype(BF16)
        ka_ref[:, sl] = (k[:, sl] + kr_rot).astype(BF16)


def _mixer_in(x, p, tabs, seq, tm):
    t = x.shape[0]
    nt = t // tm
    spt = seq // tm
    row = lambda i: (i, 0)
    full = lambda i: (0, 0)
    tab = lambda i: (i % spt, 0)
    out_w = (HA * LANES, HA * LANES, HA * D_V, HB * HD, 2 * KVB * HD, 2 * KVB * HD)
    return pl.pallas_call(
        _mixer_in_kernel,
        grid=(nt,),
        in_specs=[
            pl.BlockSpec((tm, D_MODEL), row),
            pl.BlockSpec((1, D_MODEL), full),
            pl.BlockSpec((D_MODEL, N_MIX), full),
            pl.BlockSpec((1, D_CQ), full),
            pl.BlockSpec((1, D_C), full),
            pl.BlockSpec((D_CQ, HA * LANES), full),
            pl.BlockSpec((D_C, HA * LANES), full),
            pl.BlockSpec((D_C, HA * D_V), full),
            pl.BlockSpec((tm, LANES), tab),
            pl.BlockSpec((tm, LANES), tab),
            pl.BlockSpec((tm, LANES), tab),
        ],
        out_specs=[pl.BlockSpec((tm, w), row) for w in out_w],
        out_shape=[jax.ShapeDtypeStruct((t, w), BF16) for w in out_w],
        compiler_params=pltpu.CompilerParams(
            dimension_semantics=("parallel",), vmem_limit_bytes=VMEM_LIMIT),
        name="mixer_in",
    )(x, p["ln_mix"], p["w_mix"], p["ln_cq"], p["ln_ckv"], p["w_uq"], p["w_uk"], p["w_uv"],
      tabs["cos"], tabs["sin_a"], tabs["sin_b"])


def _mla_kernel(q_ref, k_ref, v_ref, o_ref):
    v = v_ref[...]
    outs = []
    for hh in range(2):
        sl = slice(hh * LANES, (hh + 1) * LANES)
        s = _dot_nt(q_ref[:, sl], k_ref[:, sl])
        m = jnp.max(s, axis=1, keepdims=True)
        e = jnp.exp(s - m)
        l = jnp.sum(e, axis=1, keepdims=True)
        outs.append(_dot(e.astype(BF16), v) / l)
    lane = lax.broadcasted_iota(jnp.int32, outs[0].shape, 1)
    o_ref[...] = jnp.where(lane < D_V, outs[0], outs[1]).astype(BF16)


def _mla_attention(qa, ka, va, batch, seq, tq):
    t = qa.shape[0]
    nq = seq // tq
    return pl.pallas_call(
        _mla_kernel,
        grid=(batch, HA // 2, nq),
        in_specs=[
            pl.BlockSpec((tq, 2 * LANES), lambda b, j, i: (b * nq + i, j)),
            pl.BlockSpec((seq, 2 * LANES), lambda b, j, i: (b, j)),
            pl.BlockSpec((seq, 2 * D_V), lambda b, j, i: (b, j)),
        ],
        out_specs=pl.BlockSpec((tq, 2 * D_V), lambda b, j, i: (b * nq + i, j)),
        out_shape=jax.ShapeDtypeStruct((t, HA * D_V), BF16),
        compiler_params=pltpu.CompilerParams(
            dimension_semantics=("parallel", "parallel", "parallel"),
            vmem_limit_bytes=VMEM_LIMIT),
        name="mla_attention",
    )(qa, ka, va)


def _window_kernel(q_ref, k_ref, v_ref, bias_ref, sink_ref, o_ref, *, seq):
    nb = seq // WB
    n = pl.program_id(1)
    blocks = (jnp.maximum(n - 1, 0), n, jnp.minimum(n + 1, nb - 1))
    starts = [pl.multiple_of(b * WB, WB) for b in blocks]
    kw = jnp.concatenate([k_ref[pl.ds(s, WB), :] for s in starts], axis=0)
    vw = jnp.concatenate([v_ref[pl.ds(s, WB), :] for s in starts], axis=0)
    kpos = (n - 1) * WB + lax.broadcasted_iota(jnp.int32, (1, 3 * WB), 1)
    valid = (kpos >= 0) & (kpos < seq)
    lane = lax.broadcasted_iota(jnp.int32, (WB, LANES), 1)
    lo = lane < HD
    for kv in range(KVB):
        sl = slice(kv * LANES, (kv + 1) * LANES)
        kd, vd = kw[:, sl], vw[:, sl]
        qs = []
        for jj in range(GQ // 2):
            qp = q_ref[:, (kv * 2 + jj) * LANES:(kv * 2 + jj + 1) * LANES]
            qs.append(jnp.where(lo, qp, jnp.zeros_like(qp)))
            qs.append(jnp.where(lo, jnp.zeros_like(qp), qp))
        q4 = jnp.concatenate(qs, axis=0)
        s = _dot_nt(q4, kd) + bias_ref[kv]
        s = jnp.where(valid, s, NEG)
        sink = sink_ref[kv]
        m = jnp.maximum(jnp.max(s, axis=1, keepdims=True), sink)
        e = jnp.exp(s - m)
        denom = jnp.sum(e, axis=1, keepdims=True) + jnp.exp(sink - m)
        o = _dot(e.astype(BF16), vd) / denom
        for jj in range(GQ // 2):
            o_lo = o[(2 * jj) * WB:(2 * jj + 1) * WB]
            o_hi = o[(2 * jj + 1) * WB:(2 * jj + 2) * WB]
            o_ref[:, (kv * 2 + jj) * LANES:(kv * 2 + jj + 1) * LANES] = (
                jnp.where(lo, o_lo, o_hi).astype(BF16))


def _window_attention(qb, kb, vb, bias, sink, batch, seq):
    t = qb.shape[0]
    nb = seq // WB
    return pl.pallas_call(
        functools.partial(_window_kernel, seq=seq),
        grid=(batch, nb),
        in_specs=[
            pl.BlockSpec((WB, HB * HD), lambda b, n: (b * nb + n, 0)),
            pl.BlockSpec((seq, 2 * KVB * HD), lambda b, n: (b, 0)),
            pl.BlockSpec((seq, 2 * KVB * HD), lambda b, n: (b, 0)),
            pl.BlockSpec((KVB, GQ * WB, 3 * WB), lambda b, n: (0, 0, 0)),
            pl.BlockSpec((KVB, GQ * WB, 1), lambda b, n: (0, 0, 0)),
        ],
        out_specs=pl.BlockSpec((WB, HB * HD), lambda b, n: (b * nb + n, 0)),
        out_shape=jax.ShapeDtypeStruct((t, HB * HD), BF16),
        compiler_params=pltpu.CompilerParams(
            dimension_semantics=("parallel", "parallel"), vmem_limit_bytes=VMEM_LIMIT),
        name="window_attention",
    )(qb, kb, vb, bias, sink)


def _mixer_out_kernel(x_ref, oa_ref, ob_ref, lnmix_ref, wg_ref, woa_ref, wob_ref, wout_ref,
                      lnffn_ref, wrh_ref, wrl_ref, br_ref,
                      xn_ref, h2_ref, rf_ref, ri_ref, cnt_ref, carry_ref):
    i = pl.program_id(0)

    @pl.when(i == 0)
    def _():
        carry_ref[...] = jnp.zeros_like(carry_ref)

    x = x_ref[...]
    tm = x.shape[0]
    h = _rms(x, lnmix_ref[...]).astype(BF16)
    g = _dot(h, wg_ref[...])
    ya = _dot(oa_ref[...], woa_ref[...])
    yb = _dot(ob_ref[...], wob_ref[...])
    merged = jax.nn.sigmoid(g[:, :D_MODEL]) * ya + jax.nn.sigmoid(g[:, D_MODEL:]) * yb
    xn = x + _dot(merged.astype(BF16), wout_ref[...])
    xn_ref[...] = xn
    h2 = _rms(xn, lnffn_ref[...])
    h2_ref[...] = h2

    h2_hi = h2.astype(BF16)
    h2_lo = (h2 - h2_hi.astype(F32)).astype(BF16)
    logits = (_dot(h2_hi, wrh_ref[...]) + _dot(h2_hi, wrl_ref[...]) + _dot(h2_lo, wrh_ref[...])
              + br_ref[...])

    lane = lax.broadcasted_iota(jnp.int32, (tm, LANES), 1)
    lanef = lane.astype(F32)
    big = float(LANES)
    is_g = (lane >= N_EXPERTS) & (lane < N_EXPERTS + N_GROUPS)
    gl = jnp.where(is_g, logits, NEG)
    gmax = jnp.max(gl, axis=1, keepdims=True)
    grp = jnp.min(jnp.where(gl == gmax, lanef, big), axis=1, keepdims=True) - N_EXPERTS
    gsum = jnp.sum(jnp.where(is_g, jnp.exp(gl - gmax), 0.0), axis=1, keepdims=True)
    grp_w = 1.0 / gsum
    is_e = (lanef >= grp * EPG) & (lanef < grp * EPG + EPG)
    el = jnp.where(is_e, logits, NEG)
    v1 = jnp.max(el, axis=1, keepdims=True)
    i1 = jnp.min(jnp.where((el == v1) & is_e, lanef, big), axis=1, keepdims=True)
    a1 = lanef == i1
    el2 = jnp.where(a1, NEG, el)
    v2 = jnp.max(el2, axis=1, keepdims=True)
    i2 = jnp.min(jnp.where((el2 == v2) & is_e & (~a1), lanef, big), axis=1, keepdims=True)
    a2 = lanef == i2
    ex = jnp.exp(v2 - v1)
    p1 = 1.0 / (1.0 + ex)
    gate1 = grp_w * p1
    gate2 = grp_w * (ex * p1)

    a1f = a1.astype(F32)
    a2f = a2.astype(F32)
    cnt = a1f + a2f
    r_i = lax.broadcasted_iota(jnp.int32, (tm, tm), 0)
    c_i = lax.broadcasted_iota(jnp.int32, (tm, tm), 1)
    tri = jnp.where(r_i > c_i, 1.0, 0.0).astype(BF16)
    before = _dot(tri, cnt.astype(BF16)) + carry_ref[...]
    rank1 = jnp.sum(a1f * before, axis=1, keepdims=True)
    rank2 = jnp.sum(a2f * before, axis=1, keepdims=True)
    carry = carry_ref[...] + jnp.sum(cnt, axis=0, keepdims=True)
    carry_ref[...] = carry
    cnt_ref[...] = jnp.broadcast_to(carry, cnt_ref.shape)

    rf_ref[...] = jnp.where(lane == 0, gate1, jnp.where(lane == 1, gate2, 0.0))
    ri = jnp.where(lane == 0, i1, jnp.where(lane == 1, i2,
                                            jnp.where(lane == 2, rank1,
                                                      jnp.where(lane == 3, rank2, 0.0))))
    ri_ref[...] = ri.astype(jnp.int32)


def _mixer_out(x, oa, ob, p, tm):
    t = x.shape[0]
    nt = t // tm
    row = lambda i: (i, 0)
    full = lambda i: (0, 0)
    return pl.pallas_call(
        _mixer_out_kernel,
        grid=(nt,),
        in_specs=[
            pl.BlockSpec((tm, D_MODEL), row),
            pl.BlockSpec((tm, HA * D_V), row),
            pl.BlockSpec((tm, HB * HD), row),
            pl.BlockSpec((1, D_MODEL), full),
            pl.BlockSpec((D_MODEL, 2 * D_MODEL), full),
            pl.BlockSpec((HA * D_V, D_MODEL), full),
            pl.BlockSpec((HB * HD, D_MODEL), full),
            pl.BlockSpec((D_MODEL, D_MODEL), full),
            pl.BlockSpec((1, D_MODEL), full),
            pl.BlockSpec((D_MODEL, LANES), full),
            pl.BlockSpec((D_MODEL, LANES), full),
            pl.BlockSpec((1, LANES), full),
        ],
        out_specs=[
            pl.BlockSpec((tm, D_MODEL), row),
            pl.BlockSpec((tm, D_MODEL), row),
            pl.BlockSpec((tm, LANES), row),
            pl.BlockSpec((tm, LANES), row),
            pl.BlockSpec((8, LANES), full),
        ],
        out_shape=[
            jax.ShapeDtypeStruct((t, D_MODEL), F32),
            jax.ShapeDtypeStruct((t, D_MODEL), F32),
            jax.ShapeDtypeStruct((t, LANES), F32),
            jax.ShapeDtypeStruct((t, LANES), jnp.int32),
            jax.ShapeDtypeStruct((8, LANES), F32),
        ],
        scratch_shapes=[pltpu.VMEM((1, LANES), F32)],
        compiler_params=pltpu.CompilerParams(
            dimension_semantics=("arbitrary",), vmem_limit_bytes=VMEM_LIMIT),
        name="mixer_out",
    )(x, oa, ob, p["ln_mix"], p["w_g"], p["w_oa"], p["w_ob"], p["w_out"], p["ln_ffn"],
      p["w_r_hi"], p["w_r_lo"], p["b_r"])


_ZERO_ROWS = MOE_ROWS // 2
_ZERO_SIZES = tuple(_ZERO_ROWS >> s for s in range(5))


def _dispatch_kernel(zstart_ref, zlen_ref, nu_ref, dest_ref, h2_ref, xb_ref, zbuf_ref, sem, zsem, *, td):
    i = pl.program_id(0)
    base = i * td

    def row_copy(t, d):
        return pltpu.make_async_copy(h2_ref.at[pl.ds(base + t, 1)], xb_ref.at[pl.ds(d, 1)], sem)

    def issue(t, c):
        row_copy(t, dest_ref[0, 0, 2 * t]).start()
        row_copy(t, dest_ref[0, 0, 2 * t + 1]).start()
        return c

    lax.fori_loop(0, td, issue, 0)

    def zero_copy(start, size):
        return pltpu.make_async_copy(zbuf_ref.at[pl.ds(0, size)], xb_ref.at[pl.ds(start, size)], zsem)

    def zero_pass(do):
        def per_expert(e, c):
            start = zstart_ref[e]
            head = (-start) & (SUBLANES - 1)
            for j in range(SUBLANES - 1):
                @pl.when(j < head)
                def _(row=start + j):
                    do(zero_copy(row, 1))

            start = start + head
            left = zlen_ref[e] - head
            for size in _ZERO_SIZES:
                take = (left & size) != 0

                @pl.when(take)
                def _(start=start, size=size):
                    do(zero_copy(pl.multiple_of(start, SUBLANES), size))

                start = start + jnp.where(take, size, 0)
            return c

        lax.fori_loop(0, N_EXPERTS, per_expert, 0)

        def unused_block(b, c):
            for part in range(MOE_ROWS // _ZERO_ROWS):
                row = pl.multiple_of(b * MOE_ROWS + part * _ZERO_ROWS, _ZERO_ROWS)
                do(zero_copy(row, _ZERO_ROWS))
            return c

        lax.fori_loop(nu_ref[0], xb_ref.shape[0] // MOE_ROWS, unused_block, 0)

    @pl.when(i == 0)
    def _():
        zbuf_ref[...] = jnp.zeros_like(zbuf_ref)
        zero_pass(lambda cp: cp.start())
        zero_pass(lambda cp: cp.wait())

    pltpu.make_async_copy(h2_ref.at[pl.ds(0, 2 * td)], xb_ref.at[pl.ds(0, 2 * td)], sem).wait()


def _moe_dispatch(h2, dest, zstart, zlen, n_used, n_rows, td):
    t = h2.shape[0]
    nt = t // td
    dest3 = dest.reshape(nt, 1, 2 * td)
    return pl.pallas_call(
        functools.partial(_dispatch_kernel, td=td),
        grid_spec=pltpu.PrefetchScalarGridSpec(
            num_scalar_prefetch=3,
            grid=(nt,),
            in_specs=[
                pl.BlockSpec((1, 1, 2 * td), lambda i, zs, zl, nu: (i, 0, 0),
                             memory_space=pltpu.SMEM),
                pl.BlockSpec(memory_space=pl.ANY),
            ],
            out_specs=pl.BlockSpec(memory_space=pl.ANY),
            scratch_shapes=[pltpu.VMEM((_ZERO_ROWS, D_MODEL), F32),
                            pltpu.SemaphoreType.DMA, pltpu.SemaphoreType.DMA],
        ),
        out_shape=jax.ShapeDtypeStruct((n_rows, D_MODEL), F32),
        compiler_params=pltpu.CompilerParams(dimension_semantics=("arbitrary",)),
        name="moe_dispatch",
    )(zstart, zlen, n_used, dest3, h2)


def _experts_kernel(be_ref, nu_ref, x_ref, wgu_ref, wd_ref, y_ref):
    b = pl.program_id(0)

    @pl.when(b < nu_ref[0])
    def _():
        x = x_ref[...].astype(BF16)
        au = _dot(x, wgu_ref[0])
        a, u = au[:, :D_EXPERT], au[:, D_EXPERT:]
        hid = (a * jax.nn.sigmoid(a) * u).astype(BF16)
        y_ref[...] = _dot(hid, wd_ref[0])

    @pl.when(b >= nu_ref[0])
    def _():
        y_ref[...] = jnp.zeros_like(y_ref)


def _moe_experts(xb, blk_expert, n_used, w_gu, w_d):
    n_rows = xb.shape[0]
    n_blocks = n_rows // MOE_ROWS
    live = lambda b, be, nu: (jnp.minimum(b, nu[0] - 1), 0)
    return pl.pallas_call(
        _experts_kernel,
        grid_spec=pltpu.PrefetchScalarGridSpec(
            num_scalar_prefetch=2,
            grid=(n_blocks,),
            in_specs=[
                pl.BlockSpec((MOE_ROWS, D_MODEL), live),
                pl.BlockSpec((1, D_MODEL, 2 * D_EXPERT), lambda b, be, nu: (be[b], 0, 0)),
                pl.BlockSpec((1, D_EXPERT, D_MODEL), lambda b, be, nu: (be[b], 0, 0)),
            ],
            out_specs=pl.BlockSpec((MOE_ROWS, D_MODEL), lambda b, be, nu: (b, 0)),
        ),
        out_shape=jax.ShapeDtypeStruct((n_rows, D_MODEL), F32),
        compiler_params=pltpu.CompilerParams(
            dimension_semantics=("arbitrary",), vmem_limit_bytes=VMEM_LIMIT),
        name="moe_experts",
    )(blk_expert, n_used, xb, w_gu, w_d)


def _combine_kernel(dest_ref, dnext_ref, x_ref, rf_ref, lnf_ref, yb_ref, o_ref, ybuf_ref, sems,
                    *, tc, final_norm):
    i = pl.program_id(0)
    n = pl.num_programs(0)
    slot = i % 2

    def row_copy(src_row, s, k, t):
        return pltpu.make_async_copy(yb_ref.at[pl.ds(src_row, 1)],
                                     ybuf_ref.at[s, k, pl.ds(t, 1)], sems.at[s])

    def issue(d_ref, s):
        def body(t, c):
            row_copy(d_ref[0, 0, 2 * t], s, 0, t).start()
            row_copy(d_ref[0, 0, 2 * t + 1], s, 1, t).start()
            return c

        lax.fori_loop(0, tc, body, 0)

    @pl.when(i == 0)
    def _():
        issue(dest_ref, slot)

    @pl.when(i + 1 < n)
    def _():
        issue(dnext_ref, 1 - slot)

    for k in range(TOP_K):
        pltpu.make_async_copy(yb_ref.at[pl.ds(0, tc)], ybuf_ref.at[slot, k], sems.at[slot]).wait()

    rf = rf_ref[...]
    out = x_ref[...] + rf[:, 0:1] * ybuf_ref[slot, 0] + rf[:, 1:2] * ybuf_ref[slot, 1]
    if final_norm:
        out = _rms(out, lnf_ref[...])
    o_ref[...] = out


def _moe_combine(x, rf, dest, yb, ln_final, tc, final_norm):
    t = x.shape[0]
    nt = t // tc
    dest3 = dest.reshape(nt, 1, 2 * tc)
    row = lambda i: (i, 0)
    return pl.pallas_call(
        functools.partial(_combine_kernel, tc=tc, final_norm=final_norm),
        grid=(nt,),
        in_specs=[
            pl.BlockSpec((1, 1, 2 * tc), lambda i: (i, 0, 0), memory_space=pltpu.SMEM),
            pl.BlockSpec((1, 1, 2 * tc), lambda i: (jnp.minimum(i + 1, nt - 1), 0, 0),
                         memory_space=pltpu.SMEM),
            pl.BlockSpec((tc, D_MODEL), row),
            pl.BlockSpec((tc, LANES), row),
            pl.BlockSpec((1, D_MODEL), lambda i: (0, 0)),
            pl.BlockSpec(memory_space=pl.ANY),
        ],
        out_specs=pl.BlockSpec((tc, D_MODEL), row),
        out_shape=jax.ShapeDtypeStruct((t, D_MODEL), F32),
        scratch_shapes=[pltpu.VMEM((2, TOP_K, tc, D_MODEL), F32),
                        pltpu.SemaphoreType.DMA((2,))],
        compiler_params=pltpu.CompilerParams(
            dimension_semantics=("arbitrary",), vmem_limit_bytes=VMEM_LIMIT),
        name="moe_combine",
    )(dest3, dest3, x, rf, ln_final, yb)


def _rope_tables(seq):
    inv = ROPE_BASE ** (-jnp.arange(0, D_ROPE, 2, dtype=F32) / D_ROPE)
    ang = jnp.arange(seq, dtype=F32)[:, None] * inv[None, :]
    cos, sin = jnp.cos(ang), jnp.sin(ang)
    z = lambda w: jnp.zeros((seq, w), F32)
    tail = LANES - D_NOPE - D_ROPE
    return {
        "cos": jnp.concatenate([jnp.ones((seq, D_NOPE), F32), cos, cos, z(tail)], axis=1),
        "sin_a": jnp.concatenate([z(D_NOPE), -sin, z(HALF_ROPE), z(tail)], axis=1),
        "sin_b": jnp.concatenate([z(D_NOPE), z(HALF_ROPE), sin, z(tail)], axis=1),
    }


def _t5_bucket(rel):
    half = NUM_BUCKETS // 2
    max_exact = half // 2
    n = jnp.abs(rel)
    nf = jnp.maximum(n, 1).astype(F32)
    large = max_exact + (jnp.log(nf / max_exact) / math.log(MAX_DIST / max_exact)
                         * (half - max_exact)).astype(jnp.int32)
    large = jnp.minimum(large, half - 1)
    return jnp.where(rel > 0, half, 0) + jnp.where(n < max_exact, n, large)


def _window_bias(rel_bias):
    qi = jnp.arange(WB)[:, None]
    kj = jnp.arange(3 * WB)[None, :]
    rel = kj - WB - qi
    bias = rel_bias[_t5_bucket(rel)].astype(F32)
    bias = jnp.where((jnp.abs(rel) <= WINDOW)[:, :, None], bias, NEG)
    return jnp.transpose(bias, (2, 0, 1)).reshape(KVB, GQ * WB, 3 * WB)


def _pack_layer(l, w_in, ln_mix, ln_cq, ln_ckv, w_uq, w_ukv, w_oa, sink, w_ob, w_out, ln_ffn,
                w_gr, b_gr, w_er, b_er, w_gate, w_up, w_down):
    wi = w_in[l]
    o = 0
    c_q, o = wi[:, o:o + D_CQ], o + D_CQ
    c_kv, o = wi[:, o:o + D_C], o + D_C
    k_r, o = wi[:, o:o + D_ROPE], o + D_ROPE
    q_b, o = wi[:, o:o + HB * HD], o + HB * HD
    k_b, o = wi[:, o:o + KVB * HD], o + KVB * HD
    v_b, o = wi[:, o:o + KVB * HD], o + KVB * HD
    g_ab = wi[:, o:]
    zc = lambda rows, w: jnp.zeros((rows, w), F32)
    dup = lambda w: jnp.concatenate([w[:, kv * HD:(kv + 1) * HD] for kv in range(KVB) for _ in (0, 1)],
                                    axis=1)
    kr_blk = jnp.concatenate([zc(D_MODEL, D_NOPE), k_r, zc(D_MODEL, LANES - D_NOPE - D_ROPE)], axis=1)
    w_mix = jnp.concatenate([c_q, c_kv, kr_blk, q_b * WIN_SCALE, dup(k_b), dup(v_b)], axis=1)

    uq = w_uq[l].reshape(D_CQ, HA, D_NOPE + D_ROPE)
    uq = jnp.pad(uq, ((0, 0), (0, 0), (0, LANES - D_NOPE - D_ROPE))).reshape(D_CQ, HA * LANES)
    ukv = w_ukv[l].reshape(D_C, HA, D_NOPE + D_V)
    uk = jnp.pad(ukv[:, :, :D_NOPE], ((0, 0), (0, 0), (0, LANES - D_NOPE))).reshape(D_C, HA * LANES)
    uv = ukv[:, :, D_NOPE:].reshape(D_C, HA * D_V)

    w_r = jnp.concatenate([w_er[l], w_gr[l], zc(D_MODEL, LANES - N_EXPERTS - N_GROUPS)], axis=1)
    w_r_hi = w_r.astype(BF16)
    b_r = jnp.concatenate([b_er[l], b_gr[l], jnp.zeros((LANES - N_EXPERTS - N_GROUPS,), F32)])
    sink_col = jnp.repeat(sink[l].astype(F32).reshape(KVB, GQ), WB, axis=1)[:, :, None]
    return {
        "ln_mix": ln_mix[l][None, :], "w_mix": w_mix.astype(BF16),
        "ln_cq": ln_cq[l][None, :], "ln_ckv": ln_ckv[l][None, :],
        "w_uq": uq.astype(BF16), "w_uk": uk.astype(BF16), "w_uv": uv.astype(BF16),
        "w_g": g_ab.astype(BF16), "w_oa": w_oa[l].astype(BF16), "w_ob": w_ob[l].astype(BF16),
        "w_out": w_out[l].astype(BF16), "ln_ffn": ln_ffn[l][None, :],
        "w_r_hi": w_r_hi, "w_r_lo": (w_r - w_r_hi.astype(F32)).astype(BF16), "b_r": b_r[None, :],
        "sink": sink_col,
        "w_gu": jnp.concatenate([w_gate[l], w_up[l]], axis=-1).astype(BF16),
        "w_d": w_down[l].astype(BF16),
    }


def _row_tile(n, pref):
    tile = pref
    while n % tile:
        tile //= 2
    return tile


def _trunk(x, batch, seq, tabs, bias, layers, ln_final):
    t = x.shape[0]
    tm = _row_tile(seq, 512)
    tq = _row_tile(seq, 512)
    tc = _row_tile(t, 256)
    n_blocks = (t * TOP_K) // MOE_ROWS + N_EXPERTS
    n_rows = n_blocks * MOE_ROWS
    for l, p in enumerate(layers):
        qa, ka, va, qb, kb, vb = _mixer_in(x, p, tabs, seq, tm)
        oa = _mla_attention(qa, ka, va, batch, seq, tq)
        ob = _window_attention(qb, kb, vb, bias, p["sink"], batch, seq)
        x, h2, rf, ri, cnt = _mixer_out(x, oa, ob, p, tm)
        counts = cnt[0, :N_EXPERTS].astype(jnp.int32)
        padded = ((counts + MOE_ROWS - 1) // MOE_ROWS) * MOE_ROWS
        pad_end = jnp.cumsum(padded)
        pad_start = pad_end - padded
        dest = (pad_start[ri[:, 0:TOP_K]] + ri[:, TOP_K:2 * TOP_K]).reshape(-1)
        blk_expert = jnp.minimum(
            jnp.searchsorted(pad_end, jnp.arange(n_blocks, dtype=jnp.int32) * MOE_ROWS, side="right"),
            N_EXPERTS - 1).astype(jnp.int32)
        n_used = (pad_end[-1:] // MOE_ROWS).astype(jnp.int32)
        xb = _moe_dispatch(h2, dest, (pad_start + counts).astype(jnp.int32),
                           (padded - counts).astype(jnp.int32), n_used, n_rows, tc)
        yb = _moe_experts(xb, blk_expert, n_used, p["w_gu"], p["w_d"])
        x = _moe_combine(x, rf, dest, yb, ln_final, tc, final_norm=(l == len(layers) - 1))
    return x


def kernel(x_prompt, x_sample, rel_bias, ln_mix, w_in, ln_cq, ln_ckv, w_uq, w_ukv, w_oa, sink, w_ob,
           w_out, ln_ffn, w_gr, b_gr, w_er, b_er, w_gate, w_up, w_down, ln_final):
    bp, seq, d = x_prompt.shape
    bs = x_sample.shape[0]
    assert x_sample.shape[1] == seq and seq % WB == 0 and seq >= 3 * WB
    tabs = _rope_tables(seq)
    bias = _window_bias(rel_bias)
    layers = [_pack_layer(l, w_in, ln_mix, ln_cq, ln_ckv, w_uq, w_ukv, w_oa, sink, w_ob, w_out,
                          ln_ffn, w_gr, b_gr, w_er, b_er, w_gate, w_up, w_down)
              for l in range(w_in.shape[0])]
    x = jnp.concatenate([x_prompt.reshape(bp * seq, d), x_sample.reshape(bs * seq, d)], axis=0)
    y = _trunk(x, bp + bs, seq, tabs, bias, layers, ln_final[None, :])
    return (y[:bp * seq].reshape(bp, seq, d), y[bp * seq:].reshape(bs, seq, d))
```

```python
import functools
import math

import jax
import jax.numpy as jnp
from jax import lax
from jax.experimental import pallas as pl
from jax.experimental.pallas import tpu as pltpu

D_MODEL = 1024
DEPTH = 2
HA = 8
D_NOPE = 64
D_ROPE = 32
D_V = 64
D_CQ = 384
D_C = 256
ROPE_BASE = 10000.0
MLA_SCALE = (D_NOPE + D_ROPE) ** -0.5
HB = 8
KVB = 2
GQ = HB // KVB
HD = 64
WINDOW = 128
WB = 128
WIN_SCALE = HD ** -0.5
NUM_BUCKETS = 32
MAX_DIST = 128
N_GROUPS = 4
EPG = 8
N_EXPERTS = N_GROUPS * EPG
TOP_K = 2
D_EXPERT = 256
EPS = 1e-6
NEG = -1e30
LOG2E = math.log2(math.e)

LANES = 128
SUBLANES = 8
HALF_ROPE = D_ROPE // 2
MOE_ROWS = 256
VMEM_LIMIT = 48 * 1024 * 1024

C_CQ = 0
C_CKV = C_CQ + D_CQ
C_KR = C_CKV + D_C
C_QB = C_KR + LANES
C_KB = C_QB + HB * HD
N_MIX = C_KB + 2 * KVB * HD

F32 = jnp.float32
BF16 = jnp.bfloat16


def _rms(x, g):
    ms = jnp.mean(x * x, axis=-1, keepdims=True)
    return x * lax.rsqrt(ms + EPS) * g


def _dot(a, b):
    return jnp.dot(a, b, preferred_element_type=F32)


def _dot_nt(a, b):
    return lax.dot_general(a, b, (((1,), (1,)), ((), ())), preferred_element_type=F32)


def _rope(t, cos, sin_a, sin_b):
    return (t * cos + pltpu.roll(t, LANES - HALF_ROPE, 1) * sin_a
            + pltpu.roll(t, HALF_ROPE, 1) * sin_b)


def _mixer_in_kernel(x_ref, lnmix_ref, win_ref, wvbt_ref, lncq_ref, lnckv_ref, wuq_ref, wuk_ref,
                     wuvt_ref, cos_ref, sina_ref, sinb_ref,
                     qa_ref, ka_ref, vat_ref, qb_ref, kb_ref, vbt_ref):
    h = _rms(x_ref[...], lnmix_ref[...]).astype(BF16)
    cq = _dot(h, win_ref[:, C_CQ:C_CKV])
    ckv = _dot(h, win_ref[:, C_CKV:C_KR])
    kr = _dot(h, win_ref[:, C_KR:C_QB])
    qb_ref[...] = _dot(h, win_ref[:, C_QB:C_KB]).astype(BF16)
    kb_ref[...] = _dot(h, win_ref[:, C_KB:N_MIX]).astype(BF16)
    vbt = _dot_nt(wvbt_ref[...], h).astype(BF16)
    for c in range(vbt_ref.shape[0]):
        vbt_ref[c] = vbt[:, c * WB:(c + 1) * WB]
    cqn = _rms(cq, lncq_ref[...]).astype(BF16)
    ckvn = _rms(ckv, lnckv_ref[...]).astype(BF16)
    q = _dot(cqn, wuq_ref[...])
    k = _dot(ckvn, wuk_ref[...])
    vat_ref[...] = _dot_nt(wuvt_ref[...], ckvn).astype(BF16)
    cos, sin_a, sin_b = cos_ref[...], sina_ref[...], sinb_ref[...]
    kr_rot = _rope(kr, cos, sin_a, sin_b)
    for hd in range(HA):
        sl = slice(hd * LANES, (hd + 1) * LANES)
        qa_ref[:, sl] = (_rope(q[:, sl], cos, sin_a, sin_b) * (MLA_SCALE * LOG2E)).astype(BF16)
        ka_ref[:, sl] = (k[:, sl] + kr_rot).astype(BF16)


def _mixer_in(x, p, tabs, seq, tm):
    t = x.shape[0]
    nt = t // tm
    spt = seq // tm
    row = lambda i: (i, 0)
    full = lambda i: (0, 0)
    tab = lambda i: (i % spt, 0)
    rows = lambda w: (pl.BlockSpec((tm, w), row), jax.ShapeDtypeStruct((t, w), BF16))
    outs = [
        rows(HA * LANES),
        rows(HA * LANES),
        (pl.BlockSpec((HA * D_V, tm), lambda i: (0, i)),
         jax.ShapeDtypeStruct((HA * D_V, t), BF16)),
        rows(HB * HD),
        rows(2 * KVB * HD),
        (pl.BlockSpec((tm // WB, KVB * HD, WB), lambda i: (i, 0, 0)),
         jax.ShapeDtypeStruct((t // WB, KVB * HD, WB), BF16)),
    ]
    out_specs = [o[0] for o in outs]
    out_shape = [o[1] for o in outs]
    return pl.pallas_call(
        _mixer_in_kernel,
        grid=(nt,),
        in_specs=[
            pl.BlockSpec((tm, D_MODEL), row),
            pl.BlockSpec((1, D_MODEL), full),
            pl.BlockSpec((D_MODEL, N_MIX), full),
            pl.BlockSpec((KVB * HD, D_MODEL), full),
            pl.BlockSpec((1, D_CQ), full),
            pl.BlockSpec((1, D_C), full),
            pl.BlockSpec((D_CQ, HA * LANES), full),
            pl.BlockSpec((D_C, HA * LANES), full),
            pl.BlockSpec((HA * D_V, D_C), full),
            pl.BlockSpec((tm, LANES), tab),
            pl.BlockSpec((tm, LANES), tab),
            pl.BlockSpec((tm, LANES), tab),
        ],
        out_specs=out_specs,
        out_shape=out_shape,
        compiler_params=pltpu.CompilerParams(
            dimension_semantics=("parallel",), vmem_limit_bytes=VMEM_LIMIT),
        name="mixer_in",
    )(x, p["ln_mix"], p["w_mix"], p["w_vbt"], p["ln_cq"], p["ln_ckv"], p["w_uq"], p["w_uk"], p["w_uvt"],
      tabs["cos"], tabs["sin_a"], tabs["sin_b"])


def _mla_kernel(q_ref, k_ref, vt_ref, o_ref):
    outs = []
    for hh in range(2):
        sl = slice(hh * LANES, (hh + 1) * LANES)
        st = _dot_nt(k_ref[:, sl], q_ref[:, sl])
        m = jnp.max(st, axis=0, keepdims=True)
        e = jnp.exp2(st - m)
        l = jnp.sum(e, axis=0, keepdims=True)
        ot = _dot(vt_ref[hh * D_V:(hh + 1) * D_V, :], e.astype(BF16))
        outs.append(ot / l)
    o_ref[...] = jnp.concatenate(outs, axis=0).T.astype(BF16)


def _mla_attention(qa, ka, va, batch, seq, tq):
    t = qa.shape[0]
    nq = seq // tq
    return pl.pallas_call(
        _mla_kernel,
        grid=(batch, HA // 2, nq),
        in_specs=[
            pl.BlockSpec((tq, 2 * LANES), lambda b, j, i: (b * nq + i, j)),
            pl.BlockSpec((seq, 2 * LANES), lambda b, j, i: (b, j)),
            pl.BlockSpec((2 * D_V, seq), lambda b, j, i: (j, b)),
        ],
        out_specs=pl.BlockSpec((tq, 2 * D_V), lambda b, j, i: (b * nq + i, j)),
        out_shape=jax.ShapeDtypeStruct((t, HA * D_V), BF16),
        compiler_params=pltpu.CompilerParams(
            dimension_semantics=("parallel", "parallel", "parallel"),
            vmem_limit_bytes=VMEM_LIMIT),
        name="mla_attention",
    )(qa, ka, va)


def _window_kernel(q_ref, k_ref, vt_ref, bias_ref, sink_ref, o_ref, *, seq, qblocks):
    nb = seq // WB
    lane = lax.broadcasted_iota(jnp.int32, (WB, LANES), 1)
    lo = lane < HD
    for u in range(qblocks):
        n = pl.program_id(1) * qblocks + u
        rows = slice(u * WB, (u + 1) * WB)
        blocks = (jnp.maximum(n - 1, 0), n, jnp.minimum(n + 1, nb - 1))
        kw = jnp.concatenate([k_ref[pl.ds(pl.multiple_of(b * WB, WB), WB), :] for b in blocks],
                             axis=0)
        vts = [vt_ref[b] for b in blocks]
        kpos = (n - 1) * WB + lax.broadcasted_iota(jnp.int32, (3 * WB, GQ * WB), 0)
        valid = (kpos >= 0) & (kpos < seq)
        heads = []
        for kv in range(KVB):
            kd = kw[:, kv * LANES:(kv + 1) * LANES]
            qs = []
            for jj in range(GQ // 2):
                qp = q_ref[rows, (kv * 2 + jj) * LANES:(kv * 2 + jj + 1) * LANES]
                qs.append(jnp.where(lo, qp, jnp.zeros_like(qp)))
                qs.append(jnp.where(lo, jnp.zeros_like(qp), qp))
            q4 = jnp.concatenate(qs, axis=0)
            st = _dot_nt(kd, q4) + bias_ref[kv]
            st = jnp.where(valid, st, NEG)
            sink = sink_ref[kv]
            m = jnp.maximum(jnp.max(st, axis=0, keepdims=True), sink)
            e = jnp.exp2(st - m)
            denom = jnp.sum(e, axis=0, keepdims=True) + jnp.exp2(sink - m)
            vt = jnp.concatenate([v[kv * HD:(kv + 1) * HD, :] for v in vts], axis=1)
            ot = _dot(vt, e.astype(BF16)) / denom
            heads += [ot[:, g * WB:(g + 1) * WB] for g in range(GQ)]
        o_ref[rows, :] = jnp.concatenate(heads, axis=0).T.astype(BF16)


def _window_attention(qb, kb, vbt, bias, sink, batch, seq):
    t = qb.shape[0]
    nb = seq // WB
    qblocks = 4 if nb % 4 == 0 else 1
    nb //= qblocks
    return pl.pallas_call(
        functools.partial(_window_kernel, seq=seq, qblocks=qblocks),
        grid=(batch, nb),
        in_specs=[
            pl.BlockSpec((qblocks * WB, HB * HD), lambda b, n: (b * nb + n, 0)),
            pl.BlockSpec((seq, 2 * KVB * HD), lambda b, n: (b, 0)),
            pl.BlockSpec((seq // WB, KVB * HD, WB), lambda b, n: (b, 0, 0)),
            pl.BlockSpec((KVB, 3 * WB, GQ * WB), lambda b, n: (0, 0, 0)),
            pl.BlockSpec((KVB, 1, GQ * WB), lambda b, n: (0, 0, 0)),
        ],
        out_specs=pl.BlockSpec((qblocks * WB, HB * HD), lambda b, n: (b * nb + n, 0)),
        out_shape=jax.ShapeDtypeStruct((t, HB * HD), BF16),
        compiler_params=pltpu.CompilerParams(
            dimension_semantics=("parallel", "parallel"), vmem_limit_bytes=VMEM_LIMIT),
        name="window_attention",
    )(qb, kb, vbt, bias, sink)


def _mixer_out_kernel(x_ref, oa_ref, ob_ref, lnmix_ref, wg_ref, woa_ref, wob_ref, wout_ref,
                      lnffn_ref, wrh_ref, wrl_ref, br_ref,
                      xn_ref, h2_ref, rf_ref, ri_ref, cnt_ref, carry_ref):
    i = pl.program_id(0)

    @pl.when(i == 0)
    def _():
        carry_ref[...] = jnp.zeros_like(carry_ref)

    x = x_ref[...]
    tm = x.shape[0]
    h = _rms(x, lnmix_ref[...]).astype(BF16)
    g = _dot(h, wg_ref[...])
    ya = _dot(oa_ref[...], woa_ref[...])
    yb = _dot(ob_ref[...], wob_ref[...])
    merged = jax.nn.sigmoid(g[:, :D_MODEL]) * ya + jax.nn.sigmoid(g[:, D_MODEL:]) * yb
    xn = x + _dot(merged.astype(BF16), wout_ref[...])
    xn_ref[...] = xn
    h2 = _rms(xn, lnffn_ref[...])
    h2_ref[...] = h2

    h2_hi = h2.astype(BF16)
    h2_lo = (h2 - h2_hi.astype(F32)).astype(BF16)
    logits = (_dot(h2_hi, wrh_ref[...]) + _dot(h2_hi, wrl_ref[...]) + _dot(h2_lo, wrh_ref[...])
              + br_ref[...])

    lane = lax.broadcasted_iota(jnp.int32, (tm, LANES), 1)
    lanef = lane.astype(F32)
    big = float(LANES)
    is_g = (lane >= N_EXPERTS) & (lane < N_EXPERTS + N_GROUPS)
    gl = jnp.where(is_g, logits, NEG)
    gmax = jnp.max(gl, axis=1, keepdims=True)
    grp = jnp.min(jnp.where(gl == gmax, lanef, big), axis=1, keepdims=True) - N_EXPERTS
    gsum = jnp.sum(jnp.where(is_g, jnp.exp(gl - gmax), 0.0), axis=1, keepdims=True)
    grp_w = 1.0 / gsum
    is_e = (lanef >= grp * EPG) & (lanef < grp * EPG + EPG)
    el = jnp.where(is_e, logits, NEG)
    v1 = jnp.max(el, axis=1, keepdims=True)
    i1 = jnp.min(jnp.where((el == v1) & is_e, lanef, big), axis=1, keepdims=True)
    a1 = lanef == i1
    el2 = jnp.where(a1, NEG, el)
    v2 = jnp.max(el2, axis=1, keepdims=True)
    i2 = jnp.min(jnp.where((el2 == v2) & is_e & (~a1), lanef, big), axis=1, keepdims=True)
    a2 = lanef == i2
    ex = jnp.exp(v2 - v1)
    p1 = 1.0 / (1.0 + ex)
    gate1 = grp_w * p1
    gate2 = grp_w * (ex * p1)

    a1f = a1.astype(F32)
    a2f = a2.astype(F32)
    cnt = a1f + a2f
    r_i = lax.broadcasted_iota(jnp.int32, (tm, tm), 0)
    c_i = lax.broadcasted_iota(jnp.int32, (tm, tm), 1)
    tri = jnp.where(r_i > c_i, 1.0, 0.0).astype(BF16)
    before = _dot(tri, cnt.astype(BF16)) + carry_ref[...]
    rank1 = jnp.sum(a1f * before, axis=1, keepdims=True)
    rank2 = jnp.sum(a2f * before, axis=1, keepdims=True)
    carry = carry_ref[...] + jnp.sum(cnt, axis=0, keepdims=True)
    carry_ref[...] = carry
    cnt_ref[...] = jnp.broadcast_to(carry, cnt_ref.shape)

    rf_ref[...] = jnp.where(lane == 0, gate1, jnp.where(lane == 1, gate2, 0.0))
    ri = jnp.where(lane == 0, i1, jnp.where(lane == 1, i2,
                                            jnp.where(lane == 2, rank1,
                                                      jnp.where(lane == 3, rank2, 0.0))))
    ri_ref[...] = ri.astype(jnp.int32)


def _mixer_out(x, oa, ob, p, tm):
    t = x.shape[0]
    nt = t // tm
    row = lambda i: (i, 0)
    full = lambda i: (0, 0)
    return pl.pallas_call(
        _mixer_out_kernel,
        grid=(nt,),
        in_specs=[
            pl.BlockSpec((tm, D_MODEL), row),
            pl.BlockSpec((tm, HA * D_V), row),
            pl.BlockSpec((tm, HB * HD), row),
            pl.BlockSpec((1, D_MODEL), full),
            pl.BlockSpec((D_MODEL, 2 * D_MODEL), full),
            pl.BlockSpec((HA * D_V, D_MODEL), full),
            pl.BlockSpec((HB * HD, D_MODEL), full),
            pl.BlockSpec((D_MODEL, D_MODEL), full),
            pl.BlockSpec((1, D_MODEL), full),
            pl.BlockSpec((D_MODEL, LANES), full),
            pl.BlockSpec((D_MODEL, LANES), full),
            pl.BlockSpec((1, LANES), full),
        ],
        out_specs=[
            pl.BlockSpec((tm, D_MODEL), row),
            pl.BlockSpec((tm, D_MODEL), row),
            pl.BlockSpec((tm, LANES), row),
            pl.BlockSpec((tm, LANES), row),
            pl.BlockSpec((8, LANES), full),
        ],
        out_shape=[
            jax.ShapeDtypeStruct((t, D_MODEL), F32),
            jax.ShapeDtypeStruct((t, D_MODEL), F32),
            jax.ShapeDtypeStruct((t, LANES), F32),
            jax.ShapeDtypeStruct((t, LANES), jnp.int32),
            jax.ShapeDtypeStruct((8, LANES), F32),
        ],
        scratch_shapes=[pltpu.VMEM((1, LANES), F32)],
        compiler_params=pltpu.CompilerParams(
            dimension_semantics=("arbitrary",), vmem_limit_bytes=VMEM_LIMIT),
        name="mixer_out",
    )(x, oa, ob, p["ln_mix"], p["w_g"], p["w_oa"], p["w_ob"], p["w_out"], p["ln_ffn"],
      p["w_r_hi"], p["w_r_lo"], p["b_r"])


_ZERO_ROWS = MOE_ROWS // 2
_ZERO_SIZES = tuple(_ZERO_ROWS >> s for s in range(5))


def _dispatch_kernel(zstart_ref, zlen_ref, nu_ref, dest_ref, h2_ref, xb_ref, zbuf_ref, sem, zsem, *, td):
    i = pl.program_id(0)

    def row_copy(t, d):
        return pltpu.make_async_copy(h2_ref.at[pl.ds(t, 1)], xb_ref.at[pl.ds(d, 1)], sem)

    def issue(t, c):
        row_copy(t, dest_ref[0, 0, 2 * t]).start()
        row_copy(t, dest_ref[0, 0, 2 * t + 1]).start()
        return c

    lax.fori_loop(0, td, issue, 0)

    def zero_copy(start, size):
        return pltpu.make_async_copy(zbuf_ref.at[pl.ds(0, size)], xb_ref.at[pl.ds(start, size)], zsem)

    def zero_pass(do):
        def per_expert(e, c):
            start = zstart_ref[e]
            head = (-start) & (SUBLANES - 1)
            for j in range(SUBLANES - 1):
                @pl.when(j < head)
                def _(row=start + j):
                    do(zero_copy(row, 1))

            start = start + head
            left = zlen_ref[e] - head
            for size in _ZERO_SIZES:
                take = (left & size) != 0

                @pl.when(take)
                def _(start=start, size=size):
                    do(zero_copy(pl.multiple_of(start, SUBLANES), size))

                start = start + jnp.where(take, size, 0)
            return c

        lax.fori_loop(0, N_EXPERTS, per_expert, 0)

        def unused_block(b, c):
            for part in range(MOE_ROWS // _ZERO_ROWS):
                row = pl.multiple_of(b * MOE_ROWS + part * _ZERO_ROWS, _ZERO_ROWS)
                do(zero_copy(row, _ZERO_ROWS))
            return c

        lax.fori_loop(nu_ref[0], xb_ref.shape[0] // MOE_ROWS, unused_block, 0)

    @pl.when(i == 0)
    def _():
        zbuf_ref[...] = jnp.zeros_like(zbuf_ref)
        zero_pass(lambda cp: cp.start())
        zero_pass(lambda cp: cp.wait())

    for _ in range(TOP_K):
        pltpu.make_async_copy(h2_ref, xb_ref.at[pl.ds(0, td)], sem).wait()


def _moe_dispatch(h2, dest, zstart, zlen, n_used, n_rows, td):
    t = h2.shape[0]
    nt = t // td
    dest3 = dest.reshape(nt, 1, 2 * td)
    return pl.pallas_call(
        functools.partial(_dispatch_kernel, td=td),
        grid_spec=pltpu.PrefetchScalarGridSpec(
            num_scalar_prefetch=3,
            grid=(nt,),
            in_specs=[
                pl.BlockSpec((1, 1, 2 * td), lambda i, zs, zl, nu: (i, 0, 0),
                             memory_space=pltpu.SMEM),
                pl.BlockSpec((td, D_MODEL), lambda i, zs, zl, nu: (i, 0)),
            ],
            out_specs=pl.BlockSpec(memory_space=pl.ANY),
            scratch_shapes=[pltpu.VMEM((_ZERO_ROWS, D_MODEL), F32),
                            pltpu.SemaphoreType.DMA, pltpu.SemaphoreType.DMA],
        ),
        out_shape=jax.ShapeDtypeStruct((n_rows, D_MODEL), F32),
        compiler_params=pltpu.CompilerParams(dimension_semantics=("arbitrary",)),
        name="moe_dispatch",
    )(zstart, zlen, n_used, dest3, h2)


def _experts_kernel(be_ref, nu_ref, x_ref, wgu_ref, wd_ref, y_ref):
    b = pl.program_id(0)

    @pl.when(b < nu_ref[0])
    def _():
        x = x_ref[...].astype(BF16)
        au = _dot(x, wgu_ref[0])
        a, u = au[:, :D_EXPERT], au[:, D_EXPERT:]
        hid = (a * jax.nn.sigmoid(a) * u).astype(BF16)
        y_ref[...] = _dot(hid, wd_ref[0])

    @pl.when(b >= nu_ref[0])
    def _():
        y_ref[...] = jnp.zeros_like(y_ref)


def _moe_experts(xb, blk_expert, n_used, w_gu, w_d):
    n_rows = xb.shape[0]
    n_blocks = n_rows // MOE_ROWS
    live = lambda b, be, nu: (jnp.minimum(b, nu[0] - 1), 0)
    return pl.pallas_call(
        _experts_kernel,
        grid_spec=pltpu.PrefetchScalarGridSpec(
            num_scalar_prefetch=2,
            grid=(n_blocks,),
            in_specs=[
                pl.BlockSpec((MOE_ROWS, D_MODEL), live),
                pl.BlockSpec((1, D_MODEL, 2 * D_EXPERT), lambda b, be, nu: (be[b], 0, 0)),
                pl.BlockSpec((1, D_EXPERT, D_MODEL), lambda b, be, nu: (be[b], 0, 0)),
            ],
            out_specs=pl.BlockSpec((MOE_ROWS, D_MODEL), lambda b, be, nu: (b, 0)),
        ),
        out_shape=jax.ShapeDtypeStruct((n_rows, D_MODEL), F32),
        compiler_params=pltpu.CompilerParams(
            dimension_semantics=("arbitrary",), vmem_limit_bytes=VMEM_LIMIT),
        name="moe_experts",
    )(blk_expert, n_used, xb, w_gu, w_d)


def _combine_kernel(dest_ref, dnext_ref, x_ref, rf_ref, lnf_ref, yb_ref, o_ref, ybuf_ref, sems,
                    *, tc, final_norm):
    i = pl.program_id(0)
    n = pl.num_programs(0)
    slot = i % 2

    def row_copy(src_row, s, k, t):
        return pltpu.make_async_copy(yb_ref.at[pl.ds(src_row, 1)],
                                     ybuf_ref.at[s, k, pl.ds(t, 1)], sems.at[s])

    def issue(d_ref, s):
        def body(t, c):
            row_copy(d_ref[0, 0, 2 * t], s, 0, t).start()
            row_copy(d_ref[0, 0, 2 * t + 1], s, 1, t).start()
            return c

        lax.fori_loop(0, tc, body, 0)

    @pl.when(i == 0)
    def _():
        issue(dest_ref, slot)

    @pl.when(i + 1 < n)
    def _():
        issue(dnext_ref, 1 - slot)

    for k in range(TOP_K):
        pltpu.make_async_copy(yb_ref.at[pl.ds(0, tc)], ybuf_ref.at[slot, k], sems.at[slot]).wait()

    rf = rf_ref[...]
    out = x_ref[...] + rf[:, 0:1] * ybuf_ref[slot, 0] + rf[:, 1:2] * ybuf_ref[slot, 1]
    if final_norm:
        out = _rms(out, lnf_ref[...])
    o_ref[...] = out


def _moe_combine(x, rf, dest, yb, ln_final, tc, final_norm):
    t = x.shape[0]
    nt = t // tc
    dest3 = dest.reshape(nt, 1, 2 * tc)
    row = lambda i: (i, 0)
    return pl.pallas_call(
        functools.partial(_combine_kernel, tc=tc, final_norm=final_norm),
        grid=(nt,),
        in_specs=[
            pl.BlockSpec((1, 1, 2 * tc), lambda i: (i, 0, 0), memory_space=pltpu.SMEM),
            pl.BlockSpec((1, 1, 2 * tc), lambda i: (jnp.minimum(i + 1, nt - 1), 0, 0),
                         memory_space=pltpu.SMEM),
            pl.BlockSpec((tc, D_MODEL), row),
            pl.BlockSpec((tc, LANES), row),
            pl.BlockSpec((1, D_MODEL), lambda i: (0, 0)),
            pl.BlockSpec(memory_space=pl.ANY),
        ],
        out_specs=pl.BlockSpec((tc, D_MODEL), row),
        out_shape=jax.ShapeDtypeStruct((t, D_MODEL), F32),
        scratch_shapes=[pltpu.VMEM((2, TOP_K, tc, D_MODEL), F32),
                        pltpu.SemaphoreType.DMA((2,))],
        compiler_params=pltpu.CompilerParams(
            dimension_semantics=("arbitrary",), vmem_limit_bytes=VMEM_LIMIT),
        name="moe_combine",
    )(dest3, dest3, x, rf, ln_final, yb)


def _rope_tables(seq):
    inv = ROPE_BASE ** (-jnp.arange(0, D_ROPE, 2, dtype=F32) / D_ROPE)
    ang = jnp.arange(seq, dtype=F32)[:, None] * inv[None, :]
    cos, sin = jnp.cos(ang), jnp.sin(ang)
    z = lambda w: jnp.zeros((seq, w), F32)
    tail = LANES - D_NOPE - D_ROPE
    return {
        "cos": jnp.concatenate([jnp.ones((seq, D_NOPE), F32), cos, cos, z(tail)], axis=1),
        "sin_a": jnp.concatenate([z(D_NOPE), -sin, z(HALF_ROPE), z(tail)], axis=1),
        "sin_b": jnp.concatenate([z(D_NOPE), z(HALF_ROPE), sin, z(tail)], axis=1),
    }


def _t5_bucket(rel):
    half = NUM_BUCKETS // 2
    max_exact = half // 2
    n = jnp.abs(rel)
    nf = jnp.maximum(n, 1).astype(F32)
    large = max_exact + (jnp.log(nf / max_exact) / math.log(MAX_DIST / max_exact)
                         * (half - max_exact)).astype(jnp.int32)
    large = jnp.minimum(large, half - 1)
    return jnp.where(rel > 0, half, 0) + jnp.where(n < max_exact, n, large)


def _window_bias(rel_bias):
    qi = jnp.arange(WB)[:, None]
    kj = jnp.arange(3 * WB)[None, :]
    rel = kj - WB - qi
    bias = rel_bias[_t5_bucket(rel)].astype(F32) * LOG2E
    bias = jnp.where((jnp.abs(rel) <= WINDOW)[:, :, None], bias, NEG)
    bias = jnp.transpose(bias, (2, 0, 1)).reshape(KVB, GQ * WB, 3 * WB)
    return jnp.swapaxes(bias, 1, 2)


def _pack_layer(l, w_in, ln_mix, ln_cq, ln_ckv, w_uq, w_ukv, w_oa, sink, w_ob, w_out, ln_ffn,
                w_gr, b_gr, w_er, b_er, w_gate, w_up, w_down):
    wi = w_in[l]
    o = 0
    c_q, o = wi[:, o:o + D_CQ], o + D_CQ
    c_kv, o = wi[:, o:o + D_C], o + D_C
    k_r, o = wi[:, o:o + D_ROPE], o + D_ROPE
    q_b, o = wi[:, o:o + HB * HD], o + HB * HD
    k_b, o = wi[:, o:o + KVB * HD], o + KVB * HD
    v_b, o = wi[:, o:o + KVB * HD], o + KVB * HD
    g_ab = wi[:, o:]
    zc = lambda rows, w: jnp.zeros((rows, w), F32)
    dup = lambda w: jnp.concatenate([w[:, kv * HD:(kv + 1) * HD] for kv in range(KVB) for _ in (0, 1)],
                                    axis=1)
    kr_blk = jnp.concatenate([zc(D_MODEL, D_NOPE), k_r, zc(D_MODEL, LANES - D_NOPE - D_ROPE)], axis=1)
    w_mix = jnp.concatenate([c_q, c_kv, kr_blk, q_b * (WIN_SCALE * LOG2E), dup(k_b)], axis=1)

    uq = w_uq[l].reshape(D_CQ, HA, D_NOPE + D_ROPE)
    uq = jnp.pad(uq, ((0, 0), (0, 0), (0, LANES - D_NOPE - D_ROPE))).reshape(D_CQ, HA * LANES)
    ukv = w_ukv[l].reshape(D_C, HA, D_NOPE + D_V)
    uk = jnp.pad(ukv[:, :, :D_NOPE], ((0, 0), (0, 0), (0, LANES - D_NOPE))).reshape(D_C, HA * LANES)
    uv = ukv[:, :, D_NOPE:].reshape(D_C, HA * D_V)

    w_r = jnp.concatenate([w_er[l], w_gr[l], zc(D_MODEL, LANES - N_EXPERTS - N_GROUPS)], axis=1)
    w_r_hi = w_r.astype(BF16)
    b_r = jnp.concatenate([b_er[l], b_gr[l], jnp.zeros((LANES - N_EXPERTS - N_GROUPS,), F32)])
    sink_row = jnp.repeat(sink[l].astype(F32).reshape(KVB, GQ) * LOG2E, WB, axis=1)[:, None, :]
    return {
        "ln_mix": ln_mix[l][None, :], "w_mix": w_mix.astype(BF16), "w_vbt": v_b.T.astype(BF16),
        "ln_cq": ln_cq[l][None, :], "ln_ckv": ln_ckv[l][None, :],
        "w_uq": uq.astype(BF16), "w_uk": uk.astype(BF16), "w_uvt": uv.T.astype(BF16),
        "w_g": g_ab.astype(BF16), "w_oa": w_oa[l].astype(BF16), "w_ob": w_ob[l].astype(BF16),
        "w_out": w_out[l].astype(BF16), "ln_ffn": ln_ffn[l][None, :],
        "w_r_hi": w_r_hi, "w_r_lo": (w_r - w_r_hi.astype(F32)).astype(BF16), "b_r": b_r[None, :],
        "sink": sink_row,
        "w_gu": jnp.concatenate([w_gate[l], w_up[l]], axis=-1).astype(BF16),
        "w_d": w_down[l].astype(BF16),
    }


def _row_tile(n, pref):
    tile = pref
    while n % tile:
        tile //= 2
    return tile


def _trunk(x, batch, seq, tabs, bias, layers, ln_final):
    t = x.shape[0]
    tm = _row_tile(seq, 512)
    tq = _row_tile(seq, 512)
    tc = _row_tile(t, 256)
    n_blocks = (t * TOP_K) // MOE_ROWS + N_EXPERTS
    n_rows = n_blocks * MOE_ROWS
    for l, p in enumerate(layers):
        qa, ka, vat, qb, kb, vbt = _mixer_in(x, p, tabs, seq, tm)
        oa = _mla_attention(qa, ka, vat, batch, seq, tq)
        ob = _window_attention(qb, kb, vbt, bias, p["sink"], batch, seq)
        x, h2, rf, ri, cnt = _mixer_out(x, oa, ob, p, tm)
        counts = cnt[0, :N_EXPERTS].astype(jnp.int32)
        padded = ((counts + MOE_ROWS - 1) // MOE_ROWS) * MOE_ROWS
        pad_end = jnp.cumsum(padded)
        pad_start = pad_end - padded
        dest = (pad_start[ri[:, 0:TOP_K]] + ri[:, TOP_K:2 * TOP_K]).reshape(-1)
        blk_row = jnp.arange(n_blocks, dtype=jnp.int32)[:, None] * MOE_ROWS
        blk_expert = jnp.minimum(jnp.sum((pad_end[None, :] <= blk_row).astype(jnp.int32), axis=1),
                                 N_EXPERTS - 1)
        n_used = (pad_end[-1:] // MOE_ROWS).astype(jnp.int32)
        xb = _moe_dispatch(h2, dest, (pad_start + counts).astype(jnp.int32),
                           (padded - counts).astype(jnp.int32), n_used, n_rows, tc)
        yb = _moe_experts(xb, blk_expert, n_used, p["w_gu"], p["w_d"])
        x = _moe_combine(x, rf, dest, yb, ln_final, tc, final_norm=(l == len(layers) - 1))
    return x


def kernel(x_prompt, x_sample, rel_bias, ln_mix, w_in, ln_cq, ln_ckv, w_uq, w_ukv, w_oa, sink, w_ob,
           w_out, ln_ffn, w_gr, b_gr, w_er, b_er, w_gate, w_up, w_down, ln_final):
    bp, seq, d = x_prompt.shape
    bs = x_sample.shape[0]
    assert x_sample.shape[1] == seq and seq % WB == 0 and seq >= 3 * WB
    tabs = _rope_tables(seq)
    bias = _window_bias(rel_bias)
    layers = [_pack_layer(l, w_in, ln_mix, ln_cq, ln_ckv, w_uq, w_ukv, w_oa, sink, w_ob, w_out,
                          ln_ffn, w_gr, b_gr, w_er, b_er, w_gate, w_up, w_down)
              for l in range(w_in.shape[0])]
    x = jnp.concatenate([x_prompt.reshape(bp * seq, d), x_sample.reshape(bs * seq, d)], axis=0)
    y = _trunk(x, bp + bs, seq, tabs, bias, layers, ln_final[None, :])
    return (y[:bp * seq].reshape(bp, seq, d), y[bp * seq:].reshape(bs, seq, d))
```

```python
import functools
import math

import jax
import jax.numpy as jnp
from jax import lax
from jax.experimental import pallas as pl
from jax.experimental.pallas import tpu as pltpu

D_MODEL = 1024
DEPTH = 2
HA = 8
D_NOPE = 64
D_ROPE = 32
D_V = 64
D_CQ = 384
D_C = 256
ROPE_BASE = 10000.0
MLA_SCALE = (D_NOPE + D_ROPE) ** -0.5
HB = 8
KVB = 2
GQ = HB // KVB
HD = 64
WINDOW = 128
WB = 128
WIN_SCALE = HD ** -0.5
NUM_BUCKETS = 32
MAX_DIST = 128
N_GROUPS = 4
EPG = 8
N_EXPERTS = N_GROUPS * EPG
TOP_K = 2
D_EXPERT = 256
EPS = 1e-6
NEG = -1e30
LOG2E = math.log2(math.e)

LANES = 128
SUBLANES = 8
HALF_ROPE = D_ROPE // 2
MOE_ROWS = 256
VMEM_LIMIT = 48 * 1024 * 1024

C_CQ = 0
C_CKV = C_CQ + D_CQ
C_KR = C_CKV + D_C
C_QB = C_KR + LANES
C_KB = C_QB + HB * HD
N_MIX = C_KB + 2 * KVB * HD

F32 = jnp.float32
BF16 = jnp.bfloat16
U32 = jnp.uint32
D_PACK = D_MODEL // 2


def _rms(x, g):
    ms = jnp.mean(x * x, axis=-1, keepdims=True)
    return x * lax.rsqrt(ms + EPS) * g


def _dot(a, b):
    return jnp.dot(a, b, preferred_element_type=F32)


def _dot_nt(a, b):
    return lax.dot_general(a, b, (((1,), (1,)), ((), ())), preferred_element_type=F32)


def _pack_rows(x):
    n = x.shape[1] // 2
    lo = lax.bitcast_convert_type(x[:, :n].astype(BF16).astype(F32), U32)
    hi = lax.bitcast_convert_type(x[:, n:].astype(BF16).astype(F32), U32)
    return hi | (lo >> 16)


def _unpack_rows(p):
    lo = lax.bitcast_convert_type(p << 16, F32)
    hi = lax.bitcast_convert_type(p & jnp.uint32(0xFFFF0000), F32)
    return lo, hi


def _x_operands(xs, tile):
    xa, xb = xs
    na = xa.shape[0] // tile
    rows = xa.shape[0] + (0 if xb is None else xb.shape[0])
    specs = (pl.BlockSpec((tile, D_MODEL), lambda i, *_: (jnp.minimum(i, na - 1), 0)),
             pl.BlockSpec((tile, D_MODEL), lambda i, *_: (jnp.maximum(i - na, 0), 0)))
    return (xa, xa if xb is None else xb), specs, na, rows


def _x_tile(xa_ref, xb_ref, na):
    return jnp.where(pl.program_id(0) < na, xa_ref[...], xb_ref[...])


def _rope(t, cos, sin_a, sin_b):
    return (t * cos + pltpu.roll(t, LANES - HALF_ROPE, 1) * sin_a
            + pltpu.roll(t, HALF_ROPE, 1) * sin_b)


def _mixer_in_kernel(xa_ref, xb_ref, lnmix_ref, win_ref, wvbt_ref, lncq_ref, lnckv_ref, wuq_ref,
                     wuk_ref, wuvt_ref, cos_ref, sina_ref, sinb_ref,
                     qa_ref, ka_ref, vat_ref, qb_ref, kb_ref, vbt_ref, *, na):
    h = _rms(_x_tile(xa_ref, xb_ref, na), lnmix_ref[...]).astype(BF16)
    cq = _dot(h, win_ref[:, C_CQ:C_CKV])
    ckv = _dot(h, win_ref[:, C_CKV:C_KR])
    kr = _dot(h, win_ref[:, C_KR:C_QB])
    qb_ref[...] = _dot(h, win_ref[:, C_QB:C_KB]).astype(BF16)
    kb_ref[...] = _dot(h, win_ref[:, C_KB:N_MIX]).astype(BF16)
    vbt = _dot_nt(wvbt_ref[...], h).astype(BF16)
    for c in range(vbt_ref.shape[0]):
        vbt_ref[c] = vbt[:, c * WB:(c + 1) * WB]
    cqn = _rms(cq, lncq_ref[...]).astype(BF16)
    ckvn = _rms(ckv, lnckv_ref[...]).astype(BF16)
    q = _dot(cqn, wuq_ref[...])
    k = _dot(ckvn, wuk_ref[...])
    vat_ref[...] = _dot_nt(wuvt_ref[...], ckvn).astype(BF16)
    cos, sin_a, sin_b = cos_ref[...], sina_ref[...], sinb_ref[...]
    kr_rot = _rope(kr, cos, sin_a, sin_b)
    for hd in range(HA):
        sl = slice(hd * LANES, (hd + 1) * LANES)
        qa_ref[:, sl] = (_rope(q[:, sl], cos, sin_a, sin_b) * (MLA_SCALE * LOG2E)).astype(BF16)
        ka_ref[:, sl] = (k[:, sl] + kr_rot).astype(BF16)


def _mixer_in(xs, p, tabs, seq, tm):
    x_ops, x_specs, na, t = _x_operands(xs, tm)
    nt = t // tm
    spt = seq // tm
    row = lambda i: (i, 0)
    full = lambda i: (0, 0)
    tab = lambda i: (i % spt, 0)
    rows = lambda w: (pl.BlockSpec((tm, w), row), jax.ShapeDtypeStruct((t, w), BF16))
    outs = [
        rows(HA * LANES),
        rows(HA * LANES),
        (pl.BlockSpec((HA * D_V, tm), lambda i: (0, i)),
         jax.ShapeDtypeStruct((HA * D_V, t), BF16)),
        rows(HB * HD),
        rows(2 * KVB * HD),
        (pl.BlockSpec((tm // WB, KVB * HD, WB), lambda i: (i, 0, 0)),
         jax.ShapeDtypeStruct((t // WB, KVB * HD, WB), BF16)),
    ]
    out_specs = [o[0] for o in outs]
    out_shape = [o[1] for o in outs]
    return pl.pallas_call(
        functools.partial(_mixer_in_kernel, na=na),
        grid=(nt,),
        in_specs=[
            *x_specs,
            pl.BlockSpec((1, D_MODEL), full),
            pl.BlockSpec((D_MODEL, N_MIX), full),
            pl.BlockSpec((KVB * HD, D_MODEL), full),
            pl.BlockSpec((1, D_CQ), full),
            pl.BlockSpec((1, D_C), full),
            pl.BlockSpec((D_CQ, HA * LANES), full),
            pl.BlockSpec((D_C, HA * LANES), full),
            pl.BlockSpec((HA * D_V, D_C), full),
            pl.BlockSpec((tm, LANES), tab),
            pl.BlockSpec((tm, LANES), tab),
            pl.BlockSpec((tm, LANES), tab),
        ],
        out_specs=out_specs,
        out_shape=out_shape,
        compiler_params=pltpu.CompilerParams(
            dimension_semantics=("parallel",), vmem_limit_bytes=VMEM_LIMIT),
        name="mixer_in",
    )(*x_ops, p["ln_mix"], p["w_mix"], p["w_vbt"], p["ln_cq"], p["ln_ckv"], p["w_uq"], p["w_uk"], p["w_uvt"],
      tabs["cos"], tabs["sin_a"], tabs["sin_b"])


def _mla_kernel(q_ref, k_ref, vt_ref, o_ref):
    outs = []
    for hh in range(2):
        sl = slice(hh * LANES, (hh + 1) * LANES)
        st = _dot_nt(k_ref[:, sl], q_ref[:, sl])
        m = jnp.max(st, axis=0, keepdims=True)
        e = jnp.exp2(st - m)
        l = jnp.sum(e, axis=0, keepdims=True)
        ot = _dot(vt_ref[hh * D_V:(hh + 1) * D_V, :], e.astype(BF16))
        outs.append(ot / l)
    o_ref[...] = jnp.concatenate(outs, axis=0).T.astype(BF16)


def _mla_attention(qa, ka, va, batch, seq, tq):
    t = qa.shape[0]
    nq = seq // tq
    return pl.pallas_call(
        _mla_kernel,
        grid=(batch, HA // 2, nq),
        in_specs=[
            pl.BlockSpec((tq, 2 * LANES), lambda b, j, i: (b * nq + i, j)),
            pl.BlockSpec((seq, 2 * LANES), lambda b, j, i: (b, j)),
            pl.BlockSpec((2 * D_V, seq), lambda b, j, i: (j, b)),
        ],
        out_specs=pl.BlockSpec((tq, 2 * D_V), lambda b, j, i: (b * nq + i, j)),
        out_shape=jax.ShapeDtypeStruct((t, HA * D_V), BF16),
        compiler_params=pltpu.CompilerParams(
            dimension_semantics=("parallel", "parallel", "parallel"),
            vmem_limit_bytes=VMEM_LIMIT),
        name="mla_attention",
    )(qa, ka, va)


def _window_kernel(q_ref, k_ref, vt_ref, bias_ref, sink_ref, o_ref, *, seq, qblocks):
    nb = seq // WB
    lane = lax.broadcasted_iota(jnp.int32, (WB, LANES), 1)
    lo = lane < HD
    for u in range(qblocks):
        n = pl.program_id(1) * qblocks + u
        rows = slice(u * WB, (u + 1) * WB)
        blocks = (jnp.maximum(n - 1, 0), n, jnp.minimum(n + 1, nb - 1))
        kw = jnp.concatenate([k_ref[pl.ds(pl.multiple_of(b * WB, WB), WB), :] for b in blocks],
                             axis=0)
        vts = [vt_ref[b] for b in blocks]
        kpos = (n - 1) * WB + lax.broadcasted_iota(jnp.int32, (3 * WB, GQ * WB), 0)
        valid = (kpos >= 0) & (kpos < seq)
        heads = []
        for kv in range(KVB):
            kd = kw[:, kv * LANES:(kv + 1) * LANES]
            qs = []
            for jj in range(GQ // 2):
                qp = q_ref[rows, (kv * 2 + jj) * LANES:(kv * 2 + jj + 1) * LANES]
                qs.append(jnp.where(lo, qp, jnp.zeros_like(qp)))
                qs.append(jnp.where(lo, jnp.zeros_like(qp), qp))
            q4 = jnp.concatenate(qs, axis=0)
            st = _dot_nt(kd, q4) + bias_ref[kv]
            st = jnp.where(valid, st, NEG)
            sink = sink_ref[kv]
            m = jnp.maximum(jnp.max(st, axis=0, keepdims=True), sink)
            e = jnp.exp2(st - m)
            denom = jnp.sum(e, axis=0, keepdims=True) + jnp.exp2(sink - m)
            vt = jnp.concatenate([v[kv * HD:(kv + 1) * HD, :] for v in vts], axis=1)
            ot = _dot(vt, e.astype(BF16)) / denom
            heads += [ot[:, g * WB:(g + 1) * WB] for g in range(GQ)]
        o_ref[rows, :] = jnp.concatenate(heads, axis=0).T.astype(BF16)


def _window_attention(qb, kb, vbt, bias, sink, batch, seq):
    t = qb.shape[0]
    nb = seq // WB
    qblocks = 4 if nb % 4 == 0 else 1
    nb //= qblocks
    return pl.pallas_call(
        functools.partial(_window_kernel, seq=seq, qblocks=qblocks),
        grid=(batch, nb),
        in_specs=[
            pl.BlockSpec((qblocks * WB, HB * HD), lambda b, n: (b * nb + n, 0)),
            pl.BlockSpec((seq, 2 * KVB * HD), lambda b, n: (b, 0)),
            pl.BlockSpec((seq // WB, KVB * HD, WB), lambda b, n: (b, 0, 0)),
            pl.BlockSpec((KVB, 3 * WB, GQ * WB), lambda b, n: (0, 0, 0)),
            pl.BlockSpec((KVB, 1, GQ * WB), lambda b, n: (0, 0, 0)),
        ],
        out_specs=pl.BlockSpec((qblocks * WB, HB * HD), lambda b, n: (b * nb + n, 0)),
        out_shape=jax.ShapeDtypeStruct((t, HB * HD), BF16),
        compiler_params=pltpu.CompilerParams(
            dimension_semantics=("parallel", "parallel"), vmem_limit_bytes=VMEM_LIMIT),
        name="window_attention",
    )(qb, kb, vbt, bias, sink)


def _mixer_out_kernel(xa_ref, xb_ref, oa_ref, ob_ref, lnmix_ref, wg_ref, woa_ref, wob_ref, wout_ref,
                      lnffn_ref, wrh_ref, wrl_ref, br_ref,
                      xn_ref, h2_ref, rf_ref, ri_ref, cnt_ref, carry_ref, *, na):
    i = pl.program_id(0)

    @pl.when(i == 0)
    def _():
        carry_ref[...] = jnp.zeros_like(carry_ref)

    x = _x_tile(xa_ref, xb_ref, na)
    tm = x.shape[0]
    h = _rms(x, lnmix_ref[...]).astype(BF16)
    g = _dot(h, wg_ref[...])
    ya = _dot(oa_ref[...], woa_ref[...])
    yb = _dot(ob_ref[...], wob_ref[...])
    merged = jax.nn.sigmoid(g[:, :D_MODEL]) * ya + jax.nn.sigmoid(g[:, D_MODEL:]) * yb
    xn = x + _dot(merged.astype(BF16), wout_ref[...])
    xn_ref[...] = xn
    h2 = _rms(xn, lnffn_ref[...])
    h2_ref[...] = _pack_rows(h2)

    h2_hi = h2.astype(BF16)
    h2_lo = (h2 - h2_hi.astype(F32)).astype(BF16)
    logits = (_dot(h2_hi, wrh_ref[...]) + _dot(h2_hi, wrl_ref[...]) + _dot(h2_lo, wrh_ref[...])
              + br_ref[...])

    lane = lax.broadcasted_iota(jnp.int32, (tm, LANES), 1)
    lanef = lane.astype(F32)
    big = float(LANES)
    is_g = (lane >= N_EXPERTS) & (lane < N_EXPERTS + N_GROUPS)
    gl = jnp.where(is_g, logits, NEG)
    gmax = jnp.max(gl, axis=1, keepdims=True)
    grp = jnp.min(jnp.where(gl == gmax, lanef, big), axis=1, keepdims=True) - N_EXPERTS
    gsum = jnp.sum(jnp.where(is_g, jnp.exp(gl - gmax), 0.0), axis=1, keepdims=True)
    grp_w = 1.0 / gsum
    is_e = (lanef >= grp * EPG) & (lanef < grp * EPG + EPG)
    el = jnp.where(is_e, logits, NEG)
    v1 = jnp.max(el, axis=1, keepdims=True)
    i1 = jnp.min(jnp.where((el == v1) & is_e, lanef, big), axis=1, keepdims=True)
    a1 = lanef == i1
    el2 = jnp.where(a1, NEG, el)
    v2 = jnp.max(el2, axis=1, keepdims=True)
    i2 = jnp.min(jnp.where((el2 == v2) & is_e & (~a1), lanef, big), axis=1, keepdims=True)
    a2 = lanef == i2
    ex = jnp.exp(v2 - v1)
    p1 = 1.0 / (1.0 + ex)
    gate1 = grp_w * p1
    gate2 = grp_w * (ex * p1)

    a1f = a1.astype(F32)
    a2f = a2.astype(F32)
    cnt = a1f + a2f
    r_i = lax.broadcasted_iota(jnp.int32, (tm, tm), 0)
    c_i = lax.broadcasted_iota(jnp.int32, (tm, tm), 1)
    tri = jnp.where(r_i > c_i, 1.0, 0.0).astype(BF16)
    before = _dot(tri, cnt.astype(BF16)) + carry_ref[...]
    rank1 = jnp.sum(a1f * before, axis=1, keepdims=True)
    rank2 = jnp.sum(a2f * before, axis=1, keepdims=True)
    carry = carry_ref[...] + jnp.sum(cnt, axis=0, keepdims=True)
    carry_ref[...] = carry
    cnt_ref[...] = jnp.broadcast_to(carry, cnt_ref.shape)

    rf_ref[...] = jnp.where(lane == 0, gate1, jnp.where(lane == 1, gate2, 0.0))
    ri = jnp.where(lane == 0, i1, jnp.where(lane == 1, i2,
                                            jnp.where(lane == 2, rank1,
                                                      jnp.where(lane == 3, rank2, 0.0))))
    ri_ref[...] = ri.astype(jnp.int32)


def _mixer_out(xs, oa, ob, p, tm):
    x_ops, x_specs, na, t = _x_operands(xs, tm)
    nt = t // tm
    row = lambda i: (i, 0)
    full = lambda i: (0, 0)
    return pl.pallas_call(
        functools.partial(_mixer_out_kernel, na=na),
        grid=(nt,),
        in_specs=[
            *x_specs,
            pl.BlockSpec((tm, HA * D_V), row),
            pl.BlockSpec((tm, HB * HD), row),
            pl.BlockSpec((1, D_MODEL), full),
            pl.BlockSpec((D_MODEL, 2 * D_MODEL), full),
            pl.BlockSpec((HA * D_V, D_MODEL), full),
            pl.BlockSpec((HB * HD, D_MODEL), full),
            pl.BlockSpec((D_MODEL, D_MODEL), full),
            pl.BlockSpec((1, D_MODEL), full),
            pl.BlockSpec((D_MODEL, LANES), full),
            pl.BlockSpec((D_MODEL, LANES), full),
            pl.BlockSpec((1, LANES), full),
        ],
        out_specs=[
            pl.BlockSpec((tm, D_MODEL), row),
            pl.BlockSpec((tm, D_PACK), row),
            pl.BlockSpec((tm, LANES), row),
            pl.BlockSpec((tm, LANES), row),
            pl.BlockSpec((8, LANES), full),
        ],
        out_shape=[
            jax.ShapeDtypeStruct((t, D_MODEL), F32),
            jax.ShapeDtypeStruct((t, D_PACK), U32),
            jax.ShapeDtypeStruct((t, LANES), F32),
            jax.ShapeDtypeStruct((t, LANES), jnp.int32),
            jax.ShapeDtypeStruct((8, LANES), F32),
        ],
        scratch_shapes=[pltpu.VMEM((1, LANES), F32)],
        compiler_params=pltpu.CompilerParams(
            dimension_semantics=("arbitrary",), vmem_limit_bytes=VMEM_LIMIT),
        name="mixer_out",
    )(*x_ops, oa, ob, p["ln_mix"], p["w_g"], p["w_oa"], p["w_ob"], p["w_out"], p["ln_ffn"],
      p["w_r_hi"], p["w_r_lo"], p["b_r"])


_ZERO_ROWS = MOE_ROWS // 2
_ZERO_SIZES = tuple(_ZERO_ROWS >> s for s in range(5))


def _dispatch_kernel(zstart_ref, zlen_ref, nu_ref, dest_ref, h2_ref, xb_ref, zbuf_ref, sem, zsem, *, td):
    i = pl.program_id(0)

    def row_copy(t, d):
        return pltpu.make_async_copy(h2_ref.at[pl.ds(t, 1)], xb_ref.at[pl.ds(d, 1)], sem)

    def issue(t, c):
        row_copy(t, dest_ref[0, 0, 2 * t]).start()
        row_copy(t, dest_ref[0, 0, 2 * t + 1]).start()
        return c

    lax.fori_loop(0, td, issue, 0)

    def zero_copy(start, size):
        return pltpu.make_async_copy(zbuf_ref.at[pl.ds(0, size)], xb_ref.at[pl.ds(start, size)], zsem)

    def zero_pass(do):
        def per_expert(e, c):
            start = zstart_ref[e]
            head = (-start) & (SUBLANES - 1)
            for j in range(SUBLANES - 1):
                @pl.when(j < head)
                def _(row=start + j):
                    do(zero_copy(row, 1))

            start = start + head
            left = zlen_ref[e] - head
            for size in _ZERO_SIZES:
                take = (left & size) != 0

                @pl.when(take)
                def _(start=start, size=size):
                    do(zero_copy(pl.multiple_of(start, SUBLANES), size))

                start = start + jnp.where(take, size, 0)
            return c

        lax.fori_loop(0, N_EXPERTS, per_expert, 0)

        def unused_block(b, c):
            for part in range(MOE_ROWS // _ZERO_ROWS):
                row = pl.multiple_of(b * MOE_ROWS + part * _ZERO_ROWS, _ZERO_ROWS)
                do(zero_copy(row, _ZERO_ROWS))
            return c

        lax.fori_loop(nu_ref[0], xb_ref.shape[0] // MOE_ROWS, unused_block, 0)

    @pl.when(i == 0)
    def _():
        zbuf_ref[...] = jnp.zeros_like(zbuf_ref)
        zero_pass(lambda cp: cp.start())
        zero_pass(lambda cp: cp.wait())

    for _ in range(TOP_K):
        pltpu.make_async_copy(h2_ref, xb_ref.at[pl.ds(0, td)], sem).wait()


def _moe_dispatch(h2, dest, zstart, zlen, n_used, n_rows, td):
    t = h2.shape[0]
    nt = t // td
    dest3 = dest.reshape(nt, 1, 2 * td)
    return pl.pallas_call(
        functools.partial(_dispatch_kernel, td=td),
        grid_spec=pltpu.PrefetchScalarGridSpec(
            num_scalar_prefetch=3,
            grid=(nt,),
            in_specs=[
                pl.BlockSpec((1, 1, 2 * td), lambda i, zs, zl, nu: (i, 0, 0),
                             memory_space=pltpu.SMEM),
                pl.BlockSpec((td, D_PACK), lambda i, zs, zl, nu: (i, 0)),
            ],
            out_specs=pl.BlockSpec(memory_space=pl.ANY),
            scratch_shapes=[pltpu.VMEM((_ZERO_ROWS, D_PACK), U32),
                            pltpu.SemaphoreType.DMA, pltpu.SemaphoreType.DMA],
        ),
        out_shape=jax.ShapeDtypeStruct((n_rows, D_PACK), U32),
        compiler_params=pltpu.CompilerParams(dimension_semantics=("arbitrary",)),
        name="moe_dispatch",
    )(zstart, zlen, n_used, dest3, h2)


def _experts_kernel(be_ref, nu_ref, x_ref, wg_ref, wu_ref, wd_ref, y_ref, wgu_s, wd_s):
    b = pl.program_id(0)

    @pl.when((b == 0) | (be_ref[b] != be_ref[jnp.maximum(b - 1, 0)]))
    def _():
        wgu_s[:, :D_EXPERT] = wg_ref[0, 0].astype(BF16)
        wgu_s[:, D_EXPERT:] = wu_ref[0, 0].astype(BF16)
        wd_s[...] = wd_ref[0, 0].astype(BF16)

    @pl.when(b < nu_ref[0])
    def _():
        lo, hi = _unpack_rows(x_ref[...])
        x = jnp.concatenate([lo.astype(BF16), hi.astype(BF16)], axis=1)
        au = _dot(x, wgu_s[...])
        a, u = au[:, :D_EXPERT], au[:, D_EXPERT:]
        hid = (a * jax.nn.sigmoid(a) * u).astype(BF16)
        y_ref[...] = _pack_rows(_dot(hid, wd_s[...]))

    @pl.when(b >= nu_ref[0])
    def _():
        y_ref[...] = jnp.zeros_like(y_ref)


def _moe_experts(xb, blk_expert, n_used, w_gate, w_up, w_down, layer):
    n_rows = xb.shape[0]
    n_blocks = n_rows // MOE_ROWS
    live = lambda b, be, nu: (jnp.minimum(b, nu[0] - 1), 0)
    expert = lambda b, be, nu: (layer, be[b], 0, 0)
    return pl.pallas_call(
        _experts_kernel,
        grid_spec=pltpu.PrefetchScalarGridSpec(
            num_scalar_prefetch=2,
            grid=(n_blocks,),
            in_specs=[
                pl.BlockSpec((MOE_ROWS, D_PACK), live),
                pl.BlockSpec((1, 1, D_MODEL, D_EXPERT), expert),
                pl.BlockSpec((1, 1, D_MODEL, D_EXPERT), expert),
                pl.BlockSpec((1, 1, D_EXPERT, D_MODEL), expert),
            ],
            out_specs=pl.BlockSpec((MOE_ROWS, D_PACK), lambda b, be, nu: (b, 0)),
            scratch_shapes=[pltpu.VMEM((D_MODEL, 2 * D_EXPERT), BF16),
                            pltpu.VMEM((D_EXPERT, D_MODEL), BF16)],
        ),
        out_shape=jax.ShapeDtypeStruct((n_rows, D_PACK), U32),
        compiler_params=pltpu.CompilerParams(
            dimension_semantics=("arbitrary",), vmem_limit_bytes=VMEM_LIMIT),
        name="moe_experts",
    )(blk_expert, n_used, xb, w_gate, w_up, w_down)


def _combine_kernel(dest_ref, dnext_ref, x_ref, rf_ref, lnf_ref, yb_ref, *rest, tc, split):
    outs, (ybuf_ref, sems) = rest[:-2], rest[-2:]
    i = pl.program_id(0)
    n = pl.num_programs(0)
    slot = i % 2

    def row_copy(src_row, s, k, t):
        return pltpu.make_async_copy(yb_ref.at[pl.ds(src_row, 1)],
                                     ybuf_ref.at[s, k, pl.ds(t, 1)], sems.at[s])

    def issue(d_ref, s):
        def body(t, c):
            row_copy(d_ref[0, 0, 2 * t], s, 0, t).start()
            row_copy(d_ref[0, 0, 2 * t + 1], s, 1, t).start()
            return c

        lax.fori_loop(0, tc, body, 0)

    @pl.when(i == 0)
    def _():
        issue(dest_ref, slot)

    @pl.when(i + 1 < n)
    def _():
        issue(dnext_ref, 1 - slot)

    for k in range(TOP_K):
        pltpu.make_async_copy(yb_ref.at[pl.ds(0, tc)], ybuf_ref.at[slot, k], sems.at[slot]).wait()

    rf = rf_ref[...]
    lo0, hi0 = _unpack_rows(ybuf_ref[slot, 0])
    lo1, hi1 = _unpack_rows(ybuf_ref[slot, 1])
    g0, g1 = rf[:, 0:1], rf[:, 1:2]
    out = x_ref[...] + jnp.concatenate([g0 * lo0 + g1 * lo1, g0 * hi0 + g1 * hi1], axis=1)
    if split is None:
        outs[0][...] = out
    else:
        out = _rms(out, lnf_ref[...])

        @pl.when(i < split)
        def _():
            outs[0][...] = out

        @pl.when(i >= split)
        def _():
            outs[1][...] = out


def _moe_combine(x, rf, dest, yb, ln_final, tc, final_rows):
    t = x.shape[0]
    nt = t // tc
    dest3 = dest.reshape(nt, 1, 2 * tc)
    row = lambda i: (i, 0)
    if final_rows is None:
        split = None
        out_specs = pl.BlockSpec((tc, D_MODEL), row)
        out_shape = jax.ShapeDtypeStruct((t, D_MODEL), F32)
    else:
        split = final_rows[0] // tc
        out_specs = [pl.BlockSpec((tc, D_MODEL), lambda i: (jnp.minimum(i, split - 1), 0)),
                     pl.BlockSpec((tc, D_MODEL), lambda i: (jnp.maximum(i - split, 0), 0))]
        out_shape = [jax.ShapeDtypeStruct((r, D_MODEL), F32) for r in final_rows]
    return pl.pallas_call(
        functools.partial(_combine_kernel, tc=tc, split=split),
        grid=(nt,),
        in_specs=[
            pl.BlockSpec((1, 1, 2 * tc), lambda i: (i, 0, 0), memory_space=pltpu.SMEM),
            pl.BlockSpec((1, 1, 2 * tc), lambda i: (jnp.minimum(i + 1, nt - 1), 0, 0),
                         memory_space=pltpu.SMEM),
            pl.BlockSpec((tc, D_MODEL), row),
            pl.BlockSpec((tc, LANES), row),
            pl.BlockSpec((1, D_MODEL), lambda i: (0, 0)),
            pl.BlockSpec(memory_space=pl.ANY),
        ],
        out_specs=out_specs,
        out_shape=out_shape,
        scratch_shapes=[pltpu.VMEM((2, TOP_K, tc, D_PACK), U32),
                        pltpu.SemaphoreType.DMA((2,))],
        compiler_params=pltpu.CompilerParams(
            dimension_semantics=("arbitrary",), vmem_limit_bytes=VMEM_LIMIT),
        name="moe_combine",
    )(dest3, dest3, x, rf, ln_final, yb)


def _rope_tables(seq):
    inv = ROPE_BASE ** (-jnp.arange(0, D_ROPE, 2, dtype=F32) / D_ROPE)
    ang = jnp.arange(seq, dtype=F32)[:, None] * inv[None, :]
    cos, sin = jnp.cos(ang), jnp.sin(ang)
    z = lambda w: jnp.zeros((seq, w), F32)
    tail = LANES - D_NOPE - D_ROPE
    return {
        "cos": jnp.concatenate([jnp.ones((seq, D_NOPE), F32), cos, cos, z(tail)], axis=1),
        "sin_a": jnp.concatenate([z(D_NOPE), -sin, z(HALF_ROPE), z(tail)], axis=1),
        "sin_b": jnp.concatenate([z(D_NOPE), z(HALF_ROPE), sin, z(tail)], axis=1),
    }


def _t5_bucket(rel):
    half = NUM_BUCKETS // 2
    max_exact = half // 2
    n = jnp.abs(rel)
    nf = jnp.maximum(n, 1).astype(F32)
    large = max_exact + (jnp.log(nf / max_exact) / math.log(MAX_DIST / max_exact)
                         * (half - max_exact)).astype(jnp.int32)
    large = jnp.minimum(large, half - 1)
    return jnp.where(rel > 0, half, 0) + jnp.where(n < max_exact, n, large)


def _window_bias(rel_bias):
    qi = jnp.arange(WB)[:, None]
    kj = jnp.arange(3 * WB)[None, :]
    rel = kj - WB - qi
    bias = rel_bias[_t5_bucket(rel)].astype(F32) * LOG2E
    bias = jnp.where((jnp.abs(rel) <= WINDOW)[:, :, None], bias, NEG)
    bias = jnp.transpose(bias, (2, 0, 1)).reshape(KVB, GQ * WB, 3 * WB)
    return jnp.swapaxes(bias, 1, 2)


def _pack_layer(l, w_in, ln_mix, ln_cq, ln_ckv, w_uq, w_ukv, w_oa, sink, w_ob, w_out, ln_ffn,
                w_gr, b_gr, w_er, b_er):
    wi = w_in[l]
    o = 0
    c_q, o = wi[:, o:o + D_CQ], o + D_CQ
    c_kv, o = wi[:, o:o + D_C], o + D_C
    k_r, o = wi[:, o:o + D_ROPE], o + D_ROPE
    q_b, o = wi[:, o:o + HB * HD], o + HB * HD
    k_b, o = wi[:, o:o + KVB * HD], o + KVB * HD
    v_b, o = wi[:, o:o + KVB * HD], o + KVB * HD
    g_ab = wi[:, o:]
    zc = lambda rows, w: jnp.zeros((rows, w), F32)
    dup = lambda w: jnp.concatenate([w[:, kv * HD:(kv + 1) * HD] for kv in range(KVB) for _ in (0, 1)],
                                    axis=1)
    kr_blk = jnp.concatenate([zc(D_MODEL, D_NOPE), k_r, zc(D_MODEL, LANES - D_NOPE - D_ROPE)], axis=1)
    w_mix = jnp.concatenate([c_q, c_kv, kr_blk, q_b * (WIN_SCALE * LOG2E), dup(k_b)], axis=1)

    uq = w_uq[l].reshape(D_CQ, HA, D_NOPE + D_ROPE)
    uq = jnp.pad(uq, ((0, 0), (0, 0), (0, LANES - D_NOPE - D_ROPE))).reshape(D_CQ, HA * LANES)
    ukv = w_ukv[l].reshape(D_C, HA, D_NOPE + D_V)
    uk = jnp.pad(ukv[:, :, :D_NOPE], ((0, 0), (0, 0), (0, LANES - D_NOPE))).reshape(D_C, HA * LANES)
    uv = ukv[:, :, D_NOPE:].reshape(D_C, HA * D_V)

    w_r = jnp.concatenate([w_er[l], w_gr[l], zc(D_MODEL, LANES - N_EXPERTS - N_GROUPS)], axis=1)
    w_r_hi = w_r.astype(BF16)
    b_r = jnp.concatenate([b_er[l], b_gr[l], jnp.zeros((LANES - N_EXPERTS - N_GROUPS,), F32)])
    sink_row = jnp.repeat(sink[l].astype(F32).reshape(KVB, GQ) * LOG2E, WB, axis=1)[:, None, :]
    return {
        "ln_mix": ln_mix[l][None, :], "w_mix": w_mix.astype(BF16), "w_vbt": v_b.T.astype(BF16),
        "ln_cq": ln_cq[l][None, :], "ln_ckv": ln_ckv[l][None, :],
        "w_uq": uq.astype(BF16), "w_uk": uk.astype(BF16), "w_uvt": uv.T.astype(BF16),
        "w_g": g_ab.astype(BF16), "w_oa": w_oa[l].astype(BF16), "w_ob": w_ob[l].astype(BF16),
        "w_out": w_out[l].astype(BF16), "ln_ffn": ln_ffn[l][None, :],
        "w_r_hi": w_r_hi, "w_r_lo": (w_r - w_r_hi.astype(F32)).astype(BF16), "b_r": b_r[None, :],
        "sink": sink_row,
    }


def _row_tile(n, pref):
    tile = pref
    while n % tile:
        tile //= 2
    return tile


def _trunk(xs, batch, seq, tabs, bias, layers, expert_w, ln_final):
    final_rows = tuple(x.shape[0] for x in xs)
    t = sum(final_rows)
    tm = _row_tile(seq, 512)
    tq = _row_tile(seq, 512)
    tc = _row_tile(t, 256)
    n_blocks = (t * TOP_K) // MOE_ROWS + N_EXPERTS
    n_rows = n_blocks * MOE_ROWS
    for l, p in enumerate(layers):
        qa, ka, vat, qb, kb, vbt = _mixer_in(xs, p, tabs, seq, tm)
        oa = _mla_attention(qa, ka, vat, batch, seq, tq)
        ob = _window_attention(qb, kb, vbt, bias, p["sink"], batch, seq)
        x, h2, rf, ri, cnt = _mixer_out(xs, oa, ob, p, tm)
        counts = cnt[0, :N_EXPERTS].astype(jnp.int32)
        padded = ((counts + MOE_ROWS - 1) // MOE_ROWS) * MOE_ROWS
        pad_end = jnp.cumsum(padded)
        pad_start = pad_end - padded
        onehot = ri[:, 0:TOP_K, None] == jnp.arange(N_EXPERTS, dtype=jnp.int32)
        dest = (jnp.sum(jnp.where(onehot, pad_start, 0), axis=-1) + ri[:, TOP_K:2 * TOP_K]).reshape(-1)
        blk_row = jnp.arange(n_blocks, dtype=jnp.int32)[:, None] * MOE_ROWS
        blk_expert = jnp.minimum(jnp.sum((pad_end[None, :] <= blk_row).astype(jnp.int32), axis=1),
                                 N_EXPERTS - 1)
        n_used = (pad_end[-1:] // MOE_ROWS).astype(jnp.int32)
        xb = _moe_dispatch(h2, dest, (pad_start + counts).astype(jnp.int32),
                           (padded - counts).astype(jnp.int32), n_used, n_rows, tc)
        yb = _moe_experts(xb, blk_expert, n_used, *expert_w, l)
        last = l == len(layers) - 1
        x = _moe_combine(x, rf, dest, yb, ln_final, tc, final_rows if last else None)
        xs = (x, None)
    return x


def kernel(x_prompt, x_sample, rel_bias, ln_mix, w_in, ln_cq, ln_ckv, w_uq, w_ukv, w_oa, sink, w_ob,
           w_out, ln_ffn, w_gr, b_gr, w_er, b_er, w_gate, w_up, w_down, ln_final):
    bp, seq, d = x_prompt.shape
    bs = x_sample.shape[0]
    assert x_sample.shape[1] == seq and seq % WB == 0 and seq >= 3 * WB
    tabs = _rope_tables(seq)
    bias = _window_bias(rel_bias)
    layers = [_pack_layer(l, w_in, ln_mix, ln_cq, ln_ckv, w_uq, w_ukv, w_oa, sink, w_ob, w_out,
                          ln_ffn, w_gr, b_gr, w_er, b_er)
              for l in range(w_in.shape[0])]
    xs = (x_prompt.reshape(bp * seq, d), x_sample.reshape(bs * seq, d))
    y_prompt, y_sample = _trunk(xs, bp + bs, seq, tabs, bias, layers, (w_gate, w_up, w_down),
                                ln_final[None, :])
    return (y_prompt.reshape(bp, seq, d), y_sample.reshape(bs, seq, d))
```

```python
import functools
import math

import jax
import jax.numpy as jnp
from jax import lax
from jax.experimental import pallas as pl
from jax.experimental.pallas import tpu as pltpu

D_MODEL = 1024
DEPTH = 2
HA = 8
D_NOPE = 64
D_ROPE = 32
D_V = 64
D_CQ = 384
D_C = 256
ROPE_BASE = 10000.0
MLA_SCALE = (D_NOPE + D_ROPE) ** -0.5
HB = 8
KVB = 2
GQ = HB // KVB
HD = 64
WINDOW = 128
WB = 128
WIN_SCALE = HD ** -0.5
NUM_BUCKETS = 32
MAX_DIST = 128
N_GROUPS = 4
EPG = 8
N_EXPERTS = N_GROUPS * EPG
TOP_K = 2
D_EXPERT = 256
EPS = 1e-6
NEG = -1e30
LOG2E = math.log2(math.e)

LANES = 128
SUBLANES = 8
HALF_ROPE = D_ROPE // 2
MOE_ROWS = 256
DMA_QUEUES = 2
ROWS_PER_ISSUE = 4
VMEM_LIMIT = 48 * 1024 * 1024

C_CQ = 0
C_CKV = C_CQ + D_CQ
C_KR = C_CKV + D_C
C_QB = C_KR + LANES
C_KB = C_QB + HB * HD
N_MIX = C_KB + 2 * KVB * HD

F32 = jnp.float32
BF16 = jnp.bfloat16
U32 = jnp.uint32
D_PACK = D_MODEL // 2


def _rms(x, g):
    ms = jnp.mean(x * x, axis=-1, keepdims=True)
    return x * lax.rsqrt(ms + EPS) * g


def _dot(a, b):
    return jnp.dot(a, b, preferred_element_type=F32)


def _dot_nt(a, b):
    return lax.dot_general(a, b, (((1,), (1,)), ((), ())), preferred_element_type=F32)


def _pack_rows(x):
    n = x.shape[1] // 2
    lo = lax.bitcast_convert_type(x[:, :n].astype(BF16).astype(F32), U32)
    hi = lax.bitcast_convert_type(x[:, n:].astype(BF16).astype(F32), U32)
    return hi | (lo >> 16)


def _unpack_rows(p):
    lo = lax.bitcast_convert_type(p << 16, F32)
    hi = lax.bitcast_convert_type(p & jnp.uint32(0xFFFF0000), F32)
    return lo, hi


def _x_operands(xs, tile):
    xa, xb = xs
    na = xa.shape[0] // tile
    rows = xa.shape[0] + (0 if xb is None else xb.shape[0])
    specs = (pl.BlockSpec((tile, D_MODEL), lambda i, *_: (jnp.minimum(i, na - 1), 0)),
             pl.BlockSpec((tile, D_MODEL), lambda i, *_: (jnp.maximum(i - na, 0), 0)))
    return (xa, xa if xb is None else xb), specs, na, rows


def _x_tile(xa_ref, xb_ref, na):
    return jnp.where(pl.program_id(0) < na, xa_ref[...], xb_ref[...])


def _rope(t, cos, sin_a, sin_b):
    return (t * cos + pltpu.roll(t, LANES - HALF_ROPE, 1) * sin_a
            + pltpu.roll(t, HALF_ROPE, 1) * sin_b)


def _mixer_in_kernel(xa_ref, xb_ref, lnmix_ref, win_ref, wvbt_ref, lncq_ref, lnckv_ref, wuq_ref,
                     wuk_ref, wuvt_ref, cos_ref, sina_ref, sinb_ref,
                     qa_ref, ka_ref, vat_ref, qb_ref, kb_ref, vbt_ref, *, na):
    h = _rms(_x_tile(xa_ref, xb_ref, na), lnmix_ref[...]).astype(BF16)
    cq = _dot(h, win_ref[:, C_CQ:C_CKV])
    ckv = _dot(h, win_ref[:, C_CKV:C_KR])
    kr = _dot(h, win_ref[:, C_KR:C_QB])
    qb_ref[...] = _dot(h, win_ref[:, C_QB:C_KB]).astype(BF16)
    kb_ref[...] = _dot(h, win_ref[:, C_KB:N_MIX]).astype(BF16)
    vbt = _dot_nt(wvbt_ref[...], h).astype(BF16)
    for c in range(vbt_ref.shape[0]):
        vbt_ref[c] = vbt[:, c * WB:(c + 1) * WB]
    cqn = _rms(cq, lncq_ref[...]).astype(BF16)
    ckvn = _rms(ckv, lnckv_ref[...]).astype(BF16)
    q = _dot(cqn, wuq_ref[...])
    k = _dot(ckvn, wuk_ref[...])
    vat_ref[...] = _dot_nt(wuvt_ref[...], ckvn).astype(BF16)
    cos, sin_a, sin_b = cos_ref[...], sina_ref[...], sinb_ref[...]
    kr_rot = _rope(kr, cos, sin_a, sin_b)
    for hd in range(HA):
        sl = slice(hd * LANES, (hd + 1) * LANES)
        qa_ref[:, sl] = (_rope(q[:, sl], cos, sin_a, sin_b) * (MLA_SCALE * LOG2E)).astype(BF16)
        ka_ref[:, sl] = (k[:, sl] + kr_rot).astype(BF16)


def _mixer_in(xs, p, tabs, seq, tm):
    x_ops, x_specs, na, t = _x_operands(xs, tm)
    nt = t // tm
    spt = seq // tm
    row = lambda i: (i, 0)
    full = lambda i: (0, 0)
    tab = lambda i: (i % spt, 0)
    rows = lambda w: (pl.BlockSpec((tm, w), row), jax.ShapeDtypeStruct((t, w), BF16))
    outs = [
        rows(HA * LANES),
        rows(HA * LANES),
        (pl.BlockSpec((HA * D_V, tm), lambda i: (0, i)),
         jax.ShapeDtypeStruct((HA * D_V, t), BF16)),
        rows(HB * HD),
        rows(2 * KVB * HD),
        (pl.BlockSpec((tm // WB, KVB * HD, WB), lambda i: (i, 0, 0)),
         jax.ShapeDtypeStruct((t // WB, KVB * HD, WB), BF16)),
    ]
    out_specs = [o[0] for o in outs]
    out_shape = [o[1] for o in outs]
    return pl.pallas_call(
        functools.partial(_mixer_in_kernel, na=na),
        grid=(nt,),
        in_specs=[
            *x_specs,
            pl.BlockSpec((1, D_MODEL), full),
            pl.BlockSpec((D_MODEL, N_MIX), full),
            pl.BlockSpec((KVB * HD, D_MODEL), full),
            pl.BlockSpec((1, D_CQ), full),
            pl.BlockSpec((1, D_C), full),
            pl.BlockSpec((D_CQ, HA * LANES), full),
            pl.BlockSpec((D_C, HA * LANES), full),
            pl.BlockSpec((HA * D_V, D_C), full),
            pl.BlockSpec((tm, LANES), tab),
            pl.BlockSpec((tm, LANES), tab),
            pl.BlockSpec((tm, LANES), tab),
        ],
        out_specs=out_specs,
        out_shape=out_shape,
        compiler_params=pltpu.CompilerParams(
            dimension_semantics=("parallel",), vmem_limit_bytes=VMEM_LIMIT),
        name="mixer_in",
    )(*x_ops, p["ln_mix"], p["w_mix"], p["w_vbt"], p["ln_cq"], p["ln_ckv"], p["w_uq"], p["w_uk"], p["w_uvt"],
      tabs["cos"], tabs["sin_a"], tabs["sin_b"])


def _mla_kernel(q_ref, k_ref, vt_ref, o_ref):
    outs = []
    for hh in range(2):
        sl = slice(hh * LANES, (hh + 1) * LANES)
        st = _dot_nt(k_ref[:, sl], q_ref[:, sl])
        m = jnp.max(st, axis=0, keepdims=True)
        e = jnp.exp2(st - m)
        l = jnp.sum(e, axis=0, keepdims=True)
        ot = _dot(vt_ref[hh * D_V:(hh + 1) * D_V, :], e.astype(BF16))
        outs.append(ot / l)
    o_ref[...] = jnp.concatenate(outs, axis=0).T.astype(BF16)


def _mla_attention(qa, ka, va, batch, seq, tq):
    t = qa.shape[0]
    nq = seq // tq
    return pl.pallas_call(
        _mla_kernel,
        grid=(batch, HA // 2, nq),
        in_specs=[
            pl.BlockSpec((tq, 2 * LANES), lambda b, j, i: (b * nq + i, j)),
            pl.BlockSpec((seq, 2 * LANES), lambda b, j, i: (b, j)),
            pl.BlockSpec((2 * D_V, seq), lambda b, j, i: (j, b)),
        ],
        out_specs=pl.BlockSpec((tq, 2 * D_V), lambda b, j, i: (b * nq + i, j)),
        out_shape=jax.ShapeDtypeStruct((t, HA * D_V), BF16),
        compiler_params=pltpu.CompilerParams(
            dimension_semantics=("parallel", "parallel", "parallel"),
            vmem_limit_bytes=VMEM_LIMIT),
        name="mla_attention",
    )(qa, ka, va)


def _window_kernel(q_ref, k_ref, vt_ref, bias_ref, sink_ref, o_ref, *, seq, qblocks):
    nb = seq // WB
    lane = lax.broadcasted_iota(jnp.int32, (WB, LANES), 1)
    lo = lane < HD
    for u in range(qblocks):
        n = pl.program_id(1) * qblocks + u
        rows = slice(u * WB, (u + 1) * WB)
        blocks = (jnp.maximum(n - 1, 0), n, jnp.minimum(n + 1, nb - 1))
        kw = jnp.concatenate([k_ref[pl.ds(pl.multiple_of(b * WB, WB), WB), :] for b in blocks],
                             axis=0)
        vts = [vt_ref[b] for b in blocks]
        kpos = (n - 1) * WB + lax.broadcasted_iota(jnp.int32, (3 * WB, GQ * WB), 0)
        valid = (kpos >= 0) & (kpos < seq)
        heads = []
        for kv in range(KVB):
            kd = kw[:, kv * LANES:(kv + 1) * LANES]
            qs = []
            for jj in range(GQ // 2):
                qp = q_ref[rows, (kv * 2 + jj) * LANES:(kv * 2 + jj + 1) * LANES]
                qs.append(jnp.where(lo, qp, jnp.zeros_like(qp)))
                qs.append(jnp.where(lo, jnp.zeros_like(qp), qp))
            q4 = jnp.concatenate(qs, axis=0)
            st = _dot_nt(kd, q4) + bias_ref[kv]
            st = jnp.where(valid, st, NEG)
            sink = sink_ref[kv]
            m = jnp.maximum(jnp.max(st, axis=0, keepdims=True), sink)
            e = jnp.exp2(st - m)
            denom = jnp.sum(e, axis=0, keepdims=True) + jnp.exp2(sink - m)
            vt = jnp.concatenate([v[kv * HD:(kv + 1) * HD, :] for v in vts], axis=1)
            ot = _dot(vt, e.astype(BF16)) / denom
            heads += [ot[:, g * WB:(g + 1) * WB] for g in range(GQ)]
        o_ref[rows, :] = jnp.concatenate(heads, axis=0).T.astype(BF16)


def _window_attention(qb, kb, vbt, bias, sink, batch, seq):
    t = qb.shape[0]
    nb = seq // WB
    qblocks = 4 if nb % 4 == 0 else 1
    nb //= qblocks
    return pl.pallas_call(
        functools.partial(_window_kernel, seq=seq, qblocks=qblocks),
        grid=(batch, nb),
        in_specs=[
            pl.BlockSpec((qblocks * WB, HB * HD), lambda b, n: (b * nb + n, 0)),
            pl.BlockSpec((seq, 2 * KVB * HD), lambda b, n: (b, 0)),
            pl.BlockSpec((seq // WB, KVB * HD, WB), lambda b, n: (b, 0, 0)),
            pl.BlockSpec((KVB, 3 * WB, GQ * WB), lambda b, n: (0, 0, 0)),
            pl.BlockSpec((KVB, 1, GQ * WB), lambda b, n: (0, 0, 0)),
        ],
        out_specs=pl.BlockSpec((qblocks * WB, HB * HD), lambda b, n: (b * nb + n, 0)),
        out_shape=jax.ShapeDtypeStruct((t, HB * HD), BF16),
        compiler_params=pltpu.CompilerParams(
            dimension_semantics=("parallel", "parallel"), vmem_limit_bytes=VMEM_LIMIT),
        name="window_attention",
    )(qb, kb, vbt, bias, sink)


def _mixer_out_kernel(xa_ref, xb_ref, oa_ref, ob_ref, lnmix_ref, wg_ref, woa_ref, wob_ref, wout_ref,
                      lnffn_ref, wrh_ref, wrl_ref, br_ref,
                      xn_ref, h2_ref, rf_ref, ri_ref, cnt_ref, carry_ref, *, na):
    i = pl.program_id(0)

    @pl.when(i == 0)
    def _():
        carry_ref[...] = jnp.zeros_like(carry_ref)

    x = _x_tile(xa_ref, xb_ref, na)
    tm = x.shape[0]
    h = _rms(x, lnmix_ref[...]).astype(BF16)
    g = _dot(h, wg_ref[...])
    ya = _dot(oa_ref[...], woa_ref[...])
    yb = _dot(ob_ref[...], wob_ref[...])
    merged = jax.nn.sigmoid(g[:, :D_MODEL]) * ya + jax.nn.sigmoid(g[:, D_MODEL:]) * yb
    xn = x + _dot(merged.astype(BF16), wout_ref[...])
    xn_ref[...] = xn
    h2 = _rms(xn, lnffn_ref[...])
    h2_ref[...] = _pack_rows(h2)

    h2_hi = h2.astype(BF16)
    h2_lo = (h2 - h2_hi.astype(F32)).astype(BF16)
    logits = (_dot(h2_hi, wrh_ref[...]) + _dot(h2_hi, wrl_ref[...]) + _dot(h2_lo, wrh_ref[...])
              + br_ref[...])

    lane = lax.broadcasted_iota(jnp.int32, (tm, LANES), 1)
    lanef = lane.astype(F32)
    big = float(LANES)
    is_g = (lane >= N_EXPERTS) & (lane < N_EXPERTS + N_GROUPS)
    gl = jnp.where(is_g, logits, NEG)
    gmax = jnp.max(gl, axis=1, keepdims=True)
    grp = jnp.min(jnp.where(gl == gmax, lanef, big), axis=1, keepdims=True) - N_EXPERTS
    gsum = jnp.sum(jnp.where(is_g, jnp.exp(gl - gmax), 0.0), axis=1, keepdims=True)
    grp_w = 1.0 / gsum
    is_e = (lanef >= grp * EPG) & (lanef < grp * EPG + EPG)
    el = jnp.where(is_e, logits, NEG)
    v1 = jnp.max(el, axis=1, keepdims=True)
    i1 = jnp.min(jnp.where((el == v1) & is_e, lanef, big), axis=1, keepdims=True)
    a1 = lanef == i1
    el2 = jnp.where(a1, NEG, el)
    v2 = jnp.max(el2, axis=1, keepdims=True)
    i2 = jnp.min(jnp.where((el2 == v2) & is_e & (~a1), lanef, big), axis=1, keepdims=True)
    a2 = lanef == i2
    ex = jnp.exp(v2 - v1)
    p1 = 1.0 / (1.0 + ex)
    gate1 = grp_w * p1
    gate2 = grp_w * (ex * p1)

    a1f = a1.astype(F32)
    a2f = a2.astype(F32)
    cnt = a1f + a2f
    r_i = lax.broadcasted_iota(jnp.int32, (tm, tm), 0)
    c_i = lax.broadcasted_iota(jnp.int32, (tm, tm), 1)
    tri = jnp.where(r_i > c_i, 1.0, 0.0).astype(BF16)
    before = _dot(tri, cnt.astype(BF16)) + carry_ref[...]
    rank1 = jnp.sum(a1f * before, axis=1, keepdims=True)
    rank2 = jnp.sum(a2f * before, axis=1, keepdims=True)
    carry = carry_ref[...] + jnp.sum(cnt, axis=0, keepdims=True)
    carry_ref[...] = carry
    cnt_ref[...] = jnp.broadcast_to(carry, cnt_ref.shape)

    rf_ref[...] = jnp.where(lane == 0, gate1, jnp.where(lane == 1, gate2, 0.0))
    ri = jnp.where(lane == 0, i1, jnp.where(lane == 1, i2,
                                            jnp.where(lane == 2, rank1,
                                                      jnp.where(lane == 3, rank2, 0.0))))
    ri_ref[...] = ri.astype(jnp.int32)


def _mixer_out(xs, oa, ob, p, tm):
    x_ops, x_specs, na, t = _x_operands(xs, tm)
    nt = t // tm
    row = lambda i: (i, 0)
    full = lambda i: (0, 0)
    return pl.pallas_call(
        functools.partial(_mixer_out_kernel, na=na),
        grid=(nt,),
        in_specs=[
            *x_specs,
            pl.BlockSpec((tm, HA * D_V), row),
            pl.BlockSpec((tm, HB * HD), row),
            pl.BlockSpec((1, D_MODEL), full),
            pl.BlockSpec((D_MODEL, 2 * D_MODEL), full),
            pl.BlockSpec((HA * D_V, D_MODEL), full),
            pl.BlockSpec((HB * HD, D_MODEL), full),
            pl.BlockSpec((D_MODEL, D_MODEL), full),
            pl.BlockSpec((1, D_MODEL), full),
            pl.BlockSpec((D_MODEL, LANES), full),
            pl.BlockSpec((D_MODEL, LANES), full),
            pl.BlockSpec((1, LANES), full),
        ],
        out_specs=[
            pl.BlockSpec((tm, D_MODEL), row),
            pl.BlockSpec((tm, D_PACK), row),
            pl.BlockSpec((tm, LANES), row),
            pl.BlockSpec((tm, LANES), row),
            pl.BlockSpec((8, LANES), full),
        ],
        out_shape=[
            jax.ShapeDtypeStruct((t, D_MODEL), F32),
            jax.ShapeDtypeStruct((t, D_PACK), U32),
            jax.ShapeDtypeStruct((t, LANES), F32),
            jax.ShapeDtypeStruct((t, LANES), jnp.int32),
            jax.ShapeDtypeStruct((8, LANES), F32),
        ],
        scratch_shapes=[pltpu.VMEM((1, LANES), F32)],
        compiler_params=pltpu.CompilerParams(
            dimension_semantics=("arbitrary",), vmem_limit_bytes=VMEM_LIMIT),
        name="mixer_out",
    )(*x_ops, oa, ob, p["ln_mix"], p["w_g"], p["w_oa"], p["w_ob"], p["w_out"], p["ln_ffn"],
      p["w_r_hi"], p["w_r_lo"], p["b_r"])


_ZERO_ROWS = MOE_ROWS // 2
_ZERO_SIZES = tuple(_ZERO_ROWS >> s for s in range(5))


def _dispatch_kernel(zstart_ref, zlen_ref, nu_ref, dest_ref, h2_ref, xb_ref, zbuf_ref, sem, zsem, *, td):
    i = pl.program_id(0)

    def row_copy(t, d):
        return pltpu.make_async_copy(h2_ref.at[pl.ds(t, 1)], xb_ref.at[pl.ds(d, 1)], sem)

    def issue(g, c):
        for u in range(ROWS_PER_ISSUE):
            t = g * ROWS_PER_ISSUE + u
            for k in range(TOP_K):
                row_copy(t, dest_ref[0, 0, 2 * t + k]).start(
                    priority=(TOP_K * u + k) % DMA_QUEUES)
        return c

    lax.fori_loop(0, td // ROWS_PER_ISSUE, issue, 0)

    def zero_copy(start, size):
        return pltpu.make_async_copy(zbuf_ref.at[pl.ds(0, size)], xb_ref.at[pl.ds(start, size)], zsem)

    def zero_pass(do):
        def per_expert(e, c):
            start = zstart_ref[e]
            head = (-start) & (SUBLANES - 1)
            for j in range(SUBLANES - 1):
                @pl.when(j < head)
                def _(row=start + j):
                    do(zero_copy(row, 1))

            start = start + head
            left = zlen_ref[e] - head
            for size in _ZERO_SIZES:
                take = (left & size) != 0

                @pl.when(take)
                def _(start=start, size=size):
                    do(zero_copy(pl.multiple_of(start, SUBLANES), size))

                start = start + jnp.where(take, size, 0)
            return c

        lax.fori_loop(0, N_EXPERTS, per_expert, 0)

        def unused_block(b, c):
            for part in range(MOE_ROWS // _ZERO_ROWS):
                row = pl.multiple_of(b * MOE_ROWS + part * _ZERO_ROWS, _ZERO_ROWS)
                do(zero_copy(row, _ZERO_ROWS))
            return c

        lax.fori_loop(nu_ref[0], xb_ref.shape[0] // MOE_ROWS, unused_block, 0)

    @pl.when(i == 0)
    def _():
        zbuf_ref[...] = jnp.zeros_like(zbuf_ref)
        zero_pass(lambda cp: cp.start())
        zero_pass(lambda cp: cp.wait())

    for _ in range(TOP_K):
        pltpu.make_async_copy(h2_ref, xb_ref.at[pl.ds(0, td)], sem).wait()


def _moe_dispatch(h2, dest, zstart, zlen, n_used, n_rows, td):
    t = h2.shape[0]
    nt = t // td
    dest3 = dest.reshape(nt, 1, 2 * td)
    return pl.pallas_call(
        functools.partial(_dispatch_kernel, td=td),
        grid_spec=pltpu.PrefetchScalarGridSpec(
            num_scalar_prefetch=3,
            grid=(nt,),
            in_specs=[
                pl.BlockSpec((1, 1, 2 * td), lambda i, zs, zl, nu: (i, 0, 0),
                             memory_space=pltpu.SMEM),
                pl.BlockSpec((td, D_PACK), lambda i, zs, zl, nu: (i, 0)),
            ],
            out_specs=pl.BlockSpec(memory_space=pl.ANY),
            scratch_shapes=[pltpu.VMEM((_ZERO_ROWS, D_PACK), U32),
                            pltpu.SemaphoreType.DMA, pltpu.SemaphoreType.DMA],
        ),
        out_shape=jax.ShapeDtypeStruct((n_rows, D_PACK), U32),
        compiler_params=pltpu.CompilerParams(dimension_semantics=("arbitrary",)),
        name="moe_dispatch",
    )(zstart, zlen, n_used, dest3, h2)


def _experts_kernel(be_ref, nu_ref, x_ref, wg_ref, wu_ref, wd_ref, y_ref, wgu_s, wd_s):
    b = pl.program_id(0)

    @pl.when((b == 0) | (be_ref[b] != be_ref[jnp.maximum(b - 1, 0)]))
    def _():
        wgu_s[:, :D_EXPERT] = wg_ref[0, 0].astype(BF16)
        wgu_s[:, D_EXPERT:] = wu_ref[0, 0].astype(BF16)
        wd_s[...] = wd_ref[0, 0].astype(BF16)

    @pl.when(b < nu_ref[0])
    def _():
        lo, hi = _unpack_rows(x_ref[...])
        x = jnp.concatenate([lo.astype(BF16), hi.astype(BF16)], axis=1)
        au = _dot(x, wgu_s[...])
        a, u = au[:, :D_EXPERT], au[:, D_EXPERT:]
        hid = (a * jax.nn.sigmoid(a) * u).astype(BF16)
        y_ref[...] = _pack_rows(_dot(hid, wd_s[...]))

    @pl.when(b >= nu_ref[0])
    def _():
        y_ref[...] = jnp.zeros_like(y_ref)


def _moe_experts(xb, blk_expert, n_used, w_gate, w_up, w_down, layer):
    n_rows = xb.shape[0]
    n_blocks = n_rows // MOE_ROWS
    live = lambda b, be, nu: (jnp.minimum(b, nu[0] - 1), 0)
    expert = lambda b, be, nu: (layer, be[b], 0, 0)
    return pl.pallas_call(
        _experts_kernel,
        grid_spec=pltpu.PrefetchScalarGridSpec(
            num_scalar_prefetch=2,
            grid=(n_blocks,),
            in_specs=[
                pl.BlockSpec((MOE_ROWS, D_PACK), live),
                pl.BlockSpec((1, 1, D_MODEL, D_EXPERT), expert),
                pl.BlockSpec((1, 1, D_MODEL, D_EXPERT), expert),
                pl.BlockSpec((1, 1, D_EXPERT, D_MODEL), expert),
            ],
            out_specs=pl.BlockSpec((MOE_ROWS, D_PACK), lambda b, be, nu: (b, 0)),
            scratch_shapes=[pltpu.VMEM((D_MODEL, 2 * D_EXPERT), BF16),
                            pltpu.VMEM((D_EXPERT, D_MODEL), BF16)],
        ),
        out_shape=jax.ShapeDtypeStruct((n_rows, D_PACK), U32),
        compiler_params=pltpu.CompilerParams(
            dimension_semantics=("arbitrary",), vmem_limit_bytes=VMEM_LIMIT),
        name="moe_experts",
    )(blk_expert, n_used, xb, w_gate, w_up, w_down)


def _combine_kernel(dest_ref, dnext_ref, x_ref, rf_ref, lnf_ref, yb_ref, *rest, tc, split):
    outs, (ybuf_ref, sems) = rest[:-2], rest[-2:]
    i = pl.program_id(0)
    n = pl.num_programs(0)
    slot = i % 2

    def row_copy(src_row, s, k, t):
        return pltpu.make_async_copy(yb_ref.at[pl.ds(src_row, 1)],
                                     ybuf_ref.at[s, k, pl.ds(t, 1)], sems.at[s])

    def issue(d_ref, s):
        def body(g, c):
            for u in range(ROWS_PER_ISSUE):
                t = g * ROWS_PER_ISSUE + u
                for k in range(TOP_K):
                    row_copy(d_ref[0, 0, 2 * t + k], s, k, t).start(
                        priority=(TOP_K * u + k) % DMA_QUEUES)
            return c

        lax.fori_loop(0, tc // ROWS_PER_ISSUE, body, 0)

    @pl.when(i == 0)
    def _():
        issue(dest_ref, slot)

    @pl.when(i + 1 < n)
    def _():
        issue(dnext_ref, 1 - slot)

    for k in range(TOP_K):
        pltpu.make_async_copy(yb_ref.at[pl.ds(0, tc)], ybuf_ref.at[slot, k], sems.at[slot]).wait()

    rf = rf_ref[...]
    lo0, hi0 = _unpack_rows(ybuf_ref[slot, 0])
    lo1, hi1 = _unpack_rows(ybuf_ref[slot, 1])
    g0, g1 = rf[:, 0:1], rf[:, 1:2]
    out = x_ref[...] + jnp.concatenate([g0 * lo0 + g1 * lo1, g0 * hi0 + g1 * hi1], axis=1)
    if split is None:
        outs[0][...] = out
    else:
        out = _rms(out, lnf_ref[...])

        @pl.when(i < split)
        def _():
            outs[0][...] = out

        @pl.when(i >= split)
        def _():
            outs[1][...] = out


def _moe_combine(x, rf, dest, yb, ln_final, tc, final_rows):
    t = x.shape[0]
    nt = t // tc
    dest3 = dest.reshape(nt, 1, 2 * tc)
    row = lambda i: (i, 0)
    if final_rows is None:
        split = None
        out_specs = pl.BlockSpec((tc, D_MODEL), row)
        out_shape = jax.ShapeDtypeStruct((t, D_MODEL), F32)
    else:
        split = final_rows[0] // tc
        out_specs = [pl.BlockSpec((tc, D_MODEL), lambda i: (jnp.minimum(i, split - 1), 0)),
                     pl.BlockSpec((tc, D_MODEL), lambda i: (jnp.maximum(i - split, 0), 0))]
        out_shape = [jax.ShapeDtypeStruct((r, D_MODEL), F32) for r in final_rows]
    return pl.pallas_call(
        functools.partial(_combine_kernel, tc=tc, split=split),
        grid=(nt,),
        in_specs=[
            pl.BlockSpec((1, 1, 2 * tc), lambda i: (i, 0, 0), memory_space=pltpu.SMEM),
            pl.BlockSpec((1, 1, 2 * tc), lambda i: (jnp.minimum(i + 1, nt - 1), 0, 0),
                         memory_space=pltpu.SMEM),
            pl.BlockSpec((tc, D_MODEL), row),
            pl.BlockSpec((tc, LANES), row),
            pl.BlockSpec((1, D_MODEL), lambda i: (0, 0)),
            pl.BlockSpec(memory_space=pl.ANY),
        ],
        out_specs=out_specs,
        out_shape=out_shape,
        scratch_shapes=[pltpu.VMEM((2, TOP_K, tc, D_PACK), U32),
                        pltpu.SemaphoreType.DMA((2,))],
        compiler_params=pltpu.CompilerParams(
            dimension_semantics=("arbitrary",), vmem_limit_bytes=VMEM_LIMIT),
        name="moe_combine",
    )(dest3, dest3, x, rf, ln_final, yb)


def _rope_tables(seq):
    inv = ROPE_BASE ** (-jnp.arange(0, D_ROPE, 2, dtype=F32) / D_ROPE)
    ang = jnp.arange(seq, dtype=F32)[:, None] * inv[None, :]
    cos, sin = jnp.cos(ang), jnp.sin(ang)
    z = lambda w: jnp.zeros((seq, w), F32)
    tail = LANES - D_NOPE - D_ROPE
    return {
        "cos": jnp.concatenate([jnp.ones((seq, D_NOPE), F32), cos, cos, z(tail)], axis=1),
        "sin_a": jnp.concatenate([z(D_NOPE), -sin, z(HALF_ROPE), z(tail)], axis=1),
        "sin_b": jnp.concatenate([z(D_NOPE), z(HALF_ROPE), sin, z(tail)], axis=1),
    }


def _t5_bucket(rel):
    half = NUM_BUCKETS // 2
    max_exact = half // 2
    n = jnp.abs(rel)
    nf = jnp.maximum(n, 1).astype(F32)
    large = max_exact + (jnp.log(nf / max_exact) / math.log(MAX_DIST / max_exact)
                         * (half - max_exact)).astype(jnp.int32)
    large = jnp.minimum(large, half - 1)
    return jnp.where(rel > 0, half, 0) + jnp.where(n < max_exact, n, large)


def _window_bias(rel_bias):
    qi = jnp.arange(WB)[:, None]
    kj = jnp.arange(3 * WB)[None, :]
    rel = kj - WB - qi
    onehot = _t5_bucket(rel)[:, :, None] == jnp.arange(NUM_BUCKETS)
    bias = jnp.sum(jnp.where(onehot[..., None], rel_bias.astype(F32), 0.0), axis=2) * LOG2E
    bias = jnp.where((jnp.abs(rel) <= WINDOW)[:, :, None], bias, NEG)
    bias = jnp.transpose(bias, (2, 0, 1)).reshape(KVB, GQ * WB, 3 * WB)
    return jnp.swapaxes(bias, 1, 2)


def _pack_layer(l, w_in, ln_mix, ln_cq, ln_ckv, w_uq, w_ukv, w_oa, sink, w_ob, w_out, ln_ffn,
                w_gr, b_gr, w_er, b_er):
    wi = w_in[l]
    o = 0
    c_q, o = wi[:, o:o + D_CQ], o + D_CQ
    c_kv, o = wi[:, o:o + D_C], o + D_C
    k_r, o = wi[:, o:o + D_ROPE], o + D_ROPE
    q_b, o = wi[:, o:o + HB * HD], o + HB * HD
    k_b, o = wi[:, o:o + KVB * HD], o + KVB * HD
    v_b, o = wi[:, o:o + KVB * HD], o + KVB * HD
    g_ab = wi[:, o:]
    zc = lambda rows, w: jnp.zeros((rows, w), F32)
    dup = lambda w: jnp.concatenate([w[:, kv * HD:(kv + 1) * HD] for kv in range(KVB) for _ in (0, 1)],
                                    axis=1)
    kr_blk = jnp.concatenate([zc(D_MODEL, D_NOPE), k_r, zc(D_MODEL, LANES - D_NOPE - D_ROPE)], axis=1)
    w_mix = jnp.concatenate([c_q, c_kv, kr_blk, q_b * (WIN_SCALE * LOG2E), dup(k_b)], axis=1)

    uq = w_uq[l].reshape(D_CQ, HA, D_NOPE + D_ROPE)
    uq = jnp.pad(uq, ((0, 0), (0, 0), (0, LANES - D_NOPE - D_ROPE))).reshape(D_CQ, HA * LANES)
    ukv = w_ukv[l].reshape(D_C, HA, D_NOPE + D_V)
    uk = jnp.pad(ukv[:, :, :D_NOPE], ((0, 0), (0, 0), (0, LANES - D_NOPE))).reshape(D_C, HA * LANES)
    uv = ukv[:, :, D_NOPE:].reshape(D_C, HA * D_V)

    w_r = jnp.concatenate([w_er[l], w_gr[l], zc(D_MODEL, LANES - N_EXPERTS - N_GROUPS)], axis=1)
    w_r_hi = w_r.astype(BF16)
    b_r = jnp.concatenate([b_er[l], b_gr[l], jnp.zeros((LANES - N_EXPERTS - N_GROUPS,), F32)])
    sink_row = jnp.repeat(sink[l].astype(F32).reshape(KVB, GQ) * LOG2E, WB, axis=1)[:, None, :]
    return {
        "ln_mix": ln_mix[l][None, :], "w_mix": w_mix.astype(BF16), "w_vbt": v_b.T.astype(BF16),
        "ln_cq": ln_cq[l][None, :], "ln_ckv": ln_ckv[l][None, :],
        "w_uq": uq.astype(BF16), "w_uk": uk.astype(BF16), "w_uvt": uv.T.astype(BF16),
        "w_g": g_ab.astype(BF16), "w_oa": w_oa[l].astype(BF16), "w_ob": w_ob[l].astype(BF16),
        "w_out": w_out[l].astype(BF16), "ln_ffn": ln_ffn[l][None, :],
        "w_r_hi": w_r_hi, "w_r_lo": (w_r - w_r_hi.astype(F32)).astype(BF16), "b_r": b_r[None, :],
        "sink": sink_row,
    }


def _row_tile(n, pref):
    tile = pref
    while n % tile:
        tile //= 2
    return tile


def _trunk(xs, batch, seq, tabs, bias, layers, expert_w, ln_final):
    final_rows = tuple(x.shape[0] for x in xs)
    t = sum(final_rows)
    tm = _row_tile(seq, 512)
    tq = _row_tile(seq, 512)
    tc = _row_tile(t, 256)
    n_blocks = (t * TOP_K) // MOE_ROWS + N_EXPERTS
    n_rows = n_blocks * MOE_ROWS
    for l, p in enumerate(layers):
        qa, ka, vat, qb, kb, vbt = _mixer_in(xs, p, tabs, seq, tm)
        oa = _mla_attention(qa, ka, vat, batch, seq, tq)
        ob = _window_attention(qb, kb, vbt, bias, p["sink"], batch, seq)
        x, h2, rf, ri, cnt = _mixer_out(xs, oa, ob, p, tm)
        counts = cnt[0, :N_EXPERTS].astype(jnp.int32)
        padded = ((counts + MOE_ROWS - 1) // MOE_ROWS) * MOE_ROWS
        pad_end = jnp.cumsum(padded)
        pad_start = pad_end - padded
        onehot = ri[:, 0:TOP_K, None] == jnp.arange(N_EXPERTS, dtype=jnp.int32)
        dest = (jnp.sum(jnp.where(onehot, pad_start, 0), axis=-1) + ri[:, TOP_K:2 * TOP_K]).reshape(-1)
        blk_row = jnp.arange(n_blocks, dtype=jnp.int32)[:, None] * MOE_ROWS
        blk_expert = jnp.minimum(jnp.sum((pad_end[None, :] <= blk_row).astype(jnp.int32), axis=1),
                                 N_EXPERTS - 1)
        n_used = (pad_end[-1:] // MOE_ROWS).astype(jnp.int32)
        xb = _moe_dispatch(h2, dest, (pad_start + counts).astype(jnp.int32),
                           (padded - counts).astype(jnp.int32), n_used, n_rows, tc)
        yb = _moe_experts(xb, blk_expert, n_used, *expert_w, l)
        last = l == len(layers) - 1
        x = _moe_combine(x, rf, dest, yb, ln_final, tc, final_rows if last else None)
        xs = (x, None)
    return x


def kernel(x_prompt, x_sample, rel_bias, ln_mix, w_in, ln_cq, ln_ckv, w_uq, w_ukv, w_oa, sink, w_ob,
           w_out, ln_ffn, w_gr, b_gr, w_er, b_er, w_gate, w_up, w_down, ln_final):
    bp, seq, d = x_prompt.shape
    bs = x_sample.shape[0]
    assert x_sample.shape[1] == seq and seq % WB == 0 and seq >= 3 * WB
    tabs = _rope_tables(seq)
    bias = _window_bias(rel_bias)
    layers = [_pack_layer(l, w_in, ln_mix, ln_cq, ln_ckv, w_uq, w_ukv, w_oa, sink, w_ob, w_out,
                          ln_ffn, w_gr, b_gr, w_er, b_er)
              for l in range(w_in.shape[0])]
    xs = (x_prompt.reshape(bp * seq, d), x_sample.reshape(bs * seq, d))
    y_prompt, y_sample = _trunk(xs, bp + bs, seq, tabs, bias, layers, (w_gate, w_up, w_down),
                                ln_final[None, :])
    return (y_prompt.reshape(bp, seq, d), y_sample.reshape(bs, seq, d))
```

```python
import functools
import math

import jax
import jax.numpy as jnp
from jax import lax
from jax.experimental import pallas as pl
from jax.experimental.pallas import tpu as pltpu

D_MODEL = 1024
DEPTH = 2
HA = 8
D_NOPE = 64
D_ROPE = 32
D_V = 64
D_CQ = 384
D_C = 256
ROPE_BASE = 10000.0
MLA_SCALE = (D_NOPE + D_ROPE) ** -0.5
HB = 8
KVB = 2
GQ = HB // KVB
HD = 64
WINDOW = 128
WB = 128
WIN_SCALE = HD ** -0.5
NUM_BUCKETS = 32
MAX_DIST = 128
N_GROUPS = 4
EPG = 8
N_EXPERTS = N_GROUPS * EPG
TOP_K = 2
D_EXPERT = 256
EPS = 1e-6
NEG = -1e30
LOG2E = math.log2(math.e)

LANES = 128
SUBLANES = 8
HALF_ROPE = D_ROPE // 2
MOE_ROWS = 512
MLA_KEYS = 256
DMA_QUEUES = 2
ROWS_PER_ISSUE = 4
VMEM_LIMIT = 48 * 1024 * 1024

C_CQ = 0
C_CKV = C_CQ + D_CQ
C_KR = C_CKV + D_C
C_QB = C_KR + LANES
C_KB = C_QB + HB * HD
N_MIX = C_KB + 2 * KVB * HD

F32 = jnp.float32
BF16 = jnp.bfloat16
U32 = jnp.uint32
D_PACK = D_MODEL // 2


def _rms(x, g):
    ms = jnp.mean(x * x, axis=-1, keepdims=True)
    return x * lax.rsqrt(ms + EPS) * g


def _dot(a, b):
    return jnp.dot(a, b, preferred_element_type=F32)


def _dot_nt(a, b):
    return lax.dot_general(a, b, (((1,), (1,)), ((), ())), preferred_element_type=F32)


def _pack_rows(x):
    n = x.shape[1] // 2
    lo = lax.bitcast_convert_type(x[:, :n].astype(BF16).astype(F32), U32)
    hi = lax.bitcast_convert_type(x[:, n:].astype(BF16).astype(F32), U32)
    return hi | (lo >> 16)


def _unpack_rows(p):
    lo = lax.bitcast_convert_type(p << 16, F32)
    hi = lax.bitcast_convert_type(p & jnp.uint32(0xFFFF0000), F32)
    return lo, hi


def _x_operands(xs, tile):
    xa, xb = xs
    na = xa.shape[0] // tile
    rows = xa.shape[0] + (0 if xb is None else xb.shape[0])
    specs = (pl.BlockSpec((tile, D_MODEL), lambda i, *_: (jnp.minimum(i, na - 1), 0)),
             pl.BlockSpec((tile, D_MODEL), lambda i, *_: (jnp.maximum(i - na, 0), 0)))
    return (xa, xa if xb is None else xb), specs, na, rows


def _x_tile(xa_ref, xb_ref, na):
    return jnp.where(pl.program_id(0) < na, xa_ref[...], xb_ref[...])


def _rope(t, cos, sin_a, sin_b):
    return (t * cos + pltpu.roll(t, LANES - HALF_ROPE, 1) * sin_a
            + pltpu.roll(t, HALF_ROPE, 1) * sin_b)


def _mixer_in_kernel(xa_ref, xb_ref, lnmix_ref, win_ref, wvbt_ref, lncq_ref, lnckv_ref, wuq_ref,
                     wuk_ref, wuvt_ref, cos_ref, sina_ref, sinb_ref,
                     qa_ref, ka_ref, vat_ref, qb_ref, kb_ref, vbt_ref, *, na):
    h = _rms(_x_tile(xa_ref, xb_ref, na), lnmix_ref[...]).astype(BF16)
    cq = _dot(h, win_ref[:, C_CQ:C_CKV])
    ckv = _dot(h, win_ref[:, C_CKV:C_KR])
    kr = _dot(h, win_ref[:, C_KR:C_QB])
    qb_ref[...] = _dot(h, win_ref[:, C_QB:C_KB]).astype(BF16)
    kb_ref[...] = _dot(h, win_ref[:, C_KB:N_MIX]).astype(BF16)
    vbt = _dot_nt(wvbt_ref[...], h).astype(BF16)
    for c in range(vbt_ref.shape[0]):
        vbt_ref[c] = vbt[:, c * WB:(c + 1) * WB]
    cqn = _rms(cq, lncq_ref[...]).astype(BF16)
    ckvn = _rms(ckv, lnckv_ref[...]).astype(BF16)
    q = _dot(cqn, wuq_ref[...])
    k = _dot(ckvn, wuk_ref[...])
    vat_ref[...] = _dot_nt(wuvt_ref[...], ckvn).astype(BF16)
    cos, sin_a, sin_b = cos_ref[...], sina_ref[...], sinb_ref[...]
    kr_rot = _rope(kr, cos, sin_a, sin_b)
    for hd in range(HA):
        sl = slice(hd * LANES, (hd + 1) * LANES)
        qa_ref[:, sl] = (_rope(q[:, sl], cos, sin_a, sin_b) * (MLA_SCALE * LOG2E)).astype(BF16)
        ka_ref[:, sl] = (k[:, sl] + kr_rot).astype(BF16)


def _mixer_in(xs, p, tabs, seq, tm):
    x_ops, x_specs, na, t = _x_operands(xs, tm)
    nt = t // tm
    spt = seq // tm
    row = lambda i: (i, 0)
    full = lambda i: (0, 0)
    tab = lambda i: (i % spt, 0)
    rows = lambda w: (pl.BlockSpec((tm, w), row), jax.ShapeDtypeStruct((t, w), BF16))
    outs = [
        rows(HA * LANES),
        rows(HA * LANES),
        (pl.BlockSpec((HA * D_V, tm), lambda i: (0, i)),
         jax.ShapeDtypeStruct((HA * D_V, t), BF16)),
        rows(HB * HD),
        rows(2 * KVB * HD),
        (pl.BlockSpec((tm // WB, KVB * HD, WB), lambda i: (i, 0, 0)),
         jax.ShapeDtypeStruct((t // WB, KVB * HD, WB), BF16)),
    ]
    out_specs = [o[0] for o in outs]
    out_shape = [o[1] for o in outs]
    return pl.pallas_call(
        functools.partial(_mixer_in_kernel, na=na),
        grid=(nt,),
        in_specs=[
            *x_specs,
            pl.BlockSpec((1, D_MODEL), full),
            pl.BlockSpec((D_MODEL, N_MIX), full),
            pl.BlockSpec((KVB * HD, D_MODEL), full),
            pl.BlockSpec((1, D_CQ), full),
            pl.BlockSpec((1, D_C), full),
            pl.BlockSpec((D_CQ, HA * LANES), full),
            pl.BlockSpec((D_C, HA * LANES), full),
            pl.BlockSpec((HA * D_V, D_C), full),
            pl.BlockSpec((tm, LANES), tab),
            pl.BlockSpec((tm, LANES), tab),
            pl.BlockSpec((tm, LANES), tab),
        ],
        out_specs=out_specs,
        out_shape=out_shape,
        compiler_params=pltpu.CompilerParams(
            dimension_semantics=("parallel",), vmem_limit_bytes=VMEM_LIMIT),
        name="mixer_in",
    )(*x_ops, p["ln_mix"], p["w_mix"], p["w_vbt"], p["ln_cq"], p["ln_ckv"], p["w_uq"], p["w_uk"], p["w_uvt"],
      tabs["cos"], tabs["sin_a"], tabs["sin_b"])


def _mla_kernel(q_ref, k_ref, vt_ref, o_ref):
    heads = range(2)
    nch = k_ref.shape[0] // MLA_KEYS

    def scores(j, hh):
        sl = slice(hh * LANES, (hh + 1) * LANES)
        return _dot_nt(k_ref[j * MLA_KEYS:(j + 1) * MLA_KEYS, sl], q_ref[:, sl])

    def weighted_values(hh, j, p):
        return _dot(vt_ref[hh * D_V:(hh + 1) * D_V, j * MLA_KEYS:(j + 1) * MLA_KEYS], p)

    s_next = [scores(0, hh) for hh in heads]
    m = [None, None]
    l = [None, None]
    acc = [None, None]
    pending = [None, None]
    for j in range(nch):
        s_cur = s_next
        if j + 1 < nch:
            s_next = [scores(j + 1, hh) for hh in heads]
        for hh in heads:
            if pending[hh] is not None:
                p_prev, alpha_prev = pending[hh]
                pv = weighted_values(hh, j - 1, p_prev)
                acc[hh] = pv if acc[hh] is None else acc[hh] * alpha_prev + pv
            s = s_cur[hh]
            cmax = jnp.max(s, axis=0, keepdims=True)
            m_new = cmax if m[hh] is None else jnp.maximum(m[hh], cmax)
            alpha = None if m[hh] is None else jnp.exp2(m[hh] - m_new)
            p = jnp.exp2(s - m_new)
            psum = jnp.sum(p, axis=0, keepdims=True)
            l[hh] = psum if alpha is None else l[hh] * alpha + psum
            m[hh] = m_new
            pending[hh] = (p.astype(BF16), alpha)
    outs = []
    for hh in heads:
        p_prev, alpha_prev = pending[hh]
        pv = weighted_values(hh, nch - 1, p_prev)
        total = pv if acc[hh] is None else acc[hh] * alpha_prev + pv
        outs.append(total / l[hh])
    o_ref[...] = jnp.concatenate(outs, axis=0).T.astype(BF16)


def _mla_attention(qa, ka, va, batch, seq, tq):
    t = qa.shape[0]
    nq = seq // tq
    return pl.pallas_call(
        _mla_kernel,
        grid=(batch, HA // 2, nq),
        in_specs=[
            pl.BlockSpec((tq, 2 * LANES), lambda b, j, i: (b * nq + i, j)),
            pl.BlockSpec((seq, 2 * LANES), lambda b, j, i: (b, j)),
            pl.BlockSpec((2 * D_V, seq), lambda b, j, i: (j, b)),
        ],
        out_specs=pl.BlockSpec((tq, 2 * D_V), lambda b, j, i: (b * nq + i, j)),
        out_shape=jax.ShapeDtypeStruct((t, HA * D_V), BF16),
        compiler_params=pltpu.CompilerParams(
            dimension_semantics=("parallel", "parallel", "parallel"),
            vmem_limit_bytes=VMEM_LIMIT),
        name="mla_attention",
    )(qa, ka, va)


def _window_kernel(q_ref, k_ref, vt_ref, bias_ref, sink_ref, o_ref, *, seq, qblocks):
    nb = seq // WB
    lane = lax.broadcasted_iota(jnp.int32, (WB, LANES), 1)
    lo = lane < HD
    for u in range(qblocks):
        n = pl.program_id(1) * qblocks + u
        rows = slice(u * WB, (u + 1) * WB)
        blocks = (jnp.maximum(n - 1, 0), n, jnp.minimum(n + 1, nb - 1))
        kw = jnp.concatenate([k_ref[pl.ds(pl.multiple_of(b * WB, WB), WB), :] for b in blocks],
                             axis=0)
        vts = [vt_ref[b] for b in blocks]
        edge = jnp.where(n == 0, 0, jnp.where(n == nb - 1, 2, 1))
        heads = []
        for kv in range(KVB):
            kd = kw[:, kv * LANES:(kv + 1) * LANES]
            qs = []
            for jj in range(GQ // 2):
                qp = q_ref[rows, (kv * 2 + jj) * LANES:(kv * 2 + jj + 1) * LANES]
                qs.append(jnp.where(lo, qp, jnp.zeros_like(qp)))
                qs.append(jnp.where(lo, jnp.zeros_like(qp), qp))
            q4 = jnp.concatenate(qs, axis=0)
            st = _dot_nt(kd, q4) + bias_ref[edge, kv]
            sink = sink_ref[kv]
            m = jnp.maximum(jnp.max(st, axis=0, keepdims=True), sink)
            e = jnp.exp2(st - m)
            denom = jnp.sum(e, axis=0, keepdims=True) + jnp.exp2(sink - m)
            vt = jnp.concatenate([v[kv * HD:(kv + 1) * HD, :] for v in vts], axis=1)
            ot = _dot(vt, e.astype(BF16)) / denom
            heads += [ot[:, g * WB:(g + 1) * WB] for g in range(GQ)]
        o_ref[rows, :] = jnp.concatenate(heads, axis=0).T.astype(BF16)


def _window_attention(qb, kb, vbt, bias, sink, batch, seq):
    t = qb.shape[0]
    nb = seq // WB
    qblocks = 4 if nb % 4 == 0 else 1
    nb //= qblocks
    return pl.pallas_call(
        functools.partial(_window_kernel, seq=seq, qblocks=qblocks),
        grid=(batch, nb),
        in_specs=[
            pl.BlockSpec((qblocks * WB, HB * HD), lambda b, n: (b * nb + n, 0)),
            pl.BlockSpec((seq, 2 * KVB * HD), lambda b, n: (b, 0)),
            pl.BlockSpec((seq // WB, KVB * HD, WB), lambda b, n: (b, 0, 0)),
            pl.BlockSpec((3, KVB, 3 * WB, GQ * WB), lambda b, n: (0, 0, 0, 0)),
            pl.BlockSpec((KVB, 1, GQ * WB), lambda b, n: (0, 0, 0)),
        ],
        out_specs=pl.BlockSpec((qblocks * WB, HB * HD), lambda b, n: (b * nb + n, 0)),
        out_shape=jax.ShapeDtypeStruct((t, HB * HD), BF16),
        compiler_params=pltpu.CompilerParams(
            dimension_semantics=("parallel", "parallel"), vmem_limit_bytes=VMEM_LIMIT),
        name="window_attention",
    )(qb, kb, vbt, bias, sink)


def _mixer_out_kernel(xa_ref, xb_ref, oa_ref, ob_ref, lnmix_ref, wg_ref, woa_ref, wob_ref, wout_ref,
                      lnffn_ref, wrh_ref, br_ref,
                      xn_ref, h2_ref, rf_ref, ri_ref, cnt_ref, carry_ref, *, na):
    i = pl.program_id(0)

    @pl.when(i == 0)
    def _():
        carry_ref[...] = jnp.zeros_like(carry_ref)

    x = _x_tile(xa_ref, xb_ref, na)
    tm = x.shape[0]
    h = _rms(x, lnmix_ref[...]).astype(BF16)
    g = _dot(h, wg_ref[...])
    ya = _dot(oa_ref[...], woa_ref[...])
    yb = _dot(ob_ref[...], wob_ref[...])
    merged = jax.nn.sigmoid(g[:, :D_MODEL]) * ya + jax.nn.sigmoid(g[:, D_MODEL:]) * yb
    xn = x + _dot(merged.astype(BF16), wout_ref[...])
    xn_ref[...] = xn
    h2 = _rms(xn, lnffn_ref[...])
    h2_ref[...] = _pack_rows(h2)

    h2_hi = h2.astype(BF16)
    h2_lo = (h2 - h2_hi.astype(F32)).astype(BF16)
    hi_terms = _dot(h2_hi, wrh_ref[...])
    logits = (hi_terms[:, :LANES] + hi_terms[:, LANES:] + _dot(h2_lo, wrh_ref[:, :LANES])
              + br_ref[...])

    lane = lax.broadcasted_iota(jnp.int32, (tm, LANES), 1)
    lanef = lane.astype(F32)
    big = float(LANES)
    is_g = (lane >= N_EXPERTS) & (lane < N_EXPERTS + N_GROUPS)
    gl = jnp.where(is_g, logits, NEG)
    gmax = jnp.max(gl, axis=1, keepdims=True)
    grp = jnp.min(jnp.where(gl == gmax, lanef, big), axis=1, keepdims=True) - N_EXPERTS
    gsum = jnp.sum(jnp.where(is_g, jnp.exp(gl - gmax), 0.0), axis=1, keepdims=True)
    grp_w = 1.0 / gsum
    is_e = (lanef >= grp * EPG) & (lanef < grp * EPG + EPG)
    el = jnp.where(is_e, logits, NEG)
    v1 = jnp.max(el, axis=1, keepdims=True)
    i1 = jnp.min(jnp.where((el == v1) & is_e, lanef, big), axis=1, keepdims=True)
    a1 = lanef == i1
    el2 = jnp.where(a1, NEG, el)
    v2 = jnp.max(el2, axis=1, keepdims=True)
    i2 = jnp.min(jnp.where((el2 == v2) & is_e & (~a1), lanef, big), axis=1, keepdims=True)
    a2 = lanef == i2
    ex = jnp.exp(v2 - v1)
    p1 = 1.0 / (1.0 + ex)
    gate1 = grp_w * p1
    gate2 = grp_w * (ex * p1)

    a1f = a1.astype(F32)
    a2f = a2.astype(F32)
    cnt = a1f + a2f
    r_i = lax.broadcasted_iota(jnp.int32, (tm, tm), 0)
    c_i = lax.broadcasted_iota(jnp.int32, (tm, tm), 1)
    tri = jnp.where(r_i > c_i, 1.0, 0.0).astype(BF16)
    before = _dot(tri, cnt.astype(BF16)) + carry_ref[...]
    rank1 = jnp.sum(a1f * before, axis=1, keepdims=True)
    rank2 = jnp.sum(a2f * before, axis=1, keepdims=True)
    carry = carry_ref[...] + jnp.sum(cnt, axis=0, keepdims=True)
    carry_ref[...] = carry
    cnt_ref[...] = jnp.broadcast_to(carry, cnt_ref.shape)

    rf_ref[...] = jnp.where(lane == 0, gate1, jnp.where(lane == 1, gate2, 0.0))
    ri = jnp.where(lane == 0, i1, jnp.where(lane == 1, i2,
                                            jnp.where(lane == 2, rank1,
                                                      jnp.where(lane == 3, rank2, 0.0))))
    ri_ref[...] = ri.astype(jnp.int32)


def _mixer_out(xs, oa, ob, p, tm):
    x_ops, x_specs, na, t = _x_operands(xs, tm)
    nt = t // tm
    row = lambda i: (i, 0)
    full = lambda i: (0, 0)
    return pl.pallas_call(
        functools.partial(_mixer_out_kernel, na=na),
        grid=(nt,),
        in_specs=[
            *x_specs,
            pl.BlockSpec((tm, HA * D_V), row),
            pl.BlockSpec((tm, HB * HD), row),
            pl.BlockSpec((1, D_MODEL), full),
            pl.BlockSpec((D_MODEL, 2 * D_MODEL), full),
            pl.BlockSpec((HA * D_V, D_MODEL), full),
            pl.BlockSpec((HB * HD, D_MODEL), full),
            pl.BlockSpec((D_MODEL, D_MODEL), full),
            pl.BlockSpec((1, D_MODEL), full),
            pl.BlockSpec((D_MODEL, 2 * LANES), full),
            pl.BlockSpec((1, LANES), full),
        ],
        out_specs=[
            pl.BlockSpec((tm, D_MODEL), row),
            pl.BlockSpec((tm, D_PACK), row),
            pl.BlockSpec((tm, LANES), row),
            pl.BlockSpec((tm, LANES), row),
            pl.BlockSpec((8, LANES), full),
        ],
        out_shape=[
            jax.ShapeDtypeStruct((t, D_MODEL), F32),
            jax.ShapeDtypeStruct((t, D_PACK), U32),
            jax.ShapeDtypeStruct((t, LANES), F32),
            jax.ShapeDtypeStruct((t, LANES), jnp.int32),
            jax.ShapeDtypeStruct((8, LANES), F32),
        ],
        scratch_shapes=[pltpu.VMEM((1, LANES), F32)],
        compiler_params=pltpu.CompilerParams(
            dimension_semantics=("arbitrary",), vmem_limit_bytes=VMEM_LIMIT),
        name="mixer_out",
    )(*x_ops, oa, ob, p["ln_mix"], p["w_g"], p["w_oa"], p["w_ob"], p["w_out"], p["ln_ffn"],
      p["w_r"], p["b_r"])


_ZERO_ROWS = MOE_ROWS // 2
_ZERO_SIZES = tuple(_ZERO_ROWS >> s for s in range((_ZERO_ROWS // SUBLANES).bit_length()))


def _dispatch_kernel(zstart_ref, zlen_ref, nu_ref, dest_ref, h2_ref, xb_ref, zbuf_ref, sem, zsem, *, td):
    i = pl.program_id(0)

    def row_copy(t, d):
        return pltpu.make_async_copy(h2_ref.at[pl.ds(t, 1)], xb_ref.at[pl.ds(d, 1)], sem)

    def issue(g, c):
        for u in range(ROWS_PER_ISSUE):
            t = g * ROWS_PER_ISSUE + u
            for k in range(TOP_K):
                row_copy(t, dest_ref[0, 0, 2 * t + k]).start(
                    priority=(TOP_K * u + k) % DMA_QUEUES)
        return c

    lax.fori_loop(0, td // ROWS_PER_ISSUE, issue, 0)

    def zero_copy(start, size):
        return pltpu.make_async_copy(zbuf_ref.at[pl.ds(0, size)], xb_ref.at[pl.ds(start, size)], zsem)

    def zero_pass(do):
        def per_expert(e, c):
            start = zstart_ref[e]
            head = (-start) & (SUBLANES - 1)
            for j in range(SUBLANES - 1):
                @pl.when(j < head)
                def _(row=start + j):
                    do(zero_copy(row, 1))

            start = start + head
            left = zlen_ref[e] - head
            for size in _ZERO_SIZES:
                take = (left & size) != 0

                @pl.when(take)
                def _(start=start, size=size):
                    do(zero_copy(pl.multiple_of(start, SUBLANES), size))

                start = start + jnp.where(take, size, 0)
            return c

        lax.fori_loop(0, N_EXPERTS, per_expert, 0)

        def unused_block(b, c):
            for part in range(MOE_ROWS // _ZERO_ROWS):
                row = pl.multiple_of(b * MOE_ROWS + part * _ZERO_ROWS, _ZERO_ROWS)
                do(zero_copy(row, _ZERO_ROWS))
            return c

        lax.fori_loop(nu_ref[0], xb_ref.shape[0] // MOE_ROWS, unused_block, 0)

    @pl.when(i == 0)
    def _():
        zbuf_ref[...] = jnp.zeros_like(zbuf_ref)
        zero_pass(lambda cp: cp.start())
        zero_pass(lambda cp: cp.wait())

    for _ in range(TOP_K):
        pltpu.make_async_copy(h2_ref, xb_ref.at[pl.ds(0, td)], sem).wait()


def _moe_dispatch(h2, dest, zstart, zlen, n_used, n_rows, td):
    t = h2.shape[0]
    nt = t // td
    dest3 = dest.reshape(nt, 1, 2 * td)
    return pl.pallas_call(
        functools.partial(_dispatch_kernel, td=td),
        grid_spec=pltpu.PrefetchScalarGridSpec(
            num_scalar_prefetch=3,
            grid=(nt,),
            in_specs=[
                pl.BlockSpec((1, 1, 2 * td), lambda i, zs, zl, nu: (i, 0, 0),
                             memory_space=pltpu.SMEM),
                pl.BlockSpec((td, D_PACK), lambda i, zs, zl, nu: (i, 0)),
            ],
            out_specs=pl.BlockSpec(memory_space=pl.ANY),
            scratch_shapes=[pltpu.VMEM((_ZERO_ROWS, D_PACK), U32),
                            pltpu.SemaphoreType.DMA, pltpu.SemaphoreType.DMA],
        ),
        out_shape=jax.ShapeDtypeStruct((n_rows, D_PACK), U32),
        compiler_params=pltpu.CompilerParams(dimension_semantics=("arbitrary",)),
        name="moe_dispatch",
    )(zstart, zlen, n_used, dest3, h2)


def _experts_kernel(be_ref, nu_ref, x_ref, wg_ref, wu_ref, wd_ref, y_ref, wgu_s, wd_s):
    b = pl.program_id(0)

    @pl.when((b == 0) | (be_ref[b] != be_ref[jnp.maximum(b - 1, 0)]))
    def _():
        wgu_s[:, :D_EXPERT] = wg_ref[0, 0].astype(BF16)
        wgu_s[:, D_EXPERT:] = wu_ref[0, 0].astype(BF16)
        wd_s[...] = wd_ref[0, 0].astype(BF16)

    @pl.when(b < nu_ref[0])
    def _():
        lo, hi = _unpack_rows(x_ref[...])
        x = jnp.concatenate([lo.astype(BF16), hi.astype(BF16)], axis=1)
        au = _dot(x, wgu_s[...])
        a, u = au[:, :D_EXPERT], au[:, D_EXPERT:]
        hid = (a * jax.nn.sigmoid(a) * u).astype(BF16)
        y_ref[...] = _pack_rows(_dot(hid, wd_s[...]))

    @pl.when(b >= nu_ref[0])
    def _():
        y_ref[...] = jnp.zeros_like(y_ref)


def _moe_experts(xb, blk_expert, n_used, w_gate, w_up, w_down, layer):
    n_rows = xb.shape[0]
    n_blocks = n_rows // MOE_ROWS
    live = lambda b, be, nu: (jnp.minimum(b, nu[0] - 1), 0)
    expert = lambda b, be, nu: (layer, be[b], 0, 0)
    return pl.pallas_call(
        _experts_kernel,
        grid_spec=pltpu.PrefetchScalarGridSpec(
            num_scalar_prefetch=2,
            grid=(n_blocks,),
            in_specs=[
                pl.BlockSpec((MOE_ROWS, D_PACK), live),
                pl.BlockSpec((1, 1, D_MODEL, D_EXPERT), expert),
                pl.BlockSpec((1, 1, D_MODEL, D_EXPERT), expert),
                pl.BlockSpec((1, 1, D_EXPERT, D_MODEL), expert),
            ],
            out_specs=pl.BlockSpec((MOE_ROWS, D_PACK), lambda b, be, nu: (b, 0)),
            scratch_shapes=[pltpu.VMEM((D_MODEL, 2 * D_EXPERT), BF16),
                            pltpu.VMEM((D_EXPERT, D_MODEL), BF16)],
        ),
        out_shape=jax.ShapeDtypeStruct((n_rows, D_PACK), U32),
        compiler_params=pltpu.CompilerParams(
            dimension_semantics=("arbitrary",), vmem_limit_bytes=VMEM_LIMIT),
        name="moe_experts",
    )(blk_expert, n_used, xb, w_gate, w_up, w_down)


def _combine_kernel(dest_ref, dnext_ref, x_ref, rf_ref, lnf_ref, yb_ref, *rest, tc, split):
    outs, (ybuf_ref, sems) = rest[:-2], rest[-2:]
    i = pl.program_id(0)
    n = pl.num_programs(0)
    slot = i % 2

    def row_copy(src_row, s, k, t):
        return pltpu.make_async_copy(yb_ref.at[pl.ds(src_row, 1)],
                                     ybuf_ref.at[s, k, pl.ds(t, 1)], sems.at[s])

    def issue(d_ref, s):
        def body(g, c):
            for u in range(ROWS_PER_ISSUE):
                t = g * ROWS_PER_ISSUE + u
                for k in range(TOP_K):
                    row_copy(d_ref[0, 0, 2 * t + k], s, k, t).start(
                        priority=(TOP_K * u + k) % DMA_QUEUES)
            return c

        lax.fori_loop(0, tc // ROWS_PER_ISSUE, body, 0)

    @pl.when(i == 0)
    def _():
        issue(dest_ref, slot)

    @pl.when(i + 1 < n)
    def _():
        issue(dnext_ref, 1 - slot)

    for k in range(TOP_K):
        pltpu.make_async_copy(yb_ref.at[pl.ds(0, tc)], ybuf_ref.at[slot, k], sems.at[slot]).wait()

    rf = rf_ref[...]
    lo0, hi0 = _unpack_rows(ybuf_ref[slot, 0])
    lo1, hi1 = _unpack_rows(ybuf_ref[slot, 1])
    g0, g1 = rf[:, 0:1], rf[:, 1:2]
    out = x_ref[...] + jnp.concatenate([g0 * lo0 + g1 * lo1, g0 * hi0 + g1 * hi1], axis=1)
    if split is None:
        outs[0][...] = out
    else:
        out = _rms(out, lnf_ref[...])

        @pl.when(i < split)
        def _():
            outs[0][...] = out

        @pl.when(i >= split)
        def _():
            outs[1][...] = out


def _moe_combine(x, rf, dest, yb, ln_final, tc, final_rows):
    t = x.shape[0]
    nt = t // tc
    dest3 = dest.reshape(nt, 1, 2 * tc)
    row = lambda i: (i, 0)
    if final_rows is None:
        split = None
        out_specs = pl.BlockSpec((tc, D_MODEL), row)
        out_shape = jax.ShapeDtypeStruct((t, D_MODEL), F32)
    else:
        split = final_rows[0] // tc
        out_specs = [pl.BlockSpec((tc, D_MODEL), lambda i: (jnp.minimum(i, split - 1), 0)),
                     pl.BlockSpec((tc, D_MODEL), lambda i: (jnp.maximum(i - split, 0), 0))]
        out_shape = [jax.ShapeDtypeStruct((r, D_MODEL), F32) for r in final_rows]
    return pl.pallas_call(
        functools.partial(_combine_kernel, tc=tc, split=split),
        grid=(nt,),
        in_specs=[
            pl.BlockSpec((1, 1, 2 * tc), lambda i: (i, 0, 0), memory_space=pltpu.SMEM),
            pl.BlockSpec((1, 1, 2 * tc), lambda i: (jnp.minimum(i + 1, nt - 1), 0, 0),
                         memory_space=pltpu.SMEM),
            pl.BlockSpec((tc, D_MODEL), row),
            pl.BlockSpec((tc, LANES), row),
            pl.BlockSpec((1, D_MODEL), lambda i: (0, 0)),
            pl.BlockSpec(memory_space=pl.ANY),
        ],
        out_specs=out_specs,
        out_shape=out_shape,
        scratch_shapes=[pltpu.VMEM((2, TOP_K, tc, D_PACK), U32),
                        pltpu.SemaphoreType.DMA((2,))],
        compiler_params=pltpu.CompilerParams(
            dimension_semantics=("arbitrary",), vmem_limit_bytes=VMEM_LIMIT),
        name="moe_combine",
    )(dest3, dest3, x, rf, ln_final, yb)


def _rope_tables(seq):
    inv = ROPE_BASE ** (-jnp.arange(0, D_ROPE, 2, dtype=F32) / D_ROPE)
    ang = jnp.arange(seq, dtype=F32)[:, None] * inv[None, :]
    cos, sin = jnp.cos(ang), jnp.sin(ang)
    z = lambda w: jnp.zeros((seq, w), F32)
    tail = LANES - D_NOPE - D_ROPE
    return {
        "cos": jnp.concatenate([jnp.ones((seq, D_NOPE), F32), cos, cos, z(tail)], axis=1),
        "sin_a": jnp.concatenate([z(D_NOPE), -sin, z(HALF_ROPE), z(tail)], axis=1),
        "sin_b": jnp.concatenate([z(D_NOPE), z(HALF_ROPE), sin, z(tail)], axis=1),
    }


def _t5_bucket(rel):
    half = NUM_BUCKETS // 2
    max_exact = half // 2
    n = jnp.abs(rel)
    nf = jnp.maximum(n, 1).astype(F32)
    large = max_exact + (jnp.log(nf / max_exact) / math.log(MAX_DIST / max_exact)
                         * (half - max_exact)).astype(jnp.int32)
    large = jnp.minimum(large, half - 1)
    return jnp.where(rel > 0, half, 0) + jnp.where(n < max_exact, n, large)


def _window_bias(rel_bias):
    qi = jnp.arange(WB)[:, None]
    kj = jnp.arange(3 * WB)[None, :]
    rel = kj - WB - qi
    onehot = _t5_bucket(rel)[:, :, None] == jnp.arange(NUM_BUCKETS)
    bias = jnp.sum(jnp.where(onehot[..., None], rel_bias.astype(F32), 0.0), axis=2) * LOG2E
    bias = jnp.where((jnp.abs(rel) <= WINDOW)[:, :, None], bias, NEG)
    bias = jnp.transpose(bias, (2, 0, 1)).reshape(KVB, GQ * WB, 3 * WB)
    bias = jnp.swapaxes(bias, 1, 2)
    key = jnp.arange(3 * WB)[None, :, None]
    return jnp.stack([jnp.where(key < WB, NEG, bias), bias, jnp.where(key >= 2 * WB, NEG, bias)])


def _pack_layer(l, w_in, ln_mix, ln_cq, ln_ckv, w_uq, w_ukv, w_oa, sink, w_ob, w_out, ln_ffn,
                w_gr, b_gr, w_er, b_er):
    wi = w_in[l]
    o = 0
    c_q, o = wi[:, o:o + D_CQ], o + D_CQ
    c_kv, o = wi[:, o:o + D_C], o + D_C
    k_r, o = wi[:, o:o + D_ROPE], o + D_ROPE
    q_b, o = wi[:, o:o + HB * HD], o + HB * HD
    k_b, o = wi[:, o:o + KVB * HD], o + KVB * HD
    v_b, o = wi[:, o:o + KVB * HD], o + KVB * HD
    g_ab = wi[:, o:]
    zc = lambda rows, w: jnp.zeros((rows, w), F32)
    dup = lambda w: jnp.concatenate([w[:, kv * HD:(kv + 1) * HD] for kv in range(KVB) for _ in (0, 1)],
                                    axis=1)
    kr_blk = jnp.concatenate([zc(D_MODEL, D_NOPE), k_r, zc(D_MODEL, LANES - D_NOPE - D_ROPE)], axis=1)
    w_mix = jnp.concatenate([c_q, c_kv, kr_blk, q_b * (WIN_SCALE * LOG2E), dup(k_b)], axis=1)

    uq = w_uq[l].reshape(D_CQ, HA, D_NOPE + D_ROPE)
    uq = jnp.pad(uq, ((0, 0), (0, 0), (0, LANES - D_NOPE - D_ROPE))).reshape(D_CQ, HA * LANES)
    ukv = w_ukv[l].reshape(D_C, HA, D_NOPE + D_V)
    uk = jnp.pad(ukv[:, :, :D_NOPE], ((0, 0), (0, 0), (0, LANES - D_NOPE))).reshape(D_C, HA * LANES)
    uv = ukv[:, :, D_NOPE:].reshape(D_C, HA * D_V)

    w_r = jnp.concatenate([w_er[l], w_gr[l], zc(D_MODEL, LANES - N_EXPERTS - N_GROUPS)], axis=1)
    w_r_hi = w_r.astype(BF16)
    b_r = jnp.concatenate([b_er[l], b_gr[l], jnp.zeros((LANES - N_EXPERTS - N_GROUPS,), F32)])
    sink_row = jnp.repeat(sink[l].astype(F32).reshape(KVB, GQ) * LOG2E, WB, axis=1)[:, None, :]
    return {
        "ln_mix": ln_mix[l][None, :], "w_mix": w_mix.astype(BF16), "w_vbt": v_b.T.astype(BF16),
        "ln_cq": ln_cq[l][None, :], "ln_ckv": ln_ckv[l][None, :],
        "w_uq": uq.astype(BF16), "w_uk": uk.astype(BF16), "w_uvt": uv.T.astype(BF16),
        "w_g": g_ab.astype(BF16), "w_oa": w_oa[l].astype(BF16), "w_ob": w_ob[l].astype(BF16),
        "w_out": w_out[l].astype(BF16), "ln_ffn": ln_ffn[l][None, :],
        "w_r": jnp.concatenate([w_r_hi, (w_r - w_r_hi.astype(F32)).astype(BF16)], axis=1),
        "b_r": b_r[None, :],
        "sink": sink_row,
    }


def _row_tile(n, pref):
    tile = pref
    while n % tile:
        tile //= 2
    return tile


def _trunk(xs, batch, seq, tabs, bias, layers, expert_w, ln_final):
    final_rows = tuple(x.shape[0] for x in xs)
    t = sum(final_rows)
    tm = _row_tile(seq, 512)
    tq = _row_tile(seq, 512)
    tc = _row_tile(t, 256)
    n_blocks = (t * TOP_K) // MOE_ROWS + N_EXPERTS
    n_rows = n_blocks * MOE_ROWS
    for l, p in enumerate(layers):
        qa, ka, vat, qb, kb, vbt = _mixer_in(xs, p, tabs, seq, tm)
        oa = _mla_attention(qa, ka, vat, batch, seq, tq)
        ob = _window_attention(qb, kb, vbt, bias, p["sink"], batch, seq)
        x, h2, rf, ri, cnt = _mixer_out(xs, oa, ob, p, tm)
        counts = cnt[0, :N_EXPERTS].astype(jnp.int32)
        padded = ((counts + MOE_ROWS - 1) // MOE_ROWS) * MOE_ROWS
        pad_end = jnp.cumsum(padded)
        pad_start = pad_end - padded
        onehot = ri[:, 0:TOP_K, None] == jnp.arange(N_EXPERTS, dtype=jnp.int32)
        dest = (jnp.sum(jnp.where(onehot, pad_start, 0), axis=-1) + ri[:, TOP_K:2 * TOP_K]).reshape(-1)
        blk_row = jnp.arange(n_blocks, dtype=jnp.int32)[:, None] * MOE_ROWS
        blk_expert = jnp.minimum(jnp.sum((pad_end[None, :] <= blk_row).astype(jnp.int32), axis=1),
                                 N_EXPERTS - 1)
        n_used = (pad_end[-1:] // MOE_ROWS).astype(jnp.int32)
        xb = _moe_dispatch(h2, dest, (pad_start + counts).astype(jnp.int32),
                           (padded - counts).astype(jnp.int32), n_used, n_rows, tc)
        yb = _moe_experts(xb, blk_expert, n_used, *expert_w, l)
        last = l == len(layers) - 1
        x = _moe_combine(x, rf, dest, yb, ln_final, tc, final_rows if last else None)
        xs = (x, None)
    return x


def kernel(x_prompt, x_sample, rel_bias, ln_mix, w_in, ln_cq, ln_ckv, w_uq, w_ukv, w_oa, sink, w_ob,
           w_out, ln_ffn, w_gr, b_gr, w_er, b_er, w_gate, w_up, w_down, ln_final):
    bp, seq, d = x_prompt.shape
    bs = x_sample.shape[0]
    assert x_sample.shape[1] == seq and seq % WB == 0 and seq >= 3 * WB
    tabs = _rope_tables(seq)
    bias = _window_bias(rel_bias)
    layers = [_pack_layer(l, w_in, ln_mix, ln_cq, ln_ckv, w_uq, w_ukv, w_oa, sink, w_ob, w_out,
                          ln_ffn, w_gr, b_gr, w_er, b_er)
              for l in range(w_in.shape[0])]
    xs = (x_prompt.reshape(bp * seq, d), x_sample.reshape(bs * seq, d))
    y_prompt, y_sample = _trunk(xs, bp + bs, seq, tabs, bias, layers, (w_gate, w_up, w_down),
                                ln_final[None, :])
    return (y_prompt.reshape(bp, seq, d), y_sample.reshape(bs, seq, d))
```

```python
import functools
import math

import jax
import jax.numpy as jnp
from jax import lax
from jax.experimental import pallas as pl
from jax.experimental.pallas import tpu as pltpu

D_MODEL = 1024
DEPTH = 2
HA = 8
D_NOPE = 64
D_ROPE = 32
D_V = 64
D_CQ = 384
D_C = 256
ROPE_BASE = 10000.0
MLA_SCALE = (D_NOPE + D_ROPE) ** -0.5
HB = 8
KVB = 2
GQ = HB // KVB
HD = 64
WINDOW = 128
WB = 128
WIN_SCALE = HD ** -0.5
NUM_BUCKETS = 32
MAX_DIST = 128
N_GROUPS = 4
EPG = 8
N_EXPERTS = N_GROUPS * EPG
TOP_K = 2
D_EXPERT = 256
EPS = 1e-6
NEG = -1e30
LOG2E = math.log2(math.e)

LANES = 128
SUBLANES = 8
HALF_ROPE = D_ROPE // 2
MOE_ROWS = 512
MLA_KEYS = 256
DMA_QUEUES = 2
ROWS_PER_ISSUE = 4
VMEM_LIMIT = 48 * 1024 * 1024

C_CQ = 0
C_CKV = C_CQ + D_CQ
C_KR = C_CKV + D_C
C_QB = C_KR + LANES
C_KB = C_QB + HB * HD
N_MIX = C_KB + 2 * KVB * HD

F32 = jnp.float32
BF16 = jnp.bfloat16
U32 = jnp.uint32
D_PACK = D_MODEL // 2


def _rms(x, g):
    ms = jnp.mean(x * x, axis=-1, keepdims=True)
    return x * lax.rsqrt(ms + EPS) * g


def _dot(a, b):
    return jnp.dot(a, b, preferred_element_type=F32)


def _dot_nt(a, b):
    return lax.dot_general(a, b, (((1,), (1,)), ((), ())), preferred_element_type=F32)


def _pack_rows(x):
    n = x.shape[1] // 2
    lo = lax.bitcast_convert_type(x[:, :n].astype(BF16).astype(F32), U32)
    hi = lax.bitcast_convert_type(x[:, n:].astype(BF16).astype(F32), U32)
    return hi | (lo >> 16)


def _unpack_rows(p):
    lo = lax.bitcast_convert_type(p << 16, F32)
    hi = lax.bitcast_convert_type(p & jnp.uint32(0xFFFF0000), F32)
    return lo, hi


def _x_operands(xs, tile):
    xa, xb = xs
    na = xa.shape[0] // tile
    rows = xa.shape[0] + (0 if xb is None else xb.shape[0])
    specs = (pl.BlockSpec((tile, D_MODEL), lambda i, *_: (jnp.minimum(i, na - 1), 0)),
             pl.BlockSpec((tile, D_MODEL), lambda i, *_: (jnp.maximum(i - na, 0), 0)))
    return (xa, xa if xb is None else xb), specs, na, rows


def _x_tile(xa_ref, xb_ref, na):
    return jnp.where(pl.program_id(0) < na, xa_ref[...], xb_ref[...])


def _rope(t, cos, sin_a, sin_b):
    return (t * cos + pltpu.roll(t, LANES - HALF_ROPE, 1) * sin_a
            + pltpu.roll(t, HALF_ROPE, 1) * sin_b)


def _mixer_in_kernel(xa_ref, xb_ref, lnmix_ref, win_ref, wvbt_ref, lncq_ref, lnckv_ref, wuq_ref,
                     wuk_ref, wuvt_ref, cos_ref, sina_ref, sinb_ref,
                     qa_ref, ka_ref, vat_ref, qb_ref, kb_ref, vbt_ref, *, na):
    h = _rms(_x_tile(xa_ref, xb_ref, na), lnmix_ref[...]).astype(BF16)
    cq = _dot(h, win_ref[:, C_CQ:C_CKV])
    ckv = _dot(h, win_ref[:, C_CKV:C_KR])
    kr = _dot(h, win_ref[:, C_KR:C_QB])
    qb_ref[...] = _dot(h, win_ref[:, C_QB:C_KB]).astype(BF16)
    kb_ref[...] = _dot(h, win_ref[:, C_KB:N_MIX]).astype(BF16)
    vbt = _dot_nt(wvbt_ref[...], h).astype(BF16)
    for c in range(vbt_ref.shape[0]):
        vbt_ref[c] = vbt[:, c * WB:(c + 1) * WB]
    cqn = _rms(cq, lncq_ref[...]).astype(BF16)
    ckvn = _rms(ckv, lnckv_ref[...]).astype(BF16)
    q = _dot(cqn, wuq_ref[...])
    k = _dot(ckvn, wuk_ref[...])
    vat_ref[...] = _dot_nt(wuvt_ref[...], ckvn).astype(BF16)
    cos, sin_a, sin_b = cos_ref[...], sina_ref[...], sinb_ref[...]
    kr_rot = _rope(kr, cos, sin_a, sin_b)
    for hd in range(HA):
        sl = slice(hd * LANES, (hd + 1) * LANES)
        qa_ref[:, sl] = (_rope(q[:, sl], cos, sin_a, sin_b) * (MLA_SCALE * LOG2E)).astype(BF16)
        ka_ref[:, sl] = (k[:, sl] + kr_rot).astype(BF16)


def _mixer_in(xs, p, tabs, seq, tm):
    x_ops, x_specs, na, t = _x_operands(xs, tm)
    nt = t // tm
    spt = seq // tm
    row = lambda i: (i, 0)
    full = lambda i: (0, 0)
    tab = lambda i: (i % spt, 0)
    rows = lambda w: (pl.BlockSpec((tm, w), row), jax.ShapeDtypeStruct((t, w), BF16))
    outs = [
        rows(HA * LANES),
        rows(HA * LANES),
        (pl.BlockSpec((HA * D_V, tm), lambda i: (0, i)),
         jax.ShapeDtypeStruct((HA * D_V, t), BF16)),
        rows(HB * HD),
        rows(2 * KVB * HD),
        (pl.BlockSpec((tm // WB, KVB * HD, WB), lambda i: (i, 0, 0)),
         jax.ShapeDtypeStruct((t // WB, KVB * HD, WB), BF16)),
    ]
    out_specs = [o[0] for o in outs]
    out_shape = [o[1] for o in outs]
    return pl.pallas_call(
        functools.partial(_mixer_in_kernel, na=na),
        grid=(nt,),
        in_specs=[
            *x_specs,
            pl.BlockSpec((1, D_MODEL), full),
            pl.BlockSpec((D_MODEL, N_MIX), full),
            pl.BlockSpec((KVB * HD, D_MODEL), full),
            pl.BlockSpec((1, D_CQ), full),
            pl.BlockSpec((1, D_C), full),
            pl.BlockSpec((D_CQ, HA * LANES), full),
            pl.BlockSpec((D_C, HA * LANES), full),
            pl.BlockSpec((HA * D_V, D_C), full),
            pl.BlockSpec((tm, LANES), tab),
            pl.BlockSpec((tm, LANES), tab),
            pl.BlockSpec((tm, LANES), tab),
        ],
        out_specs=out_specs,
        out_shape=out_shape,
        compiler_params=pltpu.CompilerParams(
            dimension_semantics=("parallel",), vmem_limit_bytes=VMEM_LIMIT),
        name="mixer_in",
    )(*x_ops, p["ln_mix"], p["w_mix"], p["w_vbt"], p["ln_cq"], p["ln_ckv"], p["w_uq"], p["w_uk"], p["w_uvt"],
      tabs["cos"], tabs["sin_a"], tabs["sin_b"])


def _mla_kernel(q_ref, k_ref, vt_ref, o_ref):
    heads = range(2)
    nch = k_ref.shape[0] // MLA_KEYS

    def scores(j, hh):
        sl = slice(hh * LANES, (hh + 1) * LANES)
        return _dot_nt(k_ref[j * MLA_KEYS:(j + 1) * MLA_KEYS, sl], q_ref[:, sl])

    def weighted_values(hh, j, p):
        return _dot(vt_ref[hh * D_V:(hh + 1) * D_V, j * MLA_KEYS:(j + 1) * MLA_KEYS], p)

    s_next = [scores(0, hh) for hh in heads]
    m = [None, None]
    l = [None, None]
    acc = [None, None]
    pending = [None, None]
    for j in range(nch):
        s_cur = s_next
        if j + 1 < nch:
            s_next = [scores(j + 1, hh) for hh in heads]
        for hh in heads:
            if pending[hh] is not None:
                p_prev, alpha_prev = pending[hh]
                pv = weighted_values(hh, j - 1, p_prev)
                acc[hh] = pv if acc[hh] is None else acc[hh] * alpha_prev + pv
            s = s_cur[hh]
            cmax = jnp.max(s, axis=0, keepdims=True)
            m_new = cmax if m[hh] is None else jnp.maximum(m[hh], cmax)
            alpha = None if m[hh] is None else jnp.exp2(m[hh] - m_new)
            p = jnp.exp2(s - m_new)
            psum = jnp.sum(p, axis=0, keepdims=True)
            l[hh] = psum if alpha is None else l[hh] * alpha + psum
            m[hh] = m_new
            pending[hh] = (p.astype(BF16), alpha)
    outs = []
    for hh in heads:
        p_prev, alpha_prev = pending[hh]
        pv = weighted_values(hh, nch - 1, p_prev)
        total = pv if acc[hh] is None else acc[hh] * alpha_prev + pv
        outs.append(total / l[hh])
    o_ref[...] = jnp.concatenate(outs, axis=0).T.astype(BF16)


def _mla_attention(qa, ka, va, batch, seq, tq):
    t = qa.shape[0]
    nq = seq // tq
    return pl.pallas_call(
        _mla_kernel,
        grid=(batch, HA // 2, nq),
        in_specs=[
            pl.BlockSpec((tq, 2 * LANES), lambda b, j, i: (b * nq + i, j)),
            pl.BlockSpec((seq, 2 * LANES), lambda b, j, i: (b, j)),
            pl.BlockSpec((2 * D_V, seq), lambda b, j, i: (j, b)),
        ],
        out_specs=pl.BlockSpec((tq, 2 * D_V), lambda b, j, i: (b * nq + i, j)),
        out_shape=jax.ShapeDtypeStruct((t, HA * D_V), BF16),
        compiler_params=pltpu.CompilerParams(
            dimension_semantics=("parallel", "parallel", "parallel"),
            vmem_limit_bytes=VMEM_LIMIT),
        name="mla_attention",
    )(qa, ka, va)


def _window_kernel(q_ref, k_ref, vt_ref, bias_ref, sink_ref, o_ref, *, seq, qblocks):
    nb = seq // WB
    lane = lax.broadcasted_iota(jnp.int32, (WB, LANES), 1)
    lo = lane < HD
    for u in range(qblocks):
        n = pl.program_id(1) * qblocks + u
        rows = slice(u * WB, (u + 1) * WB)
        blocks = (jnp.maximum(n - 1, 0), n, jnp.minimum(n + 1, nb - 1))
        kw = jnp.concatenate([k_ref[pl.ds(pl.multiple_of(b * WB, WB), WB), :] for b in blocks],
                             axis=0)
        vts = [vt_ref[b] for b in blocks]
        edge = jnp.where(n == 0, 0, jnp.where(n == nb - 1, 2, 1))
        heads = []
        for kv in range(KVB):
            kd = kw[:, kv * LANES:(kv + 1) * LANES]
            qs = []
            for jj in range(GQ // 2):
                qp = q_ref[rows, (kv * 2 + jj) * LANES:(kv * 2 + jj + 1) * LANES]
                qs.append(jnp.where(lo, qp, jnp.zeros_like(qp)))
                qs.append(jnp.where(lo, jnp.zeros_like(qp), qp))
            q4 = jnp.concatenate(qs, axis=0)
            st = _dot_nt(kd, q4) + bias_ref[edge, kv]
            sink = sink_ref[kv]
            m = jnp.maximum(jnp.max(st, axis=0, keepdims=True), sink)
            e = jnp.exp2(st - m)
            denom = jnp.sum(e, axis=0, keepdims=True) + jnp.exp2(sink - m)
            vt = jnp.concatenate([v[kv * HD:(kv + 1) * HD, :] for v in vts], axis=1)
            ot = _dot(vt, e.astype(BF16)) / denom
            heads += [ot[:, g * WB:(g + 1) * WB] for g in range(GQ)]
        o_ref[rows, :] = jnp.concatenate(heads, axis=0).T.astype(BF16)


def _window_attention(qb, kb, vbt, bias, sink, batch, seq):
    t = qb.shape[0]
    nb = seq // WB
    qblocks = 4 if nb % 4 == 0 else 1
    nb //= qblocks
    return pl.pallas_call(
        functools.partial(_window_kernel, seq=seq, qblocks=qblocks),
        grid=(batch, nb),
        in_specs=[
            pl.BlockSpec((qblocks * WB, HB * HD), lambda b, n: (b * nb + n, 0)),
            pl.BlockSpec((seq, 2 * KVB * HD), lambda b, n: (b, 0)),
            pl.BlockSpec((seq // WB, KVB * HD, WB), lambda b, n: (b, 0, 0)),
            pl.BlockSpec((3, KVB, 3 * WB, GQ * WB), lambda b, n: (0, 0, 0, 0)),
            pl.BlockSpec((KVB, 1, GQ * WB), lambda b, n: (0, 0, 0)),
        ],
        out_specs=pl.BlockSpec((qblocks * WB, HB * HD), lambda b, n: (b * nb + n, 0)),
        out_shape=jax.ShapeDtypeStruct((t, HB * HD), BF16),
        compiler_params=pltpu.CompilerParams(
            dimension_semantics=("parallel", "parallel"), vmem_limit_bytes=VMEM_LIMIT),
        name="window_attention",
    )(qb, kb, vbt, bias, sink)


def _mixer_out_kernel(xa_ref, xb_ref, oa_ref, ob_ref, lnmix_ref, wg_ref, woa_ref, wob_ref, wout_ref,
                      lnffn_ref, wrh_ref, br_ref,
                      xn_ref, h2_ref, rf_ref, ri_ref, cnt_ref, carry_ref, *, na):
    i = pl.program_id(0)

    @pl.when(i == 0)
    def _():
        carry_ref[...] = jnp.zeros_like(carry_ref)

    x = _x_tile(xa_ref, xb_ref, na)
    tm = x.shape[0]
    h = _rms(x, lnmix_ref[...]).astype(BF16)
    g = _dot(h, wg_ref[...])
    ya = _dot(oa_ref[...], woa_ref[...])
    yb = _dot(ob_ref[...], wob_ref[...])
    merged = jax.nn.sigmoid(g[:, :D_MODEL]) * ya + jax.nn.sigmoid(g[:, D_MODEL:]) * yb
    xn = x + _dot(merged.astype(BF16), wout_ref[...])
    xn_ref[...] = xn
    h2 = _rms(xn, lnffn_ref[...])
    h2_ref[...] = _pack_rows(h2)

    h2_hi = h2.astype(BF16)
    h2_lo = (h2 - h2_hi.astype(F32)).astype(BF16)
    hi_terms = _dot(h2_hi, wrh_ref[...])
    logits = (hi_terms[:, :LANES] + hi_terms[:, LANES:] + _dot(h2_lo, wrh_ref[:, :LANES])
              + br_ref[...])

    lane = lax.broadcasted_iota(jnp.int32, (tm, LANES), 1)
    lanef = lane.astype(F32)
    big = float(LANES)
    is_g = (lane >= N_EXPERTS) & (lane < N_EXPERTS + N_GROUPS)
    gl = jnp.where(is_g, logits, NEG)
    gmax = jnp.max(gl, axis=1, keepdims=True)
    grp = jnp.min(jnp.where(gl == gmax, lanef, big), axis=1, keepdims=True) - N_EXPERTS
    gsum = jnp.sum(jnp.where(is_g, jnp.exp(gl - gmax), 0.0), axis=1, keepdims=True)
    grp_w = 1.0 / gsum
    is_e = (lanef >= grp * EPG) & (lanef < grp * EPG + EPG)
    el = jnp.where(is_e, logits, NEG)
    v1 = jnp.max(el, axis=1, keepdims=True)
    i1 = jnp.min(jnp.where((el == v1) & is_e, lanef, big), axis=1, keepdims=True)
    a1 = lanef == i1
    el2 = jnp.where(a1, NEG, el)
    v2 = jnp.max(el2, axis=1, keepdims=True)
    i2 = jnp.min(jnp.where((el2 == v2) & is_e & (~a1), lanef, big), axis=1, keepdims=True)
    a2 = lanef == i2
    ex = jnp.exp(v2 - v1)
    p1 = 1.0 / (1.0 + ex)
    gate1 = grp_w * p1
    gate2 = grp_w * (ex * p1)

    a1f = a1.astype(F32)
    a2f = a2.astype(F32)
    cnt = a1f + a2f
    r_i = lax.broadcasted_iota(jnp.int32, (tm, tm), 0)
    c_i = lax.broadcasted_iota(jnp.int32, (tm, tm), 1)
    tri = jnp.where(r_i > c_i, 1.0, 0.0).astype(BF16)
    before = _dot(tri, cnt.astype(BF16)) + carry_ref[...]
    rank1 = jnp.sum(a1f * before, axis=1, keepdims=True)
    rank2 = jnp.sum(a2f * before, axis=1, keepdims=True)
    carry = carry_ref[...] + jnp.sum(cnt, axis=0, keepdims=True)
    carry_ref[...] = carry
    cnt_ref[...] = jnp.broadcast_to(carry, cnt_ref.shape)

    rf_ref[...] = jnp.where(lane == 0, gate1, jnp.where(lane == 1, gate2, 0.0))
    ri = jnp.where(lane == 0, i1, jnp.where(lane == 1, i2,
                                            jnp.where(lane == 2, rank1,
                                                      jnp.where(lane == 3, rank2, 0.0))))
    ri_ref[...] = ri.astype(jnp.int32)


def _mixer_out(xs, oa, ob, p, tm):
    x_ops, x_specs, na, t = _x_operands(xs, tm)
    nt = t // tm
    row = lambda i: (i, 0)
    full = lambda i: (0, 0)
    return pl.pallas_call(
        functools.partial(_mixer_out_kernel, na=na),
        grid=(nt,),
        in_specs=[
            *x_specs,
            pl.BlockSpec((tm, HA * D_V), row),
            pl.BlockSpec((tm, HB * HD), row),
            pl.BlockSpec((1, D_MODEL), full),
            pl.BlockSpec((D_MODEL, 2 * D_MODEL), full),
            pl.BlockSpec((HA * D_V, D_MODEL), full),
            pl.BlockSpec((HB * HD, D_MODEL), full),
            pl.BlockSpec((D_MODEL, D_MODEL), full),
            pl.BlockSpec((1, D_MODEL), full),
            pl.BlockSpec((D_MODEL, 2 * LANES), full),
            pl.BlockSpec((1, LANES), full),
        ],
        out_specs=[
            pl.BlockSpec((tm, D_MODEL), row),
            pl.BlockSpec((tm, D_PACK), row),
            pl.BlockSpec((tm, LANES), row),
            pl.BlockSpec((tm, LANES), row),
            pl.BlockSpec((8, LANES), full),
        ],
        out_shape=[
            jax.ShapeDtypeStruct((t, D_MODEL), F32),
            jax.ShapeDtypeStruct((t, D_PACK), U32),
            jax.ShapeDtypeStruct((t, LANES), F32),
            jax.ShapeDtypeStruct((t, LANES), jnp.int32),
            jax.ShapeDtypeStruct((8, LANES), F32),
        ],
        scratch_shapes=[pltpu.VMEM((1, LANES), F32)],
        compiler_params=pltpu.CompilerParams(
            dimension_semantics=("arbitrary",), vmem_limit_bytes=VMEM_LIMIT),
        name="mixer_out",
    )(*x_ops, oa, ob, p["ln_mix"], p["w_g"], p["w_oa"], p["w_ob"], p["w_out"], p["ln_ffn"],
      p["w_r"], p["b_r"])


_ZERO_ROWS = MOE_ROWS // 2
_ZERO_SIZES = tuple(_ZERO_ROWS >> s for s in range((_ZERO_ROWS // SUBLANES).bit_length()))


def _dispatch_kernel(zstart_ref, zlen_ref, nu_ref, dest_ref, h2_ref, xb_ref, zbuf_ref, sem, zsem, *, td):
    i = pl.program_id(0)

    def row_copy(t, d):
        return pltpu.make_async_copy(h2_ref.at[pl.ds(t, 1)], xb_ref.at[pl.ds(d, 1)], sem)

    def issue(g, c):
        for u in range(ROWS_PER_ISSUE):
            t = g * ROWS_PER_ISSUE + u
            for k in range(TOP_K):
                row_copy(t, dest_ref[0, 0, 2 * t + k]).start(
                    priority=(TOP_K * u + k) % DMA_QUEUES)
        return c

    lax.fori_loop(0, td // ROWS_PER_ISSUE, issue, 0)

    def zero_copy(start, size):
        return pltpu.make_async_copy(zbuf_ref.at[pl.ds(0, size)], xb_ref.at[pl.ds(start, size)], zsem)

    def zero_pass(do):
        def per_expert(e, c):
            start = zstart_ref[e]
            head = (-start) & (SUBLANES - 1)
            for j in range(SUBLANES - 1):
                @pl.when(j < head)
                def _(row=start + j):
                    do(zero_copy(row, 1))

            start = start + head
            left = zlen_ref[e] - head
            for size in _ZERO_SIZES:
                take = (left & size) != 0

                @pl.when(take)
                def _(start=start, size=size):
                    do(zero_copy(pl.multiple_of(start, SUBLANES), size))

                start = start + jnp.where(take, size, 0)
            return c

        lax.fori_loop(0, N_EXPERTS, per_expert, 0)

        def unused_block(b, c):
            for part in range(MOE_ROWS // _ZERO_ROWS):
                row = pl.multiple_of(b * MOE_ROWS + part * _ZERO_ROWS, _ZERO_ROWS)
                do(zero_copy(row, _ZERO_ROWS))
            return c

        lax.fori_loop(nu_ref[0], xb_ref.shape[0] // MOE_ROWS, unused_block, 0)

    @pl.when(i == 0)
    def _():
        zbuf_ref[...] = jnp.zeros_like(zbuf_ref)
        zero_pass(lambda cp: cp.start())
        zero_pass(lambda cp: cp.wait())

    for _ in range(TOP_K):
        pltpu.make_async_copy(h2_ref, xb_ref.at[pl.ds(0, td)], sem).wait()


def _moe_dispatch(h2, dest, zstart, zlen, n_used, n_rows, td):
    t = h2.shape[0]
    nt = t // td
    dest3 = dest.reshape(nt, 1, 2 * td)
    return pl.pallas_call(
        functools.partial(_dispatch_kernel, td=td),
        grid_spec=pltpu.PrefetchScalarGridSpec(
            num_scalar_prefetch=3,
            grid=(nt,),
            in_specs=[
                pl.BlockSpec((1, 1, 2 * td), lambda i, zs, zl, nu: (i, 0, 0),
                             memory_space=pltpu.SMEM),
                pl.BlockSpec((td, D_PACK), lambda i, zs, zl, nu: (i, 0)),
            ],
            out_specs=pl.BlockSpec(memory_space=pl.ANY),
            scratch_shapes=[pltpu.VMEM((_ZERO_ROWS, D_PACK), U32),
                            pltpu.SemaphoreType.DMA, pltpu.SemaphoreType.DMA],
        ),
        out_shape=jax.ShapeDtypeStruct((n_rows, D_PACK), U32),
        compiler_params=pltpu.CompilerParams(dimension_semantics=("arbitrary",)),
        name="moe_dispatch",
    )(zstart, zlen, n_used, dest3, h2)


def _experts_kernel(be_ref, nu_ref, tok_ref, tnext_ref, h2_ref, wg_ref, wu_ref, wd_ref, y_ref,
                    xbuf_even, xbuf_odd, wgu_s, wd_s, sems):
    b = pl.program_id(0)
    n_used = nu_ref[0]
    bufs = (xbuf_even, xbuf_odd)

    def row_copy(t_ref, s, r):
        return pltpu.make_async_copy(h2_ref.at[pl.ds(t_ref[0, 0, r], 1)],
                                     bufs[s].at[pl.ds(r, 1)], sems.at[s])

    def wait_rows(s):
        pltpu.make_async_copy(h2_ref.at[pl.ds(0, MOE_ROWS)], bufs[s], sems.at[s]).wait()

    @pl.when((b == 0) | (be_ref[b] != be_ref[jnp.maximum(b - 1, 0)]))
    def _():
        wgu_s[:, :D_EXPERT] = wg_ref[0, 0].astype(BF16)
        wgu_s[:, D_EXPERT:] = wu_ref[0, 0].astype(BF16)
        wd_s[...] = wd_ref[0, 0].astype(BF16)

    @pl.when(b == 0)
    def _():
        def first(r, c):
            row_copy(tok_ref, 0, r).start()
            return c

        lax.fori_loop(0, MOE_ROWS, first, 0)

    for s in range(2):
        @pl.when((b < n_used) & (b % 2 == s))
        def _(s=s):
            wait_rows(s)
            for r in range(MOE_ROWS):
                row_copy(tnext_ref, 1 - s, r).start(priority=r % DMA_QUEUES)
            lo, hi = _unpack_rows(bufs[s][...])
            x = jnp.concatenate([lo.astype(BF16), hi.astype(BF16)], axis=1)
            au = _dot(x, wgu_s[...])
            a, u = au[:, :D_EXPERT], au[:, D_EXPERT:]
            hid = (a * jax.nn.sigmoid(a) * u).astype(BF16)
            y_ref[...] = _pack_rows(_dot(hid, wd_s[...]))

        @pl.when((b == n_used - 1) & (b % 2 == s))
        def _(s=s):
            wait_rows(1 - s)

    @pl.when(b >= n_used)
    def _():
        y_ref[...] = jnp.zeros_like(y_ref)


def _moe_experts(h2, row_tok, blk_expert, n_used, w_gate, w_up, w_down, layer):
    n_blocks = row_tok.shape[0] // MOE_ROWS
    tok3 = row_tok.reshape(n_blocks, 1, MOE_ROWS)
    expert = lambda b, be, nu: (layer, be[b], 0, 0)
    return pl.pallas_call(
        _experts_kernel,
        grid_spec=pltpu.PrefetchScalarGridSpec(
            num_scalar_prefetch=2,
            grid=(n_blocks,),
            in_specs=[
                pl.BlockSpec((1, 1, MOE_ROWS), lambda b, be, nu: (jnp.minimum(b, nu[0] - 1), 0, 0),
                             memory_space=pltpu.SMEM),
                pl.BlockSpec((1, 1, MOE_ROWS),
                             lambda b, be, nu: (jnp.minimum(b + 1, nu[0] - 1), 0, 0),
                             memory_space=pltpu.SMEM),
                pl.BlockSpec(memory_space=pl.ANY),
                pl.BlockSpec((1, 1, D_MODEL, D_EXPERT), expert),
                pl.BlockSpec((1, 1, D_MODEL, D_EXPERT), expert),
                pl.BlockSpec((1, 1, D_EXPERT, D_MODEL), expert),
            ],
            out_specs=pl.BlockSpec((MOE_ROWS, D_PACK), lambda b, be, nu: (b, 0)),
            scratch_shapes=[pltpu.VMEM((MOE_ROWS, D_PACK), U32),
                            pltpu.VMEM((MOE_ROWS, D_PACK), U32),
                            pltpu.VMEM((D_MODEL, 2 * D_EXPERT), BF16),
                            pltpu.VMEM((D_EXPERT, D_MODEL), BF16),
                            pltpu.SemaphoreType.DMA((2,))],
        ),
        out_shape=jax.ShapeDtypeStruct((n_blocks * MOE_ROWS, D_PACK), U32),
        compiler_params=pltpu.CompilerParams(
            dimension_semantics=("arbitrary",), vmem_limit_bytes=VMEM_LIMIT),
        name="moe_experts",
    )(blk_expert, n_used, tok3, tok3, h2, w_gate, w_up, w_down)


def _combine_kernel(dest_ref, dnext_ref, x_ref, rf_ref, lnf_ref, yb_ref, *rest, tc, split):
    outs, (ybuf_ref, sems) = rest[:-2], rest[-2:]
    i = pl.program_id(0)
    n = pl.num_programs(0)
    slot = i % 2

    def row_copy(src_row, s, k, t):
        return pltpu.make_async_copy(yb_ref.at[pl.ds(src_row, 1)],
                                     ybuf_ref.at[s, k, pl.ds(t, 1)], sems.at[s])

    def issue(d_ref, s):
        def body(g, c):
            for u in range(ROWS_PER_ISSUE):
                t = g * ROWS_PER_ISSUE + u
                for k in range(TOP_K):
                    row_copy(d_ref[0, 0, 2 * t + k], s, k, t).start(
                        priority=(TOP_K * u + k) % DMA_QUEUES)
            return c

        lax.fori_loop(0, tc // ROWS_PER_ISSUE, body, 0)

    @pl.when(i == 0)
    def _():
        issue(dest_ref, slot)

    @pl.when(i + 1 < n)
    def _():
        issue(dnext_ref, 1 - slot)

    for k in range(TOP_K):
        pltpu.make_async_copy(yb_ref.at[pl.ds(0, tc)], ybuf_ref.at[slot, k], sems.at[slot]).wait()

    rf = rf_ref[...]
    lo0, hi0 = _unpack_rows(ybuf_ref[slot, 0])
    lo1, hi1 = _unpack_rows(ybuf_ref[slot, 1])
    g0, g1 = rf[:, 0:1], rf[:, 1:2]
    out = x_ref[...] + jnp.concatenate([g0 * lo0 + g1 * lo1, g0 * hi0 + g1 * hi1], axis=1)
    if split is None:
        outs[0][...] = out
    else:
        out = _rms(out, lnf_ref[...])

        @pl.when(i < split)
        def _():
            outs[0][...] = out

        @pl.when(i >= split)
        def _():
            outs[1][...] = out


def _moe_combine(x, rf, dest, yb, ln_final, tc, final_rows):
    t = x.shape[0]
    nt = t // tc
    dest3 = dest.reshape(nt, 1, 2 * tc)
    row = lambda i: (i, 0)
    if final_rows is None:
        split = None
        out_specs = pl.BlockSpec((tc, D_MODEL), row)
        out_shape = jax.ShapeDtypeStruct((t, D_MODEL), F32)
    else:
        split = final_rows[0] // tc
        out_specs = [pl.BlockSpec((tc, D_MODEL), lambda i: (jnp.minimum(i, split - 1), 0)),
                     pl.BlockSpec((tc, D_MODEL), lambda i: (jnp.maximum(i - split, 0), 0))]
        out_shape = [jax.ShapeDtypeStruct((r, D_MODEL), F32) for r in final_rows]
    return pl.pallas_call(
        functools.partial(_combine_kernel, tc=tc, split=split),
        grid=(nt,),
        in_specs=[
            pl.BlockSpec((1, 1, 2 * tc), lambda i: (i, 0, 0), memory_space=pltpu.SMEM),
            pl.BlockSpec((1, 1, 2 * tc), lambda i: (jnp.minimum(i + 1, nt - 1), 0, 0),
                         memory_space=pltpu.SMEM),
            pl.BlockSpec((tc, D_MODEL), row),
            pl.BlockSpec((tc, LANES), row),
            pl.BlockSpec((1, D_MODEL), lambda i: (0, 0)),
            pl.BlockSpec(memory_space=pl.ANY),
        ],
        out_specs=out_specs,
        out_shape=out_shape,
        scratch_shapes=[pltpu.VMEM((2, TOP_K, tc, D_PACK), U32),
                        pltpu.SemaphoreType.DMA((2,))],
        compiler_params=pltpu.CompilerParams(
            dimension_semantics=("arbitrary",), vmem_limit_bytes=VMEM_LIMIT),
        name="moe_combine",
    )(dest3, dest3, x, rf, ln_final, yb)


def _rope_tables(seq):
    inv = ROPE_BASE ** (-jnp.arange(0, D_ROPE, 2, dtype=F32) / D_ROPE)
    ang = jnp.arange(seq, dtype=F32)[:, None] * inv[None, :]
    cos, sin = jnp.cos(ang), jnp.sin(ang)
    z = lambda w: jnp.zeros((seq, w), F32)
    tail = LANES - D_NOPE - D_ROPE
    return {
        "cos": jnp.concatenate([jnp.ones((seq, D_NOPE), F32), cos, cos, z(tail)], axis=1),
        "sin_a": jnp.concatenate([z(D_NOPE), -sin, z(HALF_ROPE), z(tail)], axis=1),
        "sin_b": jnp.concatenate([z(D_NOPE), z(HALF_ROPE), sin, z(tail)], axis=1),
    }


def _t5_bucket(rel):
    half = NUM_BUCKETS // 2
    max_exact = half // 2
    n = jnp.abs(rel)
    nf = jnp.maximum(n, 1).astype(F32)
    large = max_exact + (jnp.log(nf / max_exact) / math.log(MAX_DIST / max_exact)
                         * (half - max_exact)).astype(jnp.int32)
    large = jnp.minimum(large, half - 1)
    return jnp.where(rel > 0, half, 0) + jnp.where(n < max_exact, n, large)


def _window_bias(rel_bias):
    qi = jnp.arange(WB)[:, None]
    kj = jnp.arange(3 * WB)[None, :]
    rel = kj - WB - qi
    onehot = _t5_bucket(rel)[:, :, None] == jnp.arange(NUM_BUCKETS)
    bias = jnp.sum(jnp.where(onehot[..., None], rel_bias.astype(F32), 0.0), axis=2) * LOG2E
    bias = jnp.where((jnp.abs(rel) <= WINDOW)[:, :, None], bias, NEG)
    bias = jnp.transpose(bias, (2, 0, 1)).reshape(KVB, GQ * WB, 3 * WB)
    bias = jnp.swapaxes(bias, 1, 2)
    key = jnp.arange(3 * WB)[None, :, None]
    return jnp.stack([jnp.where(key < WB, NEG, bias), bias, jnp.where(key >= 2 * WB, NEG, bias)])


def _pack_layer(l, w_in, ln_mix, ln_cq, ln_ckv, w_uq, w_ukv, w_oa, sink, w_ob, w_out, ln_ffn,
                w_gr, b_gr, w_er, b_er):
    wi = w_in[l]
    o = 0
    c_q, o = wi[:, o:o + D_CQ], o + D_CQ
    c_kv, o = wi[:, o:o + D_C], o + D_C
    k_r, o = wi[:, o:o + D_ROPE], o + D_ROPE
    q_b, o = wi[:, o:o + HB * HD], o + HB * HD
    k_b, o = wi[:, o:o + KVB * HD], o + KVB * HD
    v_b, o = wi[:, o:o + KVB * HD], o + KVB * HD
    g_ab = wi[:, o:]
    zc = lambda rows, w: jnp.zeros((rows, w), F32)
    dup = lambda w: jnp.concatenate([w[:, kv * HD:(kv + 1) * HD] for kv in range(KVB) for _ in (0, 1)],
                                    axis=1)
    kr_blk = jnp.concatenate([zc(D_MODEL, D_NOPE), k_r, zc(D_MODEL, LANES - D_NOPE - D_ROPE)], axis=1)
    w_mix = jnp.concatenate([c_q, c_kv, kr_blk, q_b * (WIN_SCALE * LOG2E), dup(k_b)], axis=1)

    uq = w_uq[l].reshape(D_CQ, HA, D_NOPE + D_ROPE)
    uq = jnp.pad(uq, ((0, 0), (0, 0), (0, LANES - D_NOPE - D_ROPE))).reshape(D_CQ, HA * LANES)
    ukv = w_ukv[l].reshape(D_C, HA, D_NOPE + D_V)
    uk = jnp.pad(ukv[:, :, :D_NOPE], ((0, 0), (0, 0), (0, LANES - D_NOPE))).reshape(D_C, HA * LANES)
    uv = ukv[:, :, D_NOPE:].reshape(D_C, HA * D_V)

    w_r = jnp.concatenate([w_er[l], w_gr[l], zc(D_MODEL, LANES - N_EXPERTS - N_GROUPS)], axis=1)
    w_r_hi = w_r.astype(BF16)
    b_r = jnp.concatenate([b_er[l], b_gr[l], jnp.zeros((LANES - N_EXPERTS - N_GROUPS,), F32)])
    sink_row = jnp.repeat(sink[l].astype(F32).reshape(KVB, GQ) * LOG2E, WB, axis=1)[:, None, :]
    return {
        "ln_mix": ln_mix[l][None, :], "w_mix": w_mix.astype(BF16), "w_vbt": v_b.T.astype(BF16),
        "ln_cq": ln_cq[l][None, :], "ln_ckv": ln_ckv[l][None, :],
        "w_uq": uq.astype(BF16), "w_uk": uk.astype(BF16), "w_uvt": uv.T.astype(BF16),
        "w_g": g_ab.astype(BF16), "w_oa": w_oa[l].astype(BF16), "w_ob": w_ob[l].astype(BF16),
        "w_out": w_out[l].astype(BF16), "ln_ffn": ln_ffn[l][None, :],
        "w_r": jnp.concatenate([w_r_hi, (w_r - w_r_hi.astype(F32)).astype(BF16)], axis=1),
        "b_r": b_r[None, :],
        "sink": sink_row,
    }


def _row_tile(n, pref):
    tile = pref
    while n % tile:
        tile //= 2
    return tile


def _trunk(xs, batch, seq, tabs, bias, layers, expert_w, ln_final):
    final_rows = tuple(x.shape[0] for x in xs)
    t = sum(final_rows)
    tm = _row_tile(seq, 512)
    tq = _row_tile(seq, 512)
    tc = _row_tile(t, 256)
    n_blocks = (t * TOP_K) // MOE_ROWS + N_EXPERTS
    n_rows = n_blocks * MOE_ROWS
    for l, p in enumerate(layers):
        qa, ka, vat, qb, kb, vbt = _mixer_in(xs, p, tabs, seq, tm)
        oa = _mla_attention(qa, ka, vat, batch, seq, tq)
        ob = _window_attention(qb, kb, vbt, bias, p["sink"], batch, seq)
        x, h2, rf, ri, cnt = _mixer_out(xs, oa, ob, p, tm)
        counts = cnt[0, :N_EXPERTS].astype(jnp.int32)
        padded = ((counts + MOE_ROWS - 1) // MOE_ROWS) * MOE_ROWS
        pad_end = jnp.cumsum(padded)
        pad_start = pad_end - padded
        onehot = ri[:, 0:TOP_K, None] == jnp.arange(N_EXPERTS, dtype=jnp.int32)
        dest = (jnp.sum(jnp.where(onehot, pad_start, 0), axis=-1) + ri[:, TOP_K:2 * TOP_K]).reshape(-1)
        blk_row = jnp.arange(n_blocks, dtype=jnp.int32)[:, None] * MOE_ROWS
        blk_expert = jnp.minimum(jnp.sum((pad_end[None, :] <= blk_row).astype(jnp.int32), axis=1),
                                 N_EXPERTS - 1)
        n_used = (pad_end[-1:] // MOE_ROWS).astype(jnp.int32)
        row_tok = jnp.zeros((n_rows,), jnp.int32).at[dest].set(
            jnp.arange(t * TOP_K, dtype=jnp.int32) // TOP_K, unique_indices=True)
        yb = _moe_experts(h2, row_tok, blk_expert, n_used, *expert_w, l)
        last = l == len(layers) - 1
        x = _moe_combine(x, rf, dest, yb, ln_final, tc, final_rows if last else None)
        xs = (x, None)
    return x


def kernel(x_prompt, x_sample, rel_bias, ln_mix, w_in, ln_cq, ln_ckv, w_uq, w_ukv, w_oa, sink, w_ob,
           w_out, ln_ffn, w_gr, b_gr, w_er, b_er, w_gate, w_up, w_down, ln_final):
    bp, seq, d = x_prompt.shape
    bs = x_sample.shape[0]
    assert x_sample.shape[1] == seq and seq % WB == 0 and seq >= 3 * WB
    tabs = _rope_tables(seq)
    bias = _window_bias(rel_bias)
    layers = [_pack_layer(l, w_in, ln_mix, ln_cq, ln_ckv, w_uq, w_ukv, w_oa, sink, w_ob, w_out,
                          ln_ffn, w_gr, b_gr, w_er, b_er)
              for l in range(w_in.shape[0])]
    xs = (x_prompt.reshape(bp * seq, d), x_sample.reshape(bs * seq, d))
    y_prompt, y_sample = _trunk(xs, bp + bs, seq, tabs, bias, layers, (w_gate, w_up, w_down),
                                ln_final[None, :])
    return (y_prompt.reshape(bp, seq, d), y_sample.reshape(bs, seq, d))
```

```python
import functools
import math

import jax
import jax.numpy as jnp
from jax import lax
from jax.experimental import pallas as pl
from jax.experimental.pallas import tpu as pltpu

D_MODEL = 1024
DEPTH = 2
HA = 8
D_NOPE = 64
D_ROPE = 32
D_V = 64
D_CQ = 384
D_C = 256
ROPE_BASE = 10000.0
MLA_SCALE = (D_NOPE + D_ROPE) ** -0.5
HB = 8
KVB = 2
GQ = HB // KVB
HD = 64
WINDOW = 128
WB = 128
WIN_SCALE = HD ** -0.5
NUM_BUCKETS = 32
MAX_DIST = 128
N_GROUPS = 4
EPG = 8
N_EXPERTS = N_GROUPS * EPG
TOP_K = 2
D_EXPERT = 256
EPS = 1e-6
NEG = -1e30
LOG2E = math.log2(math.e)

LANES = 128
SUBLANES = 8
HALF_ROPE = D_ROPE // 2
MOE_ROWS = 512
MLA_KEYS = 256
DMA_QUEUES = 2
ROWS_PER_ISSUE = 4
VMEM_LIMIT = 48 * 1024 * 1024

C_CQ = 0
C_CKV = C_CQ + D_CQ
C_KR = C_CKV + D_C
C_QB = C_KR + LANES
C_KB = C_QB + HB * HD
N_MIX = C_KB + 2 * KVB * HD

F32 = jnp.float32
BF16 = jnp.bfloat16
U32 = jnp.uint32
D_PACK = D_MODEL // 2
ROW_SLABS = D_PACK // LANES


def _rms(x, g):
    ms = jnp.mean(x * x, axis=-1, keepdims=True)
    return x * lax.rsqrt(ms + EPS) * g


def _dot(a, b):
    return jnp.dot(a, b, preferred_element_type=F32)


def _dot_nt(a, b):
    return lax.dot_general(a, b, (((1,), (1,)), ((), ())), preferred_element_type=F32)


def _pack_rows(x):
    n = x.shape[1] // 2
    lo = lax.bitcast_convert_type(x[:, :n].astype(BF16).astype(F32), U32)
    hi = lax.bitcast_convert_type(x[:, n:].astype(BF16).astype(F32), U32)
    return hi | (lo >> 16)


def _unpack_rows(p):
    lo = lax.bitcast_convert_type(p << 16, F32)
    hi = lax.bitcast_convert_type(p & jnp.uint32(0xFFFF0000), F32)
    return lo, hi


def _load_rows(ref, n):
    return jnp.concatenate([ref[pl.ds(c, n, stride=ROW_SLABS), :] for c in range(ROW_SLABS)], axis=1)


def _store_rows(ref, rows):
    n = rows.shape[0]
    for c in range(ROW_SLABS):
        ref[pl.ds(c, n, stride=ROW_SLABS), :] = rows[:, c * LANES:(c + 1) * LANES]


def _row_at(i):
    return pl.ds(pl.multiple_of(i * ROW_SLABS, ROW_SLABS), ROW_SLABS)


def _x_operands(xs, tile):
    xa, xb = xs
    na = xa.shape[0] // tile
    rows = xa.shape[0] + (0 if xb is None else xb.shape[0])
    specs = (pl.BlockSpec((tile, D_MODEL), lambda i, *_: (jnp.minimum(i, na - 1), 0)),
             pl.BlockSpec((tile, D_MODEL), lambda i, *_: (jnp.maximum(i - na, 0), 0)))
    return (xa, xa if xb is None else xb), specs, na, rows


def _x_tile(xa_ref, xb_ref, na):
    return jnp.where(pl.program_id(0) < na, xa_ref[...], xb_ref[...])


def _rope(t, cos, sin_a, sin_b):
    return (t * cos + pltpu.roll(t, LANES - HALF_ROPE, 1) * sin_a
            + pltpu.roll(t, HALF_ROPE, 1) * sin_b)


def _mixer_in_kernel(xa_ref, xb_ref, lnmix_ref, win_ref, wvbt_ref, lncq_ref, lnckv_ref, wuq_ref,
                     wuk_ref, wuvt_ref, cos_ref, sina_ref, sinb_ref,
                     qa_ref, ka_ref, vat_ref, qb_ref, kb_ref, vbt_ref, *, na):
    h = _rms(_x_tile(xa_ref, xb_ref, na), lnmix_ref[...]).astype(BF16)
    cq = _dot(h, win_ref[:, C_CQ:C_CKV])
    ckv = _dot(h, win_ref[:, C_CKV:C_KR])
    kr = _dot(h, win_ref[:, C_KR:C_QB])
    qb_ref[...] = _dot(h, win_ref[:, C_QB:C_KB]).astype(BF16)
    kb_ref[...] = _dot(h, win_ref[:, C_KB:N_MIX]).astype(BF16)
    vbt = _dot_nt(wvbt_ref[...], h).astype(BF16)
    for c in range(vbt_ref.shape[0]):
        vbt_ref[c] = vbt[:, c * WB:(c + 1) * WB]
    cqn = _rms(cq, lncq_ref[...]).astype(BF16)
    ckvn = _rms(ckv, lnckv_ref[...]).astype(BF16)
    q = _dot(cqn, wuq_ref[...])
    k = _dot(ckvn, wuk_ref[...])
    vat_ref[...] = _dot_nt(wuvt_ref[...], ckvn).astype(BF16)
    cos, sin_a, sin_b = cos_ref[...], sina_ref[...], sinb_ref[...]
    kr_rot = _rope(kr, cos, sin_a, sin_b)
    for hd in range(HA):
        sl = slice(hd * LANES, (hd + 1) * LANES)
        qa_ref[:, sl] = (_rope(q[:, sl], cos, sin_a, sin_b) * (MLA_SCALE * LOG2E)).astype(BF16)
        ka_ref[:, sl] = (k[:, sl] + kr_rot).astype(BF16)


def _mixer_in(xs, p, tabs, seq, tm):
    x_ops, x_specs, na, t = _x_operands(xs, tm)
    nt = t // tm
    spt = seq // tm
    row = lambda i: (i, 0)
    full = lambda i: (0, 0)
    tab = lambda i: (i % spt, 0)
    rows = lambda w: (pl.BlockSpec((tm, w), row), jax.ShapeDtypeStruct((t, w), BF16))
    outs = [
        rows(HA * LANES),
        rows(HA * LANES),
        (pl.BlockSpec((HA * D_V, tm), lambda i: (0, i)),
         jax.ShapeDtypeStruct((HA * D_V, t), BF16)),
        rows(HB * HD),
        rows(2 * KVB * HD),
        (pl.BlockSpec((tm // WB, KVB * HD, WB), lambda i: (i, 0, 0)),
         jax.ShapeDtypeStruct((t // WB, KVB * HD, WB), BF16)),
    ]
    out_specs = [o[0] for o in outs]
    out_shape = [o[1] for o in outs]
    return pl.pallas_call(
        functools.partial(_mixer_in_kernel, na=na),
        grid=(nt,),
        in_specs=[
            *x_specs,
            pl.BlockSpec((1, D_MODEL), full),
            pl.BlockSpec((D_MODEL, N_MIX), full),
            pl.BlockSpec((KVB * HD, D_MODEL), full),
            pl.BlockSpec((1, D_CQ), full),
            pl.BlockSpec((1, D_C), full),
            pl.BlockSpec((D_CQ, HA * LANES), full),
            pl.BlockSpec((D_C, HA * LANES), full),
            pl.BlockSpec((HA * D_V, D_C), full),
            pl.BlockSpec((tm, LANES), tab),
            pl.BlockSpec((tm, LANES), tab),
            pl.BlockSpec((tm, LANES), tab),
        ],
        out_specs=out_specs,
        out_shape=out_shape,
        compiler_params=pltpu.CompilerParams(
            dimension_semantics=("parallel",), vmem_limit_bytes=VMEM_LIMIT),
        name="mixer_in",
    )(*x_ops, p["ln_mix"], p["w_mix"], p["w_vbt"], p["ln_cq"], p["ln_ckv"], p["w_uq"], p["w_uk"], p["w_uvt"],
      tabs["cos"], tabs["sin_a"], tabs["sin_b"])


def _mla_kernel(q_ref, k_ref, vt_ref, o_ref):
    heads = range(2)
    nch = k_ref.shape[0] // MLA_KEYS

    def scores(j, hh):
        sl = slice(hh * LANES, (hh + 1) * LANES)
        return _dot_nt(k_ref[j * MLA_KEYS:(j + 1) * MLA_KEYS, sl], q_ref[:, sl])

    def weighted_values(hh, j, p):
        return _dot(vt_ref[hh * D_V:(hh + 1) * D_V, j * MLA_KEYS:(j + 1) * MLA_KEYS], p)

    s_next = [scores(0, hh) for hh in heads]
    m = [None, None]
    l = [None, None]
    acc = [None, None]
    pending = [None, None]
    for j in range(nch):
        s_cur = s_next
        if j + 1 < nch:
            s_next = [scores(j + 1, hh) for hh in heads]
        for hh in heads:
            if pending[hh] is not None:
                p_prev, alpha_prev = pending[hh]
                pv = weighted_values(hh, j - 1, p_prev)
                acc[hh] = pv if acc[hh] is None else acc[hh] * alpha_prev + pv
            s = s_cur[hh]
            cmax = jnp.max(s, axis=0, keepdims=True)
            m_new = cmax if m[hh] is None else jnp.maximum(m[hh], cmax)
            alpha = None if m[hh] is None else jnp.exp2(m[hh] - m_new)
            p = jnp.exp2(s - m_new)
            psum = jnp.sum(p, axis=0, keepdims=True)
            l[hh] = psum if alpha is None else l[hh] * alpha + psum
            m[hh] = m_new
            pending[hh] = (p.astype(BF16), alpha)
    outs = []
    for hh in heads:
        p_prev, alpha_prev = pending[hh]
        pv = weighted_values(hh, nch - 1, p_prev)
        total = pv if acc[hh] is None else acc[hh] * alpha_prev + pv
        outs.append(total / l[hh])
    o_ref[...] = jnp.concatenate(outs, axis=0).T.astype(BF16)


def _mla_attention(qa, ka, va, batch, seq, tq):
    t = qa.shape[0]
    nq = seq // tq
    return pl.pallas_call(
        _mla_kernel,
        grid=(batch, HA // 2, nq),
        in_specs=[
            pl.BlockSpec((tq, 2 * LANES), lambda b, j, i: (b * nq + i, j)),
            pl.BlockSpec((seq, 2 * LANES), lambda b, j, i: (b, j)),
            pl.BlockSpec((2 * D_V, seq), lambda b, j, i: (j, b)),
        ],
        out_specs=pl.BlockSpec((tq, 2 * D_V), lambda b, j, i: (b * nq + i, j)),
        out_shape=jax.ShapeDtypeStruct((t, HA * D_V), BF16),
        compiler_params=pltpu.CompilerParams(
            dimension_semantics=("parallel", "parallel", "parallel"),
            vmem_limit_bytes=VMEM_LIMIT),
        name="mla_attention",
    )(qa, ka, va)


def _window_kernel(q_ref, k_ref, vt_ref, bias_ref, sink_ref, o_ref, *, seq, qblocks):
    nb = seq // WB
    lane = lax.broadcasted_iota(jnp.int32, (WB, LANES), 1)
    lo = lane < HD
    for u in range(qblocks):
        n = pl.program_id(1) * qblocks + u
        rows = slice(u * WB, (u + 1) * WB)
        blocks = (jnp.maximum(n - 1, 0), n, jnp.minimum(n + 1, nb - 1))
        kw = jnp.concatenate([k_ref[pl.ds(pl.multiple_of(b * WB, WB), WB), :] for b in blocks],
                             axis=0)
        vts = [vt_ref[b] for b in blocks]
        edge = jnp.where(n == 0, 0, jnp.where(n == nb - 1, 2, 1))
        heads = []
        for kv in range(KVB):
            kd = kw[:, kv * LANES:(kv + 1) * LANES]
            qs = []
            for jj in range(GQ // 2):
                qp = q_ref[rows, (kv * 2 + jj) * LANES:(kv * 2 + jj + 1) * LANES]
                qs.append(jnp.where(lo, qp, jnp.zeros_like(qp)))
                qs.append(jnp.where(lo, jnp.zeros_like(qp), qp))
            q4 = jnp.concatenate(qs, axis=0)
            st = _dot_nt(kd, q4) + bias_ref[edge, kv]
            sink = sink_ref[kv]
            m = jnp.maximum(jnp.max(st, axis=0, keepdims=True), sink)
            e = jnp.exp2(st - m)
            denom = jnp.sum(e, axis=0, keepdims=True) + jnp.exp2(sink - m)
            vt = jnp.concatenate([v[kv * HD:(kv + 1) * HD, :] for v in vts], axis=1)
            ot = _dot(vt, e.astype(BF16)) / denom
            heads += [ot[:, g * WB:(g + 1) * WB] for g in range(GQ)]
        o_ref[rows, :] = jnp.concatenate(heads, axis=0).T.astype(BF16)


def _window_attention(qb, kb, vbt, bias, sink, batch, seq):
    t = qb.shape[0]
    nb = seq // WB
    qblocks = 4 if nb % 4 == 0 else 1
    nb //= qblocks
    return pl.pallas_call(
        functools.partial(_window_kernel, seq=seq, qblocks=qblocks),
        grid=(batch, nb),
        in_specs=[
            pl.BlockSpec((qblocks * WB, HB * HD), lambda b, n: (b * nb + n, 0)),
            pl.BlockSpec((seq, 2 * KVB * HD), lambda b, n: (b, 0)),
            pl.BlockSpec((seq // WB, KVB * HD, WB), lambda b, n: (b, 0, 0)),
            pl.BlockSpec((3, KVB, 3 * WB, GQ * WB), lambda b, n: (0, 0, 0, 0)),
            pl.BlockSpec((KVB, 1, GQ * WB), lambda b, n: (0, 0, 0)),
        ],
        out_specs=pl.BlockSpec((qblocks * WB, HB * HD), lambda b, n: (b * nb + n, 0)),
        out_shape=jax.ShapeDtypeStruct((t, HB * HD), BF16),
        compiler_params=pltpu.CompilerParams(
            dimension_semantics=("parallel", "parallel"), vmem_limit_bytes=VMEM_LIMIT),
        name="window_attention",
    )(qb, kb, vbt, bias, sink)


def _mixer_out_kernel(xa_ref, xb_ref, oa_ref, ob_ref, lnmix_ref, wg_ref, woa_ref, wob_ref, wout_ref,
                      lnffn_ref, wrh_ref, br_ref,
                      xn_ref, h2_ref, rf_ref, ri_ref, cnt_ref, carry_ref, *, na):
    i = pl.program_id(0)

    @pl.when(i == 0)
    def _():
        carry_ref[...] = jnp.zeros_like(carry_ref)

    x = _x_tile(xa_ref, xb_ref, na)
    tm = x.shape[0]
    h = _rms(x, lnmix_ref[...]).astype(BF16)
    g = _dot(h, wg_ref[...])
    ya = _dot(oa_ref[...], woa_ref[...])
    yb = _dot(ob_ref[...], wob_ref[...])
    merged = jax.nn.sigmoid(g[:, :D_MODEL]) * ya + jax.nn.sigmoid(g[:, D_MODEL:]) * yb
    xn = x + _dot(merged.astype(BF16), wout_ref[...])
    xn_ref[...] = xn
    h2 = _rms(xn, lnffn_ref[...])
    _store_rows(h2_ref, _pack_rows(h2))

    h2_hi = h2.astype(BF16)
    h2_lo = (h2 - h2_hi.astype(F32)).astype(BF16)
    hi_terms = _dot(h2_hi, wrh_ref[...])
    logits = (hi_terms[:, :LANES] + hi_terms[:, LANES:] + _dot(h2_lo, wrh_ref[:, :LANES])
              + br_ref[...])

    lane = lax.broadcasted_iota(jnp.int32, (tm, LANES), 1)
    lanef = lane.astype(F32)
    big = float(LANES)
    is_g = (lane >= N_EXPERTS) & (lane < N_EXPERTS + N_GROUPS)
    gl = jnp.where(is_g, logits, NEG)
    gmax = jnp.max(gl, axis=1, keepdims=True)
    grp = jnp.min(jnp.where(gl == gmax, lanef, big), axis=1, keepdims=True) - N_EXPERTS
    gsum = jnp.sum(jnp.where(is_g, jnp.exp(gl - gmax), 0.0), axis=1, keepdims=True)
    grp_w = 1.0 / gsum
    is_e = (lanef >= grp * EPG) & (lanef < grp * EPG + EPG)
    el = jnp.where(is_e, logits, NEG)
    v1 = jnp.max(el, axis=1, keepdims=True)
    i1 = jnp.min(jnp.where((el == v1) & is_e, lanef, big), axis=1, keepdims=True)
    a1 = lanef == i1
    el2 = jnp.where(a1, NEG, el)
    v2 = jnp.max(el2, axis=1, keepdims=True)
    i2 = jnp.min(jnp.where((el2 == v2) & is_e & (~a1), lanef, big), axis=1, keepdims=True)
    a2 = lanef == i2
    ex = jnp.exp(v2 - v1)
    p1 = 1.0 / (1.0 + ex)
    gate1 = grp_w * p1
    gate2 = grp_w * (ex * p1)

    a1f = a1.astype(F32)
    a2f = a2.astype(F32)
    cnt = a1f + a2f
    r_i = lax.broadcasted_iota(jnp.int32, (tm, tm), 0)
    c_i = lax.broadcasted_iota(jnp.int32, (tm, tm), 1)
    tri = jnp.where(r_i > c_i, 1.0, 0.0).astype(BF16)
    before = _dot(tri, cnt.astype(BF16)) + carry_ref[...]
    rank1 = jnp.sum(a1f * before, axis=1, keepdims=True)
    rank2 = jnp.sum(a2f * before, axis=1, keepdims=True)
    carry = carry_ref[...] + jnp.sum(cnt, axis=0, keepdims=True)
    carry_ref[...] = carry
    cnt_ref[...] = jnp.broadcast_to(carry, cnt_ref.shape)

    rf_ref[...] = jnp.where(lane == 0, gate1, jnp.where(lane == 1, gate2, 0.0))
    ri = jnp.where(lane == 0, i1, jnp.where(lane == 1, i2,
                                            jnp.where(lane == 2, rank1,
                                                      jnp.where(lane == 3, rank2, 0.0))))
    ri_ref[...] = ri.astype(jnp.int32)


def _mixer_out(xs, oa, ob, p, tm):
    x_ops, x_specs, na, t = _x_operands(xs, tm)
    nt = t // tm
    row = lambda i: (i, 0)
    full = lambda i: (0, 0)
    return pl.pallas_call(
        functools.partial(_mixer_out_kernel, na=na),
        grid=(nt,),
        in_specs=[
            *x_specs,
            pl.BlockSpec((tm, HA * D_V), row),
            pl.BlockSpec((tm, HB * HD), row),
            pl.BlockSpec((1, D_MODEL), full),
            pl.BlockSpec((D_MODEL, 2 * D_MODEL), full),
            pl.BlockSpec((HA * D_V, D_MODEL), full),
            pl.BlockSpec((HB * HD, D_MODEL), full),
            pl.BlockSpec((D_MODEL, D_MODEL), full),
            pl.BlockSpec((1, D_MODEL), full),
            pl.BlockSpec((D_MODEL, 2 * LANES), full),
            pl.BlockSpec((1, LANES), full),
        ],
        out_specs=[
            pl.BlockSpec((tm, D_MODEL), row),
            pl.BlockSpec((ROW_SLABS * tm, LANES), row),
            pl.BlockSpec((tm, LANES), row),
            pl.BlockSpec((tm, LANES), row),
            pl.BlockSpec((8, LANES), full),
        ],
        out_shape=[
            jax.ShapeDtypeStruct((t, D_MODEL), F32),
            jax.ShapeDtypeStruct((ROW_SLABS * t, LANES), U32),
            jax.ShapeDtypeStruct((t, LANES), F32),
            jax.ShapeDtypeStruct((t, LANES), jnp.int32),
            jax.ShapeDtypeStruct((8, LANES), F32),
        ],
        scratch_shapes=[pltpu.VMEM((1, LANES), F32)],
        compiler_params=pltpu.CompilerParams(
            dimension_semantics=("arbitrary",), vmem_limit_bytes=VMEM_LIMIT),
        name="mixer_out",
    )(*x_ops, oa, ob, p["ln_mix"], p["w_g"], p["w_oa"], p["w_ob"], p["w_out"], p["ln_ffn"],
      p["w_r"], p["b_r"])


_ZERO_ROWS = MOE_ROWS // 2
_ZERO_SIZES = tuple(_ZERO_ROWS >> s for s in range((_ZERO_ROWS // SUBLANES).bit_length()))


def _dispatch_kernel(zstart_ref, zlen_ref, nu_ref, dest_ref, h2_ref, xb_ref, zbuf_ref, sem, zsem, *, td):
    i = pl.program_id(0)

    def row_copy(t, d):
        return pltpu.make_async_copy(h2_ref.at[_row_at(t)], xb_ref.at[_row_at(d)], sem)

    def issue(g, c):
        for u in range(ROWS_PER_ISSUE):
            t = g * ROWS_PER_ISSUE + u
            for k in range(TOP_K):
                row_copy(t, dest_ref[0, 0, 2 * t + k]).start(
                    priority=(TOP_K * u + k) % DMA_QUEUES)
        return c

    lax.fori_loop(0, td // ROWS_PER_ISSUE, issue, 0)

    def zero_copy(start, size):
        first = pl.multiple_of(start * ROW_SLABS, ROW_SLABS if size == 1 else SUBLANES * ROW_SLABS)
        return pltpu.make_async_copy(zbuf_ref.at[pl.ds(0, size * ROW_SLABS)],
                                     xb_ref.at[pl.ds(first, size * ROW_SLABS)], zsem)

    def zero_pass(do):
        def per_expert(e, c):
            start = zstart_ref[e]
            head = (-start) & (SUBLANES - 1)
            for j in range(SUBLANES - 1):
                @pl.when(j < head)
                def _(row=start + j):
                    do(zero_copy(row, 1))

            start = start + head
            left = zlen_ref[e] - head
            for size in _ZERO_SIZES:
                take = (left & size) != 0

                @pl.when(take)
                def _(start=start, size=size):
                    do(zero_copy(pl.multiple_of(start, SUBLANES), size))

                start = start + jnp.where(take, size, 0)
            return c

        lax.fori_loop(0, N_EXPERTS, per_expert, 0)

        def unused_block(b, c):
            for part in range(MOE_ROWS // _ZERO_ROWS):
                row = pl.multiple_of(b * MOE_ROWS + part * _ZERO_ROWS, _ZERO_ROWS)
                do(zero_copy(row, _ZERO_ROWS))
            return c

        lax.fori_loop(nu_ref[0], xb_ref.shape[0] // (ROW_SLABS * MOE_ROWS), unused_block, 0)

    @pl.when(i == 0)
    def _():
        zbuf_ref[...] = jnp.zeros_like(zbuf_ref)
        zero_pass(lambda cp: cp.start())
        zero_pass(lambda cp: cp.wait())

    for _ in range(TOP_K):
        pltpu.make_async_copy(h2_ref, xb_ref.at[pl.ds(0, td * ROW_SLABS)], sem).wait()


def _moe_dispatch(h2, dest, zstart, zlen, n_used, n_rows, td):
    t = h2.shape[0] // ROW_SLABS
    nt = t // td
    dest3 = dest.reshape(nt, 1, 2 * td)
    return pl.pallas_call(
        functools.partial(_dispatch_kernel, td=td),
        grid_spec=pltpu.PrefetchScalarGridSpec(
            num_scalar_prefetch=3,
            grid=(nt,),
            in_specs=[
                pl.BlockSpec((1, 1, 2 * td), lambda i, zs, zl, nu: (i, 0, 0),
                             memory_space=pltpu.SMEM),
                pl.BlockSpec((ROW_SLABS * td, LANES), lambda i, zs, zl, nu: (i, 0)),
            ],
            out_specs=pl.BlockSpec(memory_space=pl.ANY),
            scratch_shapes=[pltpu.VMEM((ROW_SLABS * _ZERO_ROWS, LANES), U32),
                            pltpu.SemaphoreType.DMA, pltpu.SemaphoreType.DMA],
        ),
        out_shape=jax.ShapeDtypeStruct((ROW_SLABS * n_rows, LANES), U32),
        compiler_params=pltpu.CompilerParams(dimension_semantics=("arbitrary",)),
        name="moe_dispatch",
    )(zstart, zlen, n_used, dest3, h2)


def _experts_kernel(be_ref, nu_ref, x_ref, wg_ref, wu_ref, wd_ref, y_ref, wgu_s, wd_s):
    b = pl.program_id(0)

    @pl.when((b == 0) | (be_ref[b] != be_ref[jnp.maximum(b - 1, 0)]))
    def _():
        wgu_s[:, :D_EXPERT] = wg_ref[0, 0].astype(BF16)
        wgu_s[:, D_EXPERT:] = wu_ref[0, 0].astype(BF16)
        wd_s[...] = wd_ref[0, 0].astype(BF16)

    @pl.when(b < nu_ref[0])
    def _():
        lo, hi = _unpack_rows(_load_rows(x_ref, MOE_ROWS))
        x = jnp.concatenate([lo.astype(BF16), hi.astype(BF16)], axis=1)
        au = _dot(x, wgu_s[...])
        a, u = au[:, :D_EXPERT], au[:, D_EXPERT:]
        hid = (a * jax.nn.sigmoid(a) * u).astype(BF16)
        _store_rows(y_ref, _pack_rows(_dot(hid, wd_s[...])))

    @pl.when(b >= nu_ref[0])
    def _():
        y_ref[...] = jnp.zeros_like(y_ref)


def _moe_experts(xb, blk_expert, n_used, w_gate, w_up, w_down, layer):
    n_rows = xb.shape[0] // ROW_SLABS
    n_blocks = n_rows // MOE_ROWS
    live = lambda b, be, nu: (jnp.minimum(b, nu[0] - 1), 0)
    expert = lambda b, be, nu: (layer, be[b], 0, 0)
    return pl.pallas_call(
        _experts_kernel,
        grid_spec=pltpu.PrefetchScalarGridSpec(
            num_scalar_prefetch=2,
            grid=(n_blocks,),
            in_specs=[
                pl.BlockSpec((ROW_SLABS * MOE_ROWS, LANES), live),
                pl.BlockSpec((1, 1, D_MODEL, D_EXPERT), expert),
                pl.BlockSpec((1, 1, D_MODEL, D_EXPERT), expert),
                pl.BlockSpec((1, 1, D_EXPERT, D_MODEL), expert),
            ],
            out_specs=pl.BlockSpec((ROW_SLABS * MOE_ROWS, LANES), lambda b, be, nu: (b, 0)),
            scratch_shapes=[pltpu.VMEM((D_MODEL, 2 * D_EXPERT), BF16),
                            pltpu.VMEM((D_EXPERT, D_MODEL), BF16)],
        ),
        out_shape=jax.ShapeDtypeStruct((ROW_SLABS * n_rows, LANES), U32),
        compiler_params=pltpu.CompilerParams(
            dimension_semantics=("arbitrary",), vmem_limit_bytes=VMEM_LIMIT),
        name="moe_experts",
    )(blk_expert, n_used, xb, w_gate, w_up, w_down)


def _combine_kernel(dest_ref, dnext_ref, x_ref, rf_ref, lnf_ref, yb_ref, *rest, tc, split):
    outs, (ybuf_ref, sems) = rest[:-2], rest[-2:]
    i = pl.program_id(0)
    n = pl.num_programs(0)
    slot = i % 2

    def row_copy(src_row, s, k, t):
        return pltpu.make_async_copy(yb_ref.at[_row_at(src_row)],
                                     ybuf_ref.at[s, k, _row_at(t)], sems.at[s])

    def issue(d_ref, s):
        def body(g, c):
            for u in range(ROWS_PER_ISSUE):
                t = g * ROWS_PER_ISSUE + u
                for k in range(TOP_K):
                    row_copy(d_ref[0, 0, 2 * t + k], s, k, t).start(
                        priority=(TOP_K * u + k) % DMA_QUEUES)
            return c

        lax.fori_loop(0, tc // ROWS_PER_ISSUE, body, 0)

    @pl.when(i == 0)
    def _():
        issue(dest_ref, slot)

    @pl.when(i + 1 < n)
    def _():
        issue(dnext_ref, 1 - slot)

    for k in range(TOP_K):
        pltpu.make_async_copy(yb_ref.at[pl.ds(0, tc * ROW_SLABS)], ybuf_ref.at[slot, k],
                              sems.at[slot]).wait()

    rf = rf_ref[...]
    lo0, hi0 = _unpack_rows(_load_rows(ybuf_ref.at[slot, 0], tc))
    lo1, hi1 = _unpack_rows(_load_rows(ybuf_ref.at[slot, 1], tc))
    g0, g1 = rf[:, 0:1], rf[:, 1:2]
    out = x_ref[...] + jnp.concatenate([g0 * lo0 + g1 * lo1, g0 * hi0 + g1 * hi1], axis=1)
    if split is None:
        outs[0][...] = out
    else:
        out = _rms(out, lnf_ref[...])

        @pl.when(i < split)
        def _():
            outs[0][...] = out

        @pl.when(i >= split)
        def _():
            outs[1][...] = out


def _moe_combine(x, rf, dest, yb, ln_final, tc, final_rows):
    t = x.shape[0]
    nt = t // tc
    dest3 = dest.reshape(nt, 1, 2 * tc)
    row = lambda i: (i, 0)
    if final_rows is None:
        split = None
        out_specs = pl.BlockSpec((tc, D_MODEL), row)
        out_shape = jax.ShapeDtypeStruct((t, D_MODEL), F32)
    else:
        split = final_rows[0] // tc
        out_specs = [pl.BlockSpec((tc, D_MODEL), lambda i: (jnp.minimum(i, split - 1), 0)),
                     pl.BlockSpec((tc, D_MODEL), lambda i: (jnp.maximum(i - split, 0), 0))]
        out_shape = [jax.ShapeDtypeStruct((r, D_MODEL), F32) for r in final_rows]
    return pl.pallas_call(
        functools.partial(_combine_kernel, tc=tc, split=split),
        grid=(nt,),
        in_specs=[
            pl.BlockSpec((1, 1, 2 * tc), lambda i: (i, 0, 0), memory_space=pltpu.SMEM),
            pl.BlockSpec((1, 1, 2 * tc), lambda i: (jnp.minimum(i + 1, nt - 1), 0, 0),
                         memory_space=pltpu.SMEM),
            pl.BlockSpec((tc, D_MODEL), row),
            pl.BlockSpec((tc, LANES), row),
            pl.BlockSpec((1, D_MODEL), lambda i: (0, 0)),
            pl.BlockSpec(memory_space=pl.ANY),
        ],
        out_specs=out_specs,
        out_shape=out_shape,
        scratch_shapes=[pltpu.VMEM((2, TOP_K, ROW_SLABS * tc, LANES), U32),
                        pltpu.SemaphoreType.DMA((2,))],
        compiler_params=pltpu.CompilerParams(
            dimension_semantics=("arbitrary",), vmem_limit_bytes=VMEM_LIMIT),
        name="moe_combine",
    )(dest3, dest3, x, rf, ln_final, yb)


def _rope_tables(seq):
    inv = ROPE_BASE ** (-jnp.arange(0, D_ROPE, 2, dtype=F32) / D_ROPE)
    ang = jnp.arange(seq, dtype=F32)[:, None] * inv[None, :]
    cos, sin = jnp.cos(ang), jnp.sin(ang)
    z = lambda w: jnp.zeros((seq, w), F32)
    tail = LANES - D_NOPE - D_ROPE
    return {
        "cos": jnp.concatenate([jnp.ones((seq, D_NOPE), F32), cos, cos, z(tail)], axis=1),
        "sin_a": jnp.concatenate([z(D_NOPE), -sin, z(HALF_ROPE), z(tail)], axis=1),
        "sin_b": jnp.concatenate([z(D_NOPE), z(HALF_ROPE), sin, z(tail)], axis=1),
    }


def _t5_bucket(rel):
    half = NUM_BUCKETS // 2
    max_exact = half // 2
    n = jnp.abs(rel)
    nf = jnp.maximum(n, 1).astype(F32)
    large = max_exact + (jnp.log(nf / max_exact) / math.log(MAX_DIST / max_exact)
                         * (half - max_exact)).astype(jnp.int32)
    large = jnp.minimum(large, half - 1)
    return jnp.where(rel > 0, half, 0) + jnp.where(n < max_exact, n, large)


def _window_bias(rel_bias):
    qi = jnp.arange(WB)[:, None]
    kj = jnp.arange(3 * WB)[None, :]
    rel = kj - WB - qi
    onehot = _t5_bucket(rel)[:, :, None] == jnp.arange(NUM_BUCKETS)
    bias = jnp.sum(jnp.where(onehot[..., None], rel_bias.astype(F32), 0.0), axis=2) * LOG2E
    bias = jnp.where((jnp.abs(rel) <= WINDOW)[:, :, None], bias, NEG)
    bias = jnp.transpose(bias, (2, 0, 1)).reshape(KVB, GQ * WB, 3 * WB)
    bias = jnp.swapaxes(bias, 1, 2)
    key = jnp.arange(3 * WB)[None, :, None]
    return jnp.stack([jnp.where(key < WB, NEG, bias), bias, jnp.where(key >= 2 * WB, NEG, bias)])


def _pack_layer(l, w_in, ln_mix, ln_cq, ln_ckv, w_uq, w_ukv, w_oa, sink, w_ob, w_out, ln_ffn,
                w_gr, b_gr, w_er, b_er):
    wi = w_in[l]
    o = 0
    c_q, o = wi[:, o:o + D_CQ], o + D_CQ
    c_kv, o = wi[:, o:o + D_C], o + D_C
    k_r, o = wi[:, o:o + D_ROPE], o + D_ROPE
    q_b, o = wi[:, o:o + HB * HD], o + HB * HD
    k_b, o = wi[:, o:o + KVB * HD], o + KVB * HD
    v_b, o = wi[:, o:o + KVB * HD], o + KVB * HD
    g_ab = wi[:, o:]
    zc = lambda rows, w: jnp.zeros((rows, w), F32)
    dup = lambda w: jnp.concatenate([w[:, kv * HD:(kv + 1) * HD] for kv in range(KVB) for _ in (0, 1)],
                                    axis=1)
    kr_blk = jnp.concatenate([zc(D_MODEL, D_NOPE), k_r, zc(D_MODEL, LANES - D_NOPE - D_ROPE)], axis=1)
    w_mix = jnp.concatenate([c_q, c_kv, kr_blk, q_b * (WIN_SCALE * LOG2E), dup(k_b)], axis=1)

    uq = w_uq[l].reshape(D_CQ, HA, D_NOPE + D_ROPE)
    uq = jnp.pad(uq, ((0, 0), (0, 0), (0, LANES - D_NOPE - D_ROPE))).reshape(D_CQ, HA * LANES)
    ukv = w_ukv[l].reshape(D_C, HA, D_NOPE + D_V)
    uk = jnp.pad(ukv[:, :, :D_NOPE], ((0, 0), (0, 0), (0, LANES - D_NOPE))).reshape(D_C, HA * LANES)
    uv = ukv[:, :, D_NOPE:].reshape(D_C, HA * D_V)

    w_r = jnp.concatenate([w_er[l], w_gr[l], zc(D_MODEL, LANES - N_EXPERTS - N_GROUPS)], axis=1)
    w_r_hi = w_r.astype(BF16)
    b_r = jnp.concatenate([b_er[l], b_gr[l], jnp.zeros((LANES - N_EXPERTS - N_GROUPS,), F32)])
    sink_row = jnp.repeat(sink[l].astype(F32).reshape(KVB, GQ) * LOG2E, WB, axis=1)[:, None, :]
    return {
        "ln_mix": ln_mix[l][None, :], "w_mix": w_mix.astype(BF16), "w_vbt": v_b.T.astype(BF16),
        "ln_cq": ln_cq[l][None, :], "ln_ckv": ln_ckv[l][None, :],
        "w_uq": uq.astype(BF16), "w_uk": uk.astype(BF16), "w_uvt": uv.T.astype(BF16),
        "w_g": g_ab.astype(BF16), "w_oa": w_oa[l].astype(BF16), "w_ob": w_ob[l].astype(BF16),
        "w_out": w_out[l].astype(BF16), "ln_ffn": ln_ffn[l][None, :],
        "w_r": jnp.concatenate([w_r_hi, (w_r - w_r_hi.astype(F32)).astype(BF16)], axis=1),
        "b_r": b_r[None, :],
        "sink": sink_row,
    }


def _row_tile(n, pref):
    tile = pref
    while n % tile:
        tile //= 2
    return tile


def _trunk(xs, batch, seq, tabs, bias, layers, expert_w, ln_final):
    final_rows = tuple(x.shape[0] for x in xs)
    t = sum(final_rows)
    tm = _row_tile(seq, 512)
    tq = _row_tile(seq, 512)
    tc = _row_tile(t, 256)
    n_blocks = (t * TOP_K) // MOE_ROWS + N_EXPERTS
    n_rows = n_blocks * MOE_ROWS
    for l, p in enumerate(layers):
        qa, ka, vat, qb, kb, vbt = _mixer_in(xs, p, tabs, seq, tm)
        oa = _mla_attention(qa, ka, vat, batch, seq, tq)
        ob = _window_attention(qb, kb, vbt, bias, p["sink"], batch, seq)
        x, h2, rf, ri, cnt = _mixer_out(xs, oa, ob, p, tm)
        counts = cnt[0, :N_EXPERTS].astype(jnp.int32)
        padded = ((counts + MOE_ROWS - 1) // MOE_ROWS) * MOE_ROWS
        pad_end = jnp.cumsum(padded)
        pad_start = pad_end - padded
        onehot = ri[:, 0:TOP_K, None] == jnp.arange(N_EXPERTS, dtype=jnp.int32)
        dest = (jnp.sum(jnp.where(onehot, pad_start, 0), axis=-1) + ri[:, TOP_K:2 * TOP_K]).reshape(-1)
        blk_row = jnp.arange(n_blocks, dtype=jnp.int32)[:, None] * MOE_ROWS
        blk_expert = jnp.minimum(jnp.sum((pad_end[None, :] <= blk_row).astype(jnp.int32), axis=1),
                                 N_EXPERTS - 1)
        n_used = (pad_end[-1:] // MOE_ROWS).astype(jnp.int32)
        xb = _moe_dispatch(h2, dest, (pad_start + counts).astype(jnp.int32),
                           (padded - counts).astype(jnp.int32), n_used, n_rows, tc)
        yb = _moe_experts(xb, blk_expert, n_used, *expert_w, l)
        last = l == len(layers) - 1
        x = _moe_combine(x, rf, dest, yb, ln_final, tc, final_rows if last else None)
        xs = (x, None)
    return x


def kernel(x_prompt, x_sample, rel_bias, ln_mix, w_in, ln_cq, ln_ckv, w_uq, w_ukv, w_oa, sink, w_ob,
           w_out, ln_ffn, w_gr, b_gr, w_er, b_er, w_gate, w_up, w_down, ln_final):
    bp, seq, d = x_prompt.shape
    bs = x_sample.shape[0]
    assert x_sample.shape[1] == seq and seq % WB == 0 and seq >= 3 * WB
    tabs = _rope_tables(seq)
    bias = _window_bias(rel_bias)
    layers = [_pack_layer(l, w_in, ln_mix, ln_cq, ln_ckv, w_uq, w_ukv, w_oa, sink, w_ob, w_out,
                          ln_ffn, w_gr, b_gr, w_er, b_er)
              for l in range(w_in.shape[0])]
    xs = (x_prompt.reshape(bp * seq, d), x_sample.reshape(bs * seq, d))
    y_prompt, y_sample = _trunk(xs, bp + bs, seq, tabs, bias, layers, (w_gate, w_up, w_down),
                                ln_final[None, :])
    return (y_prompt.reshape(bp, seq, d), y_sample.reshape(bs, seq, d))
```

```python
import functools
import math

import jax
import jax.numpy as jnp
from jax import lax
from jax.experimental import pallas as pl
from jax.experimental.pallas import tpu as pltpu

D_MODEL = 1024
DEPTH = 2
HA = 8
D_NOPE = 64
D_ROPE = 32
D_V = 64
D_CQ = 384
D_C = 256
ROPE_BASE = 10000.0
MLA_SCALE = (D_NOPE + D_ROPE) ** -0.5
HB = 8
KVB = 2
GQ = HB // KVB
HD = 64
WINDOW = 128
WB = 128
WIN_SCALE = HD ** -0.5
NUM_BUCKETS = 32
MAX_DIST = 128
N_GROUPS = 4
EPG = 8
N_EXPERTS = N_GROUPS * EPG
TOP_K = 2
D_EXPERT = 256
EPS = 1e-6
NEG = -1e30
LOG2E = math.log2(math.e)

LANES = 128
SUBLANES = 8
HALF_ROPE = D_ROPE // 2
MOE_ROWS = 512
MLA_HEADS = 4
MLA_KEYS = 256
DMA_QUEUES = 2
ROWS_PER_ISSUE = 4
VMEM_LIMIT = 48 * 1024 * 1024

C_CQ = 0
C_CKV = C_CQ + D_CQ
C_KR = C_CKV + D_C
C_QB = C_KR + LANES
C_KB = C_QB + HB * HD
N_MIX = C_KB + 2 * KVB * HD

F32 = jnp.float32
BF16 = jnp.bfloat16
U32 = jnp.uint32
D_PACK = D_MODEL // 2
ROW_SLABS = D_PACK // LANES


def _rms(x, g):
    ms = jnp.mean(x * x, axis=-1, keepdims=True)
    return x * lax.rsqrt(ms + EPS) * g


def _dot(a, b):
    return jnp.dot(a, b, preferred_element_type=F32)


def _dot_nt(a, b):
    return lax.dot_general(a, b, (((1,), (1,)), ((), ())), preferred_element_type=F32)


def _pack_rows(x):
    n = x.shape[1] // 2
    lo = lax.bitcast_convert_type(x[:, :n].astype(BF16).astype(F32), U32)
    hi = lax.bitcast_convert_type(x[:, n:].astype(BF16).astype(F32), U32)
    return hi | (lo >> 16)


def _unpack_rows(p):
    lo = lax.bitcast_convert_type(p << 16, F32)
    hi = lax.bitcast_convert_type(p & jnp.uint32(0xFFFF0000), F32)
    return lo, hi


def _load_rows(ref, n):
    return jnp.concatenate([ref[pl.ds(c, n, stride=ROW_SLABS), :] for c in range(ROW_SLABS)], axis=1)


def _store_rows(ref, rows):
    n = rows.shape[0]
    for c in range(ROW_SLABS):
        ref[pl.ds(c, n, stride=ROW_SLABS), :] = rows[:, c * LANES:(c + 1) * LANES]


def _row_at(i):
    return pl.ds(pl.multiple_of(i * ROW_SLABS, ROW_SLABS), ROW_SLABS)


def _x_operands(xs, tile):
    xa, xb = xs
    na = xa.shape[0] // tile
    rows = xa.shape[0] + (0 if xb is None else xb.shape[0])
    specs = (pl.BlockSpec((tile, D_MODEL), lambda i, *_: (jnp.minimum(i, na - 1), 0)),
             pl.BlockSpec((tile, D_MODEL), lambda i, *_: (jnp.maximum(i - na, 0), 0)))
    return (xa, xa if xb is None else xb), specs, na, rows


def _x_tile(xa_ref, xb_ref, na):
    return jnp.where(pl.program_id(0) < na, xa_ref[...], xb_ref[...])


def _rope(t, cos, sin_a, sin_b):
    return (t * cos + pltpu.roll(t, LANES - HALF_ROPE, 1) * sin_a
            + pltpu.roll(t, HALF_ROPE, 1) * sin_b)


def _mixer_in_kernel(xa_ref, xb_ref, lnmix_ref, win_ref, wvbt_ref, lncq_ref, lnckv_ref, wuq_ref,
                     wuk_ref, wuvt_ref, cos_ref, sina_ref, sinb_ref,
                     qa_ref, ka_ref, vat_ref, qb_ref, kb_ref, vbt_ref, *, na):
    h = _rms(_x_tile(xa_ref, xb_ref, na), lnmix_ref[...]).astype(BF16)
    cq = _dot(h, win_ref[:, C_CQ:C_CKV])
    ckv = _dot(h, win_ref[:, C_CKV:C_KR])
    kr = _dot(h, win_ref[:, C_KR:C_QB])
    qb_ref[...] = _dot(h, win_ref[:, C_QB:C_KB]).astype(BF16)
    kb_ref[...] = _dot(h, win_ref[:, C_KB:N_MIX]).astype(BF16)
    vbt = _dot_nt(wvbt_ref[...], h).astype(BF16)
    for c in range(vbt_ref.shape[0]):
        vbt_ref[c] = vbt[:, c * WB:(c + 1) * WB]
    cqn = _rms(cq, lncq_ref[...]).astype(BF16)
    ckvn = _rms(ckv, lnckv_ref[...]).astype(BF16)
    q = _dot(cqn, wuq_ref[...])
    k = _dot(ckvn, wuk_ref[...])
    vat_ref[...] = _dot_nt(wuvt_ref[...], ckvn).astype(BF16)
    cos, sin_a, sin_b = cos_ref[...], sina_ref[...], sinb_ref[...]
    kr_rot = _rope(kr, cos, sin_a, sin_b)
    for hd in range(HA):
        sl = slice(hd * LANES, (hd + 1) * LANES)
        qa_ref[:, sl] = (_rope(q[:, sl], cos, sin_a, sin_b) * (MLA_SCALE * LOG2E)).astype(BF16)
        ka_ref[:, sl] = (k[:, sl] + kr_rot).astype(BF16)


def _mixer_in(xs, p, tabs, seq, tm):
    x_ops, x_specs, na, t = _x_operands(xs, tm)
    nt = t // tm
    spt = seq // tm
    row = lambda i: (i, 0)
    full = lambda i: (0, 0)
    tab = lambda i: (i % spt, 0)
    rows = lambda w: (pl.BlockSpec((tm, w), row), jax.ShapeDtypeStruct((t, w), BF16))
    outs = [
        rows(HA * LANES),
        rows(HA * LANES),
        (pl.BlockSpec((HA * D_V, tm), lambda i: (0, i)),
         jax.ShapeDtypeStruct((HA * D_V, t), BF16)),
        rows(HB * HD),
        rows(2 * KVB * HD),
        (pl.BlockSpec((tm // WB, KVB * HD, WB), lambda i: (i, 0, 0)),
         jax.ShapeDtypeStruct((t // WB, KVB * HD, WB), BF16)),
    ]
    out_specs = [o[0] for o in outs]
    out_shape = [o[1] for o in outs]
    return pl.pallas_call(
        functools.partial(_mixer_in_kernel, na=na),
        grid=(nt,),
        in_specs=[
            *x_specs,
            pl.BlockSpec((1, D_MODEL), full),
            pl.BlockSpec((D_MODEL, N_MIX), full),
            pl.BlockSpec((KVB * HD, D_MODEL), full),
            pl.BlockSpec((1, D_CQ), full),
            pl.BlockSpec((1, D_C), full),
            pl.BlockSpec((D_CQ, HA * LANES), full),
            pl.BlockSpec((D_C, HA * LANES), full),
            pl.BlockSpec((HA * D_V, D_C), full),
            pl.BlockSpec((tm, LANES), tab),
            pl.BlockSpec((tm, LANES), tab),
            pl.BlockSpec((tm, LANES), tab),
        ],
        out_specs=out_specs,
        out_shape=out_shape,
        compiler_params=pltpu.CompilerParams(
            dimension_semantics=("parallel",), vmem_limit_bytes=VMEM_LIMIT),
        name="mixer_in",
    )(*x_ops, p["ln_mix"], p["w_mix"], p["w_vbt"], p["ln_cq"], p["ln_ckv"], p["w_uq"], p["w_uk"], p["w_uvt"],
      tabs["cos"], tabs["sin_a"], tabs["sin_b"])


def _mla_kernel(q_ref, k_ref, vt_ref, o_ref):
    heads = range(MLA_HEADS)
    nch = k_ref.shape[0] // MLA_KEYS

    def scores(j, hh):
        sl = slice(hh * LANES, (hh + 1) * LANES)
        return _dot_nt(k_ref[j * MLA_KEYS:(j + 1) * MLA_KEYS, sl], q_ref[:, sl])

    ones = jnp.ones((2 * SUBLANES, MLA_KEYS), BF16)

    def weighted_values(hh, j, p):
        vt = vt_ref[hh * D_V:(hh + 1) * D_V, j * MLA_KEYS:(j + 1) * MLA_KEYS]
        return _dot(jnp.concatenate([vt, ones], axis=0), p)

    s_next = [scores(0, hh) for hh in heads]
    m = [None] * MLA_HEADS
    acc = [None] * MLA_HEADS
    pending = [None] * MLA_HEADS
    for j in range(nch):
        s_cur = s_next
        if j + 1 < nch:
            s_next = [scores(j + 1, hh) for hh in heads]
        for hh in heads:
            if pending[hh] is not None:
                p_prev, alpha_prev = pending[hh]
                pv = weighted_values(hh, j - 1, p_prev)
                acc[hh] = pv if acc[hh] is None else acc[hh] * alpha_prev + pv
            s = s_cur[hh]
            cmax = jnp.max(s, axis=0, keepdims=True)
            m_new = cmax if m[hh] is None else jnp.maximum(m[hh], cmax)
            alpha = None if m[hh] is None else jnp.exp2(m[hh] - m_new)
            p = jnp.exp2(s - m_new)
            m[hh] = m_new
            pending[hh] = (p.astype(BF16), alpha)
    outs = []
    for hh in heads:
        p_prev, alpha_prev = pending[hh]
        pv = weighted_values(hh, nch - 1, p_prev)
        total = pv if acc[hh] is None else acc[hh] * alpha_prev + pv
        outs.append(total[:D_V] / total[D_V:D_V + 1])
    o_ref[...] = jnp.concatenate(outs, axis=0).T.astype(BF16)


def _mla_attention(qa, ka, va, batch, seq, tq):
    t = qa.shape[0]
    nq = seq // tq
    return pl.pallas_call(
        _mla_kernel,
        grid=(batch, HA // MLA_HEADS, nq),
        in_specs=[
            pl.BlockSpec((tq, MLA_HEADS * LANES), lambda b, j, i: (b * nq + i, j)),
            pl.BlockSpec((seq, MLA_HEADS * LANES), lambda b, j, i: (b, j)),
            pl.BlockSpec((MLA_HEADS * D_V, seq), lambda b, j, i: (j, b)),
        ],
        out_specs=pl.BlockSpec((tq, MLA_HEADS * D_V), lambda b, j, i: (b * nq + i, j)),
        out_shape=jax.ShapeDtypeStruct((t, HA * D_V), BF16),
        compiler_params=pltpu.CompilerParams(
            dimension_semantics=("parallel", "parallel", "parallel"),
            vmem_limit_bytes=VMEM_LIMIT),
        name="mla_attention",
    )(qa, ka, va)


def _window_kernel(q_ref, k_ref, vt_ref, bias_ref, sink_ref, o_ref, *, seq, qblocks):
    nb = seq // WB
    lane = lax.broadcasted_iota(jnp.int32, (WB, LANES), 1)
    lo = lane < HD
    for u in range(qblocks):
        n = pl.program_id(1) * qblocks + u
        rows = slice(u * WB, (u + 1) * WB)
        blocks = (jnp.maximum(n - 1, 0), n, jnp.minimum(n + 1, nb - 1))
        kw = jnp.concatenate([k_ref[pl.ds(pl.multiple_of(b * WB, WB), WB), :] for b in blocks],
                             axis=0)
        vts = [vt_ref[b] for b in blocks]
        edge = jnp.where(n == 0, 0, jnp.where(n == nb - 1, 2, 1))
        heads = []
        for kv in range(KVB):
            kd = kw[:, kv * LANES:(kv + 1) * LANES]
            qs = []
            for jj in range(GQ // 2):
                qp = q_ref[rows, (kv * 2 + jj) * LANES:(kv * 2 + jj + 1) * LANES]
                qs.append(jnp.where(lo, qp, jnp.zeros_like(qp)))
                qs.append(jnp.where(lo, jnp.zeros_like(qp), qp))
            q4 = jnp.concatenate(qs, axis=0)
            st = _dot_nt(kd, q4) + bias_ref[edge, kv]
            sink = sink_ref[kv]
            m = jnp.maximum(jnp.max(st, axis=0, keepdims=True), sink)
            e = jnp.exp2(st - m)
            denom = jnp.sum(e, axis=0, keepdims=True) + jnp.exp2(sink - m)
            vt = jnp.concatenate([v[kv * HD:(kv + 1) * HD, :] for v in vts], axis=1)
            ot = _dot(vt, e.astype(BF16)) / denom
            heads += [ot[:, g * WB:(g + 1) * WB] for g in range(GQ)]
        o_ref[rows, :] = jnp.concatenate(heads, axis=0).T.astype(BF16)


def _window_attention(qb, kb, vbt, bias, sink, batch, seq):
    t = qb.shape[0]
    nb = seq // WB
    qblocks = 4 if nb % 4 == 0 else 1
    nb //= qblocks
    return pl.pallas_call(
        functools.partial(_window_kernel, seq=seq, qblocks=qblocks),
        grid=(batch, nb),
        in_specs=[
            pl.BlockSpec((qblocks * WB, HB * HD), lambda b, n: (b * nb + n, 0)),
            pl.BlockSpec((seq, 2 * KVB * HD), lambda b, n: (b, 0)),
            pl.BlockSpec((seq // WB, KVB * HD, WB), lambda b, n: (b, 0, 0)),
            pl.BlockSpec((3, KVB, 3 * WB, GQ * WB), lambda b, n: (0, 0, 0, 0)),
            pl.BlockSpec((KVB, 1, GQ * WB), lambda b, n: (0, 0, 0)),
        ],
        out_specs=pl.BlockSpec((qblocks * WB, HB * HD), lambda b, n: (b * nb + n, 0)),
        out_shape=jax.ShapeDtypeStruct((t, HB * HD), BF16),
        compiler_params=pltpu.CompilerParams(
            dimension_semantics=("parallel", "parallel"), vmem_limit_bytes=VMEM_LIMIT),
        name="window_attention",
    )(qb, kb, vbt, bias, sink)


def _mixer_out_kernel(xa_ref, xb_ref, oa_ref, ob_ref, lnmix_ref, wg_ref, woa_ref, wob_ref, wout_ref,
                      lnffn_ref, wrh_ref, br_ref,
                      xn_ref, h2_ref, rf_ref, ri_ref, cnt_ref, carry_ref, *, na):
    i = pl.program_id(0)

    @pl.when(i == 0)
    def _():
        carry_ref[...] = jnp.zeros_like(carry_ref)

    x = _x_tile(xa_ref, xb_ref, na)
    tm = x.shape[0]
    h = _rms(x, lnmix_ref[...]).astype(BF16)
    g = _dot(h, wg_ref[...])
    ya = _dot(oa_ref[...], woa_ref[...])
    yb = _dot(ob_ref[...], wob_ref[...])
    merged = jax.nn.sigmoid(g[:, :D_MODEL]) * ya + jax.nn.sigmoid(g[:, D_MODEL:]) * yb
    xn = x + _dot(merged.astype(BF16), wout_ref[...])
    xn_ref[...] = xn
    h2 = _rms(xn, lnffn_ref[...])
    _store_rows(h2_ref, _pack_rows(h2))

    h2_hi = h2.astype(BF16)
    h2_lo = (h2 - h2_hi.astype(F32)).astype(BF16)
    hi_terms = _dot(h2_hi, wrh_ref[...])
    logits = (hi_terms[:, :LANES] + hi_terms[:, LANES:] + _dot(h2_lo, wrh_ref[:, :LANES])
              + br_ref[...])

    lane = lax.broadcasted_iota(jnp.int32, (tm, LANES), 1)
    lanef = lane.astype(F32)
    big = float(LANES)
    is_g = (lane >= N_EXPERTS) & (lane < N_EXPERTS + N_GROUPS)
    gl = jnp.where(is_g, logits, NEG)
    gmax = jnp.max(gl, axis=1, keepdims=True)
    grp = jnp.min(jnp.where(gl == gmax, lanef, big), axis=1, keepdims=True) - N_EXPERTS
    gsum = jnp.sum(jnp.where(is_g, jnp.exp(gl - gmax), 0.0), axis=1, keepdims=True)
    grp_w = 1.0 / gsum
    is_e = (lanef >= grp * EPG) & (lanef < grp * EPG + EPG)
    el = jnp.where(is_e, logits, NEG)
    v1 = jnp.max(el, axis=1, keepdims=True)
    i1 = jnp.min(jnp.where((el == v1) & is_e, lanef, big), axis=1, keepdims=True)
    a1 = lanef == i1
    el2 = jnp.where(a1, NEG, el)
    v2 = jnp.max(el2, axis=1, keepdims=True)
    i2 = jnp.min(jnp.where((el2 == v2) & is_e & (~a1), lanef, big), axis=1, keepdims=True)
    a2 = lanef == i2
    ex = jnp.exp(v2 - v1)
    p1 = 1.0 / (1.0 + ex)
    gate1 = grp_w * p1
    gate2 = grp_w * (ex * p1)

    a1f = a1.astype(F32)
    a2f = a2.astype(F32)
    cnt = a1f + a2f
    r_i = lax.broadcasted_iota(jnp.int32, (tm, tm), 0)
    c_i = lax.broadcasted_iota(jnp.int32, (tm, tm), 1)
    tri = jnp.where(r_i > c_i, 1.0, 0.0).astype(BF16)
    before = _dot(tri, cnt.astype(BF16)) + carry_ref[...]
    rank1 = jnp.sum(a1f * before, axis=1, keepdims=True)
    rank2 = jnp.sum(a2f * before, axis=1, keepdims=True)
    carry = carry_ref[...] + jnp.sum(cnt, axis=0, keepdims=True)
    carry_ref[...] = carry
    cnt_ref[...] = jnp.broadcast_to(carry, cnt_ref.shape)

    rf_ref[...] = jnp.where(lane == 0, gate1, jnp.where(lane == 1, gate2, 0.0))
    ri = jnp.where(lane == 0, i1, jnp.where(lane == 1, i2,
                                            jnp.where(lane == 2, rank1,
                                                      jnp.where(lane == 3, rank2, 0.0))))
    ri_ref[...] = ri.astype(jnp.int32)


def _mixer_out(xs, oa, ob, p, tm):
    x_ops, x_specs, na, t = _x_operands(xs, tm)
    nt = t // tm
    row = lambda i: (i, 0)
    full = lambda i: (0, 0)
    return pl.pallas_call(
        functools.partial(_mixer_out_kernel, na=na),
        grid=(nt,),
        in_specs=[
            *x_specs,
            pl.BlockSpec((tm, HA * D_V), row),
            pl.BlockSpec((tm, HB * HD), row),
            pl.BlockSpec((1, D_MODEL), full),
            pl.BlockSpec((D_MODEL, 2 * D_MODEL), full),
            pl.BlockSpec((HA * D_V, D_MODEL), full),
            pl.BlockSpec((HB * HD, D_MODEL), full),
            pl.BlockSpec((D_MODEL, D_MODEL), full),
            pl.BlockSpec((1, D_MODEL), full),
            pl.BlockSpec((D_MODEL, 2 * LANES), full),
            pl.BlockSpec((1, LANES), full),
        ],
        out_specs=[
            pl.BlockSpec((tm, D_MODEL), row),
            pl.BlockSpec((ROW_SLABS * tm, LANES), row),
            pl.BlockSpec((tm, LANES), row),
            pl.BlockSpec((tm, LANES), row),
            pl.BlockSpec((8, LANES), full),
        ],
        out_shape=[
            jax.ShapeDtypeStruct((t, D_MODEL), F32),
            jax.ShapeDtypeStruct((ROW_SLABS * t, LANES), U32),
            jax.ShapeDtypeStruct((t, LANES), F32),
            jax.ShapeDtypeStruct((t, LANES), jnp.int32),
            jax.ShapeDtypeStruct((8, LANES), F32),
        ],
        scratch_shapes=[pltpu.VMEM((1, LANES), F32)],
        compiler_params=pltpu.CompilerParams(
            dimension_semantics=("arbitrary",), vmem_limit_bytes=VMEM_LIMIT),
        name="mixer_out",
    )(*x_ops, oa, ob, p["ln_mix"], p["w_g"], p["w_oa"], p["w_ob"], p["w_out"], p["ln_ffn"],
      p["w_r"], p["b_r"])


_ZERO_ROWS = MOE_ROWS // 2
_ZERO_SIZES = tuple(_ZERO_ROWS >> s for s in range((_ZERO_ROWS // SUBLANES).bit_length()))


def _dispatch_kernel(zstart_ref, zlen_ref, nu_ref, dest_ref, h2_ref, xb_ref, zbuf_ref, sem, zsem, *, td):
    i = pl.program_id(0)

    def row_copy(t, d):
        return pltpu.make_async_copy(h2_ref.at[_row_at(t)], xb_ref.at[_row_at(d)], sem)

    def issue(g, c):
        for u in range(ROWS_PER_ISSUE):
            t = g * ROWS_PER_ISSUE + u
            for k in range(TOP_K):
                row_copy(t, dest_ref[0, 0, 2 * t + k]).start(
                    priority=(TOP_K * u + k) % DMA_QUEUES)
        return c

    lax.fori_loop(0, td // ROWS_PER_ISSUE, issue, 0)

    def zero_copy(start, size):
        first = pl.multiple_of(start * ROW_SLABS, ROW_SLABS if size == 1 else SUBLANES * ROW_SLABS)
        return pltpu.make_async_copy(zbuf_ref.at[pl.ds(0, size * ROW_SLABS)],
                                     xb_ref.at[pl.ds(first, size * ROW_SLABS)], zsem)

    def zero_pass(do):
        def per_expert(e, c):
            start = zstart_ref[e]
            head = (-start) & (SUBLANES - 1)
            for j in range(SUBLANES - 1):
                @pl.when(j < head)
                def _(row=start + j):
                    do(zero_copy(row, 1))

            start = start + head
            left = zlen_ref[e] - head
            for size in _ZERO_SIZES:
                take = (left & size) != 0

                @pl.when(take)
                def _(start=start, size=size):
                    do(zero_copy(pl.multiple_of(start, SUBLANES), size))

                start = start + jnp.where(take, size, 0)
            return c

        lax.fori_loop(0, N_EXPERTS, per_expert, 0)

        def unused_block(b, c):
            for part in range(MOE_ROWS // _ZERO_ROWS):
                row = pl.multiple_of(b * MOE_ROWS + part * _ZERO_ROWS, _ZERO_ROWS)
                do(zero_copy(row, _ZERO_ROWS))
            return c

        lax.fori_loop(nu_ref[0], xb_ref.shape[0] // (ROW_SLABS * MOE_ROWS), unused_block, 0)

    @pl.when(i == 0)
    def _():
        zbuf_ref[...] = jnp.zeros_like(zbuf_ref)
        zero_pass(lambda cp: cp.start())
        zero_pass(lambda cp: cp.wait())

    for _ in range(TOP_K):
        pltpu.make_async_copy(h2_ref, xb_ref.at[pl.ds(0, td * ROW_SLABS)], sem).wait()


def _moe_dispatch(h2, dest, zstart, zlen, n_used, n_rows, td):
    t = h2.shape[0] // ROW_SLABS
    nt = t // td
    dest3 = dest.reshape(nt, 1, 2 * td)
    return pl.pallas_call(
        functools.partial(_dispatch_kernel, td=td),
        grid_spec=pltpu.PrefetchScalarGridSpec(
            num_scalar_prefetch=3,
            grid=(nt,),
            in_specs=[
                pl.BlockSpec((1, 1, 2 * td), lambda i, zs, zl, nu: (i, 0, 0),
                             memory_space=pltpu.SMEM),
                pl.BlockSpec((ROW_SLABS * td, LANES), lambda i, zs, zl, nu: (i, 0)),
            ],
            out_specs=pl.BlockSpec(memory_space=pl.ANY),
            scratch_shapes=[pltpu.VMEM((ROW_SLABS * _ZERO_ROWS, LANES), U32),
                            pltpu.SemaphoreType.DMA, pltpu.SemaphoreType.DMA],
        ),
        out_shape=jax.ShapeDtypeStruct((ROW_SLABS * n_rows, LANES), U32),
        compiler_params=pltpu.CompilerParams(dimension_semantics=("arbitrary",)),
        name="moe_dispatch",
    )(zstart, zlen, n_used, dest3, h2)


def _experts_kernel(be_ref, nu_ref, x_ref, wg_ref, wu_ref, wd_ref, y_ref, wgu_s, wd_s):
    b = pl.program_id(0)

    @pl.when((b == 0) | (be_ref[b] != be_ref[jnp.maximum(b - 1, 0)]))
    def _():
        wgu_s[:, :D_EXPERT] = wg_ref[0, 0].astype(BF16)
        wgu_s[:, D_EXPERT:] = wu_ref[0, 0].astype(BF16)
        wd_s[...] = wd_ref[0, 0].astype(BF16)

    @pl.when(b < nu_ref[0])
    def _():
        lo, hi = _unpack_rows(_load_rows(x_ref, MOE_ROWS))
        x = jnp.concatenate([lo.astype(BF16), hi.astype(BF16)], axis=1)
        au = _dot(x, wgu_s[...])
        a, u = au[:, :D_EXPERT], au[:, D_EXPERT:]
        hid = (a * jax.nn.sigmoid(a) * u).astype(BF16)
        _store_rows(y_ref, _pack_rows(_dot(hid, wd_s[...])))

    @pl.when(b >= nu_ref[0])
    def _():
        y_ref[...] = jnp.zeros_like(y_ref)


def _moe_experts(xb, blk_expert, n_used, w_gate, w_up, w_down, layer):
    n_rows = xb.shape[0] // ROW_SLABS
    n_blocks = n_rows // MOE_ROWS
    live = lambda b, be, nu: (jnp.minimum(b, nu[0] - 1), 0)
    expert = lambda b, be, nu: (layer, be[b], 0, 0)
    return pl.pallas_call(
        _experts_kernel,
        grid_spec=pltpu.PrefetchScalarGridSpec(
            num_scalar_prefetch=2,
            grid=(n_blocks,),
            in_specs=[
                pl.BlockSpec((ROW_SLABS * MOE_ROWS, LANES), live),
                pl.BlockSpec((1, 1, D_MODEL, D_EXPERT), expert),
                pl.BlockSpec((1, 1, D_MODEL, D_EXPERT), expert),
                pl.BlockSpec((1, 1, D_EXPERT, D_MODEL), expert),
            ],
            out_specs=pl.BlockSpec((ROW_SLABS * MOE_ROWS, LANES), lambda b, be, nu: (b, 0)),
            scratch_shapes=[pltpu.VMEM((D_MODEL, 2 * D_EXPERT), BF16),
                            pltpu.VMEM((D_EXPERT, D_MODEL), BF16)],
        ),
        out_shape=jax.ShapeDtypeStruct((ROW_SLABS * n_rows, LANES), U32),
        compiler_params=pltpu.CompilerParams(
            dimension_semantics=("arbitrary",), vmem_limit_bytes=VMEM_LIMIT),
        name="moe_experts",
    )(blk_expert, n_used, xb, w_gate, w_up, w_down)


def _combine_kernel(dest_ref, dnext_ref, x_ref, rf_ref, lnf_ref, yb_ref, *rest, tc, split):
    outs, (ybuf_ref, sems) = rest[:-2], rest[-2:]
    i = pl.program_id(0)
    n = pl.num_programs(0)
    slot = i % 2

    def row_copy(src_row, s, k, t):
        return pltpu.make_async_copy(yb_ref.at[_row_at(src_row)],
                                     ybuf_ref.at[s, k, _row_at(t)], sems.at[s])

    def issue(d_ref, s):
        def body(g, c):
            for u in range(ROWS_PER_ISSUE):
                t = g * ROWS_PER_ISSUE + u
                for k in range(TOP_K):
                    row_copy(d_ref[0, 0, 2 * t + k], s, k, t).start(
                        priority=(TOP_K * u + k) % DMA_QUEUES)
            return c

        lax.fori_loop(0, tc // ROWS_PER_ISSUE, body, 0)

    @pl.when(i == 0)
    def _():
        issue(dest_ref, slot)

    @pl.when(i + 1 < n)
    def _():
        issue(dnext_ref, 1 - slot)

    for k in range(TOP_K):
        pltpu.make_async_copy(yb_ref.at[pl.ds(0, tc * ROW_SLABS)], ybuf_ref.at[slot, k],
                              sems.at[slot]).wait()

    rf = rf_ref[...]
    lo0, hi0 = _unpack_rows(_load_rows(ybuf_ref.at[slot, 0], tc))
    lo1, hi1 = _unpack_rows(_load_rows(ybuf_ref.at[slot, 1], tc))
    g0, g1 = rf[:, 0:1], rf[:, 1:2]
    out = x_ref[...] + jnp.concatenate([g0 * lo0 + g1 * lo1, g0 * hi0 + g1 * hi1], axis=1)
    if split is None:
        outs[0][...] = out
    else:
        out = _rms(out, lnf_ref[...])

        @pl.when(i < split)
        def _():
            outs[0][...] = out

        @pl.when(i >= split)
        def _():
            outs[1][...] = out


def _moe_combine(x, rf, dest, yb, ln_final, tc, final_rows):
    t = x.shape[0]
    nt = t // tc
    dest3 = dest.reshape(nt, 1, 2 * tc)
    row = lambda i: (i, 0)
    if final_rows is None:
        split = None
        out_specs = pl.BlockSpec((tc, D_MODEL), row)
        out_shape = jax.ShapeDtypeStruct((t, D_MODEL), F32)
    else:
        split = final_rows[0] // tc
        out_specs = [pl.BlockSpec((tc, D_MODEL), lambda i: (jnp.minimum(i, split - 1), 0)),
                     pl.BlockSpec((tc, D_MODEL), lambda i: (jnp.maximum(i - split, 0), 0))]
        out_shape = [jax.ShapeDtypeStruct((r, D_MODEL), F32) for r in final_rows]
    return pl.pallas_call(
        functools.partial(_combine_kernel, tc=tc, split=split),
        grid=(nt,),
        in_specs=[
            pl.BlockSpec((1, 1, 2 * tc), lambda i: (i, 0, 0), memory_space=pltpu.SMEM),
            pl.BlockSpec((1, 1, 2 * tc), lambda i: (jnp.minimum(i + 1, nt - 1), 0, 0),
                         memory_space=pltpu.SMEM),
            pl.BlockSpec((tc, D_MODEL), row),
            pl.BlockSpec((tc, LANES), row),
            pl.BlockSpec((1, D_MODEL), lambda i: (0, 0)),
            pl.BlockSpec(memory_space=pl.ANY),
        ],
        out_specs=out_specs,
        out_shape=out_shape,
        scratch_shapes=[pltpu.VMEM((2, TOP_K, ROW_SLABS * tc, LANES), U32),
                        pltpu.SemaphoreType.DMA((2,))],
        compiler_params=pltpu.CompilerParams(
            dimension_semantics=("arbitrary",), vmem_limit_bytes=VMEM_LIMIT),
        name="moe_combine",
    )(dest3, dest3, x, rf, ln_final, yb)


def _rope_tables(seq):
    inv = ROPE_BASE ** (-jnp.arange(0, D_ROPE, 2, dtype=F32) / D_ROPE)
    ang = jnp.arange(seq, dtype=F32)[:, None] * inv[None, :]
    cos, sin = jnp.cos(ang), jnp.sin(ang)
    z = lambda w: jnp.zeros((seq, w), F32)
    tail = LANES - D_NOPE - D_ROPE
    return {
        "cos": jnp.concatenate([jnp.ones((seq, D_NOPE), F32), cos, cos, z(tail)], axis=1),
        "sin_a": jnp.concatenate([z(D_NOPE), -sin, z(HALF_ROPE), z(tail)], axis=1),
        "sin_b": jnp.concatenate([z(D_NOPE), z(HALF_ROPE), sin, z(tail)], axis=1),
    }


def _t5_bucket(rel):
    half = NUM_BUCKETS // 2
    max_exact = half // 2
    n = jnp.abs(rel)
    nf = jnp.maximum(n, 1).astype(F32)
    large = max_exact + (jnp.log(nf / max_exact) / math.log(MAX_DIST / max_exact)
                         * (half - max_exact)).astype(jnp.int32)
    large = jnp.minimum(large, half - 1)
    return jnp.where(rel > 0, half, 0) + jnp.where(n < max_exact, n, large)


def _window_bias(rel_bias):
    qi = jnp.arange(WB)[:, None]
    kj = jnp.arange(3 * WB)[None, :]
    rel = kj - WB - qi
    onehot = _t5_bucket(rel)[:, :, None] == jnp.arange(NUM_BUCKETS)
    bias = jnp.sum(jnp.where(onehot[..., None], rel_bias.astype(F32), 0.0), axis=2) * LOG2E
    bias = jnp.where((jnp.abs(rel) <= WINDOW)[:, :, None], bias, NEG)
    bias = jnp.transpose(bias, (2, 0, 1)).reshape(KVB, GQ * WB, 3 * WB)
    bias = jnp.swapaxes(bias, 1, 2)
    key = jnp.arange(3 * WB)[None, :, None]
    return jnp.stack([jnp.where(key < WB, NEG, bias), bias, jnp.where(key >= 2 * WB, NEG, bias)])


def _pack_layer(l, w_in, ln_mix, ln_cq, ln_ckv, w_uq, w_ukv, w_oa, sink, w_ob, w_out, ln_ffn,
                w_gr, b_gr, w_er, b_er):
    wi = w_in[l]
    o = 0
    c_q, o = wi[:, o:o + D_CQ], o + D_CQ
    c_kv, o = wi[:, o:o + D_C], o + D_C
    k_r, o = wi[:, o:o + D_ROPE], o + D_ROPE
    q_b, o = wi[:, o:o + HB * HD], o + HB * HD
    k_b, o = wi[:, o:o + KVB * HD], o + KVB * HD
    v_b, o = wi[:, o:o + KVB * HD], o + KVB * HD
    g_ab = wi[:, o:]
    zc = lambda rows, w: jnp.zeros((rows, w), F32)
    dup = lambda w: jnp.concatenate([w[:, kv * HD:(kv + 1) * HD] for kv in range(KVB) for _ in (0, 1)],
                                    axis=1)
    kr_blk = jnp.concatenate([zc(D_MODEL, D_NOPE), k_r, zc(D_MODEL, LANES - D_NOPE - D_ROPE)], axis=1)
    w_mix = jnp.concatenate([c_q, c_kv, kr_blk, q_b * (WIN_SCALE * LOG2E), dup(k_b)], axis=1)

    uq = w_uq[l].reshape(D_CQ, HA, D_NOPE + D_ROPE)
    uq = jnp.pad(uq, ((0, 0), (0, 0), (0, LANES - D_NOPE - D_ROPE))).reshape(D_CQ, HA * LANES)
    ukv = w_ukv[l].reshape(D_C, HA, D_NOPE + D_V)
    uk = jnp.pad(ukv[:, :, :D_NOPE], ((0, 0), (0, 0), (0, LANES - D_NOPE))).reshape(D_C, HA * LANES)
    uv = ukv[:, :, D_NOPE:].reshape(D_C, HA * D_V)

    w_r = jnp.concatenate([w_er[l], w_gr[l], zc(D_MODEL, LANES - N_EXPERTS - N_GROUPS)], axis=1)
    w_r_hi = w_r.astype(BF16)
    b_r = jnp.concatenate([b_er[l], b_gr[l], jnp.zeros((LANES - N_EXPERTS - N_GROUPS,), F32)])
    sink_row = jnp.repeat(sink[l].astype(F32).reshape(KVB, GQ) * LOG2E, WB, axis=1)[:, None, :]
    return {
        "ln_mix": ln_mix[l][None, :], "w_mix": w_mix.astype(BF16), "w_vbt": v_b.T.astype(BF16),
        "ln_cq": ln_cq[l][None, :], "ln_ckv": ln_ckv[l][None, :],
        "w_uq": uq.astype(BF16), "w_uk": uk.astype(BF16), "w_uvt": uv.T.astype(BF16),
        "w_g": g_ab.astype(BF16), "w_oa": w_oa[l].astype(BF16), "w_ob": w_ob[l].astype(BF16),
        "w_out": w_out[l].astype(BF16), "ln_ffn": ln_ffn[l][None, :],
        "w_r": jnp.concatenate([w_r_hi, (w_r - w_r_hi.astype(F32)).astype(BF16)], axis=1),
        "b_r": b_r[None, :],
        "sink": sink_row,
    }


def _row_tile(n, pref):
    tile = pref
    while n % tile:
        tile //= 2
    return tile


def _trunk(xs, batch, seq, tabs, bias, layers, expert_w, ln_final):
    final_rows = tuple(x.shape[0] for x in xs)
    t = sum(final_rows)
    tm = _row_tile(seq, 512)
    tq = _row_tile(seq, 512)
    tc = _row_tile(t, 256)
    n_blocks = (t * TOP_K) // MOE_ROWS + N_EXPERTS
    n_rows = n_blocks * MOE_ROWS
    for l, p in enumerate(layers):
        qa, ka, vat, qb, kb, vbt = _mixer_in(xs, p, tabs, seq, tm)
        oa = _mla_attention(qa, ka, vat, batch, seq, tq)
        ob = _window_attention(qb, kb, vbt, bias, p["sink"], batch, seq)
        x, h2, rf, ri, cnt = _mixer_out(xs, oa, ob, p, tm)
        counts = cnt[0, :N_EXPERTS].astype(jnp.int32)
        padded = ((counts + MOE_ROWS - 1) // MOE_ROWS) * MOE_ROWS
        pad_end = jnp.cumsum(padded)
        pad_start = pad_end - padded
        onehot = ri[:, 0:TOP_K, None] == jnp.arange(N_EXPERTS, dtype=jnp.int32)
        dest = (jnp.sum(jnp.where(onehot, pad_start, 0), axis=-1) + ri[:, TOP_K:2 * TOP_K]).reshape(-1)
        blk_row = jnp.arange(n_blocks, dtype=jnp.int32)[:, None] * MOE_ROWS
        blk_expert = jnp.minimum(jnp.sum((pad_end[None, :] <= blk_row).astype(jnp.int32), axis=1),
                                 N_EXPERTS - 1)
        n_used = (pad_end[-1:] // MOE_ROWS).astype(jnp.int32)
        xb = _moe_dispatch(h2, dest, (pad_start + counts).astype(jnp.int32),
                           (padded - counts).astype(jnp.int32), n_used, n_rows, tc)
        yb = _moe_experts(xb, blk_expert, n_used, *expert_w, l)
        last = l == len(layers) - 1
        x = _moe_combine(x, rf, dest, yb, ln_final, tc, final_rows if last else None)
        xs = (x, None)
    return x


def kernel(x_prompt, x_sample, rel_bias, ln_mix, w_in, ln_cq, ln_ckv, w_uq, w_ukv, w_oa, sink, w_ob,
           w_out, ln_ffn, w_gr, b_gr, w_er, b_er, w_gate, w_up, w_down, ln_final):
    bp, seq, d = x_prompt.shape
    bs = x_sample.shape[0]
    assert x_sample.shape[1] == seq and seq % WB == 0 and seq >= 3 * WB
    tabs = _rope_tables(seq)
    bias = _window_bias(rel_bias)
    layers = [_pack_layer(l, w_in, ln_mix, ln_cq, ln_ckv, w_uq, w_ukv, w_oa, sink, w_ob, w_out,
                          ln_ffn, w_gr, b_gr, w_er, b_er)
              for l in range(w_in.shape[0])]
    xs = (x_prompt.reshape(bp * seq, d), x_sample.reshape(bs * seq, d))
    y_prompt, y_sample = _trunk(xs, bp + bs, seq, tabs, bias, layers, (w_gate, w_up, w_down),
                                ln_final[None, :])
    return (y_prompt.reshape(bp, seq, d), y_sample.reshape(bs, seq, d))
```

```python
import functools
import math

import jax
import jax.numpy as jnp
from jax import lax
from jax.experimental import pallas as pl
from jax.experimental.pallas import tpu as pltpu

D_MODEL = 1024
DEPTH = 2
HA = 8
D_NOPE = 64
D_ROPE = 32
D_V = 64
D_CQ = 384
D_C = 256
ROPE_BASE = 10000.0
MLA_SCALE = (D_NOPE + D_ROPE) ** -0.5
HB = 8
KVB = 2
GQ = HB // KVB
HD = 64
WINDOW = 128
WB = 128
WIN_SCALE = HD ** -0.5
NUM_BUCKETS = 32
MAX_DIST = 128
N_GROUPS = 4
EPG = 8
N_EXPERTS = N_GROUPS * EPG
TOP_K = 2
D_EXPERT = 256
EPS = 1e-6
NEG = -1e30
LOG2E = math.log2(math.e)

LANES = 128
SUBLANES = 8
HALF_ROPE = D_ROPE // 2
MOE_ROWS = 512
MLA_HEADS = 4
MLA_KEYS = 256
DMA_QUEUES = 2
ROWS_PER_ISSUE = 4
VMEM_LIMIT = 48 * 1024 * 1024

C_CQ = 0
C_CKV = C_CQ + D_CQ
C_KR = C_CKV + D_C
C_QB = C_KR + LANES
C_KB = C_QB + HB * HD
N_MIX = C_KB + 2 * KVB * HD

F32 = jnp.float32
BF16 = jnp.bfloat16
U32 = jnp.uint32
D_PACK = D_MODEL // 2
ROW_SLABS = D_PACK // LANES


def _rms(x, g):
    ms = jnp.mean(x * x, axis=-1, keepdims=True)
    return x * lax.rsqrt(ms + EPS) * g


def _dot(a, b):
    return jnp.dot(a, b, preferred_element_type=F32)


def _dot_nt(a, b):
    return lax.dot_general(a, b, (((1,), (1,)), ((), ())), preferred_element_type=F32)


def _pack_rows(x):
    n = x.shape[1] // 2
    lo = lax.bitcast_convert_type(x[:, :n].astype(BF16).astype(F32), U32)
    hi = lax.bitcast_convert_type(x[:, n:].astype(BF16).astype(F32), U32)
    return hi | (lo >> 16)


def _unpack_rows(p):
    lo = lax.bitcast_convert_type(p << 16, F32)
    hi = lax.bitcast_convert_type(p & jnp.uint32(0xFFFF0000), F32)
    return lo, hi


def _load_rows(ref, n):
    return jnp.concatenate([ref[pl.ds(c, n, stride=ROW_SLABS), :] for c in range(ROW_SLABS)], axis=1)


def _store_rows(ref, rows):
    n = rows.shape[0]
    for c in range(ROW_SLABS):
        ref[pl.ds(c, n, stride=ROW_SLABS), :] = rows[:, c * LANES:(c + 1) * LANES]


def _row_at(i):
    return pl.ds(pl.multiple_of(i * ROW_SLABS, ROW_SLABS), ROW_SLABS)


def _x_operands(xs, tile):
    xa, xb = xs
    na = xa.shape[0] // tile
    rows = xa.shape[0] + (0 if xb is None else xb.shape[0])
    specs = (pl.BlockSpec((tile, D_MODEL), lambda i, *_: (jnp.minimum(i, na - 1), 0)),
             pl.BlockSpec((tile, D_MODEL), lambda i, *_: (jnp.maximum(i - na, 0), 0)))
    return (xa, xa if xb is None else xb), specs, na, rows


def _x_tile(xa_ref, xb_ref, na):
    return jnp.where(pl.program_id(0) < na, xa_ref[...], xb_ref[...])


def _rope(t, cos, sin_a, sin_b):
    return (t * cos + pltpu.roll(t, LANES - HALF_ROPE, 1) * sin_a
            + pltpu.roll(t, HALF_ROPE, 1) * sin_b)


def _mixer_in_kernel(xa_ref, xb_ref, lnmix_ref, win_ref, wvbt_ref, lncq_ref, lnckv_ref, wuq_ref,
                     wuk_ref, wuvt_ref, cos_ref, sina_ref, sinb_ref,
                     qa_ref, ka_ref, vat_ref, qb_ref, kb_ref, vbt_ref, *, na):
    h = _rms(_x_tile(xa_ref, xb_ref, na), lnmix_ref[...]).astype(BF16)
    cq = _dot(h, win_ref[:, C_CQ:C_CKV])
    ckv = _dot(h, win_ref[:, C_CKV:C_KR])
    kr = _dot(h, win_ref[:, C_KR:C_QB])
    qb_ref[...] = _dot(h, win_ref[:, C_QB:C_KB]).astype(BF16)
    kb_ref[...] = _dot(h, win_ref[:, C_KB:N_MIX]).astype(BF16)
    vbt = _dot_nt(wvbt_ref[...], h).astype(BF16)
    for c in range(vbt_ref.shape[0]):
        vbt_ref[c] = vbt[:, c * WB:(c + 1) * WB]
    cqn = _rms(cq, lncq_ref[...]).astype(BF16)
    ckvn = _rms(ckv, lnckv_ref[...]).astype(BF16)
    q = _dot(cqn, wuq_ref[...])
    k = _dot(ckvn, wuk_ref[...])
    vat_ref[...] = _dot_nt(wuvt_ref[...], ckvn).astype(BF16)
    cos, sin_a, sin_b = cos_ref[...], sina_ref[...], sinb_ref[...]
    kr_rot = _rope(kr, cos, sin_a, sin_b)
    for hd in range(HA):
        sl = slice(hd * LANES, (hd + 1) * LANES)
        qa_ref[:, sl] = (_rope(q[:, sl], cos, sin_a, sin_b) * (MLA_SCALE * LOG2E)).astype(BF16)
        ka_ref[:, sl] = (k[:, sl] + kr_rot).astype(BF16)


def _mixer_in(xs, p, tabs, seq, tm):
    x_ops, x_specs, na, t = _x_operands(xs, tm)
    nt = t // tm
    spt = seq // tm
    row = lambda i: (i, 0)
    full = lambda i: (0, 0)
    tab = lambda i: (i % spt, 0)
    rows = lambda w: (pl.BlockSpec((tm, w), row), jax.ShapeDtypeStruct((t, w), BF16))
    outs = [
        rows(HA * LANES),
        rows(HA * LANES),
        (pl.BlockSpec((HA * D_V, tm), lambda i: (0, i)),
         jax.ShapeDtypeStruct((HA * D_V, t), BF16)),
        rows(HB * HD),
        rows(2 * KVB * HD),
        (pl.BlockSpec((tm // WB, KVB * HD, WB), lambda i: (i, 0, 0)),
         jax.ShapeDtypeStruct((t // WB, KVB * HD, WB), BF16)),
    ]
    out_specs = [o[0] for o in outs]
    out_shape = [o[1] for o in outs]
    return pl.pallas_call(
        functools.partial(_mixer_in_kernel, na=na),
        grid=(nt,),
        in_specs=[
            *x_specs,
            pl.BlockSpec((1, D_MODEL), full),
            pl.BlockSpec((D_MODEL, N_MIX), full),
            pl.BlockSpec((KVB * HD, D_MODEL), full),
            pl.BlockSpec((1, D_CQ), full),
            pl.BlockSpec((1, D_C), full),
            pl.BlockSpec((D_CQ, HA * LANES), full),
            pl.BlockSpec((D_C, HA * LANES), full),
            pl.BlockSpec((HA * D_V, D_C), full),
            pl.BlockSpec((tm, LANES), tab),
            pl.BlockSpec((tm, LANES), tab),
            pl.BlockSpec((tm, LANES), tab),
        ],
        out_specs=out_specs,
        out_shape=out_shape,
        compiler_params=pltpu.CompilerParams(
            dimension_semantics=("parallel",), vmem_limit_bytes=VMEM_LIMIT),
        name="mixer_in",
    )(*x_ops, p["ln_mix"], p["w_mix"], p["w_vbt"], p["ln_cq"], p["ln_ckv"], p["w_uq"], p["w_uk"], p["w_uvt"],
      tabs["cos"], tabs["sin_a"], tabs["sin_b"])


def _mla_kernel(q_ref, k_ref, vt_ref, o_ref):
    heads = range(MLA_HEADS)
    nch = k_ref.shape[0] // MLA_KEYS

    def scores(j, hh):
        sl = slice(hh * LANES, (hh + 1) * LANES)
        return _dot_nt(k_ref[j * MLA_KEYS:(j + 1) * MLA_KEYS, sl], q_ref[:, sl])

    ones = jnp.ones((2 * SUBLANES, MLA_KEYS), BF16)

    def weighted_values(hh, j, p):
        vt = vt_ref[hh * D_V:(hh + 1) * D_V, j * MLA_KEYS:(j + 1) * MLA_KEYS]
        return _dot(jnp.concatenate([vt, ones], axis=0), p)

    s_next = [scores(0, hh) for hh in heads]
    m = [None] * MLA_HEADS
    acc = [None] * MLA_HEADS
    pending = [None] * MLA_HEADS
    for j in range(nch):
        s_cur = s_next
        if j + 1 < nch:
            s_next = [scores(j + 1, hh) for hh in heads]
        for hh in heads:
            if pending[hh] is not None:
                p_prev, alpha_prev = pending[hh]
                pv = weighted_values(hh, j - 1, p_prev)
                acc[hh] = pv if acc[hh] is None else acc[hh] * alpha_prev + pv
            s = s_cur[hh]
            cmax = jnp.max(s, axis=0, keepdims=True)
            m_new = cmax if m[hh] is None else jnp.maximum(m[hh], cmax)
            alpha = None if m[hh] is None else jnp.exp2(m[hh] - m_new)
            p = jnp.exp2(s - m_new)
            m[hh] = m_new
            pending[hh] = (p.astype(BF16), alpha)
    outs = []
    for hh in heads:
        p_prev, alpha_prev = pending[hh]
        pv = weighted_values(hh, nch - 1, p_prev)
        total = pv if acc[hh] is None else acc[hh] * alpha_prev + pv
        outs.append(total[:D_V] / total[D_V:D_V + 1])
    o_ref[...] = jnp.concatenate(outs, axis=0).T.astype(BF16)


def _mla_attention(qa, ka, va, batch, seq, tq):
    t = qa.shape[0]
    nq = seq // tq
    return pl.pallas_call(
        _mla_kernel,
        grid=(batch, HA // MLA_HEADS, nq),
        in_specs=[
            pl.BlockSpec((tq, MLA_HEADS * LANES), lambda b, j, i: (b * nq + i, j)),
            pl.BlockSpec((seq, MLA_HEADS * LANES), lambda b, j, i: (b, j)),
            pl.BlockSpec((MLA_HEADS * D_V, seq), lambda b, j, i: (j, b)),
        ],
        out_specs=pl.BlockSpec((tq, MLA_HEADS * D_V), lambda b, j, i: (b * nq + i, j)),
        out_shape=jax.ShapeDtypeStruct((t, HA * D_V), BF16),
        compiler_params=pltpu.CompilerParams(
            dimension_semantics=("parallel", "parallel", "parallel"),
            vmem_limit_bytes=VMEM_LIMIT),
        name="mla_attention",
    )(qa, ka, va)


def _window_kernel(q_ref, k_ref, vt_ref, bias_ref, sink_ref, o_ref, *, seq, qblocks):
    nb = seq // WB
    lane = lax.broadcasted_iota(jnp.int32, (WB, LANES), 1)
    lo = lane < HD
    for u in range(qblocks):
        n = pl.program_id(1) * qblocks + u
        rows = slice(u * WB, (u + 1) * WB)
        blocks = (jnp.maximum(n - 1, 0), n, jnp.minimum(n + 1, nb - 1))
        kw = jnp.concatenate([k_ref[pl.ds(pl.multiple_of(b * WB, WB), WB), :] for b in blocks],
                             axis=0)
        vts = [vt_ref[b] for b in blocks]
        edge = jnp.where(n == 0, 0, jnp.where(n == nb - 1, 2, 1))
        heads = []
        for kv in range(KVB):
            kd = kw[:, kv * LANES:(kv + 1) * LANES]
            qs = []
            for jj in range(GQ // 2):
                qp = q_ref[rows, (kv * 2 + jj) * LANES:(kv * 2 + jj + 1) * LANES]
                qs.append(jnp.where(lo, qp, jnp.zeros_like(qp)))
                qs.append(jnp.where(lo, jnp.zeros_like(qp), qp))
            q4 = jnp.concatenate(qs, axis=0)
            st = _dot_nt(kd, q4) + bias_ref[edge, kv]
            sink = sink_ref[kv]
            m = jnp.maximum(jnp.max(st, axis=0, keepdims=True), sink)
            e = jnp.exp2(st - m)
            denom = jnp.sum(e, axis=0, keepdims=True) + jnp.exp2(sink - m)
            vt = jnp.concatenate([v[kv * HD:(kv + 1) * HD, :] for v in vts], axis=1)
            ot = _dot(vt, e.astype(BF16)) / denom
            heads += [ot[:, g * WB:(g + 1) * WB] for g in range(GQ)]
        o_ref[rows, :] = jnp.concatenate(heads, axis=0).T.astype(BF16)


def _window_attention(qb, kb, vbt, bias, sink, batch, seq):
    t = qb.shape[0]
    nb = seq // WB
    qblocks = 8 if nb % 8 == 0 else (4 if nb % 4 == 0 else 1)
    nb //= qblocks
    return pl.pallas_call(
        functools.partial(_window_kernel, seq=seq, qblocks=qblocks),
        grid=(batch, nb),
        in_specs=[
            pl.BlockSpec((qblocks * WB, HB * HD), lambda b, n: (b * nb + n, 0)),
            pl.BlockSpec((seq, 2 * KVB * HD), lambda b, n: (b, 0)),
            pl.BlockSpec((seq // WB, KVB * HD, WB), lambda b, n: (b, 0, 0)),
            pl.BlockSpec((3, KVB, 3 * WB, GQ * WB), lambda b, n: (0, 0, 0, 0)),
            pl.BlockSpec((KVB, 1, GQ * WB), lambda b, n: (0, 0, 0)),
        ],
        out_specs=pl.BlockSpec((qblocks * WB, HB * HD), lambda b, n: (b * nb + n, 0)),
        out_shape=jax.ShapeDtypeStruct((t, HB * HD), BF16),
        compiler_params=pltpu.CompilerParams(
            dimension_semantics=("parallel", "parallel"), vmem_limit_bytes=VMEM_LIMIT),
        name="window_attention",
    )(qb, kb, vbt, bias, sink)


def _mixer_out_kernel(xa_ref, xb_ref, oa_ref, ob_ref, lnmix_ref, wg_ref, woa_ref, wob_ref, wout_ref,
                      lnffn_ref, wrh_ref, br_ref,
                      xn_ref, h2_ref, rf_ref, ri_ref, cnt_ref, carry_ref, *, na):
    i = pl.program_id(0)

    @pl.when(i == 0)
    def _():
        carry_ref[...] = jnp.zeros_like(carry_ref)

    x = _x_tile(xa_ref, xb_ref, na)
    tm = x.shape[0]
    h = _rms(x, lnmix_ref[...]).astype(BF16)
    g = _dot(h, wg_ref[...])
    ya = _dot(oa_ref[...], woa_ref[...])
    yb = _dot(ob_ref[...], wob_ref[...])
    merged = jax.nn.sigmoid(g[:, :D_MODEL]) * ya + jax.nn.sigmoid(g[:, D_MODEL:]) * yb
    xn = x + _dot(merged.astype(BF16), wout_ref[...])
    xn_ref[...] = xn
    h2 = _rms(xn, lnffn_ref[...])
    _store_rows(h2_ref, _pack_rows(h2))

    h2_hi = h2.astype(BF16)
    h2_lo = (h2 - h2_hi.astype(F32)).astype(BF16)
    hi_terms = _dot(h2_hi, wrh_ref[...])
    logits = (hi_terms[:, :LANES] + hi_terms[:, LANES:] + _dot(h2_lo, wrh_ref[:, :LANES])
              + br_ref[...])

    lane = lax.broadcasted_iota(jnp.int32, (tm, LANES), 1)
    lanef = lane.astype(F32)
    big = float(LANES)
    is_g = (lane >= N_EXPERTS) & (lane < N_EXPERTS + N_GROUPS)
    gl = jnp.where(is_g, logits, NEG)
    gmax = jnp.max(gl, axis=1, keepdims=True)
    grp = jnp.min(jnp.where(gl == gmax, lanef, big), axis=1, keepdims=True) - N_EXPERTS
    gsum = jnp.sum(jnp.where(is_g, jnp.exp(gl - gmax), 0.0), axis=1, keepdims=True)
    grp_w = 1.0 / gsum
    is_e = (lanef >= grp * EPG) & (lanef < grp * EPG + EPG)
    el = jnp.where(is_e, logits, NEG)
    v1 = jnp.max(el, axis=1, keepdims=True)
    i1 = jnp.min(jnp.where((el == v1) & is_e, lanef, big), axis=1, keepdims=True)
    a1 = lanef == i1
    el2 = jnp.where(a1, NEG, el)
    v2 = jnp.max(el2, axis=1, keepdims=True)
    i2 = jnp.min(jnp.where((el2 == v2) & is_e & (~a1), lanef, big), axis=1, keepdims=True)
    a2 = lanef == i2
    ex = jnp.exp(v2 - v1)
    p1 = 1.0 / (1.0 + ex)
    gate1 = grp_w * p1
    gate2 = grp_w * (ex * p1)

    a1f = a1.astype(F32)
    a2f = a2.astype(F32)
    cnt = a1f + a2f
    r_i = lax.broadcasted_iota(jnp.int32, (tm, tm), 0)
    c_i = lax.broadcasted_iota(jnp.int32, (tm, tm), 1)
    tri = jnp.where(r_i > c_i, 1.0, 0.0).astype(BF16)
    before = _dot(tri, cnt.astype(BF16)) + carry_ref[...]
    rank1 = jnp.sum(a1f * before, axis=1, keepdims=True)
    rank2 = jnp.sum(a2f * before, axis=1, keepdims=True)
    carry = carry_ref[...] + jnp.sum(cnt, axis=0, keepdims=True)
    carry_ref[...] = carry
    cnt_ref[...] = jnp.broadcast_to(carry, cnt_ref.shape)

    rf_ref[...] = jnp.where(lane == 0, gate1, jnp.where(lane == 1, gate2, 0.0))
    ri = jnp.where(lane == 0, i1, jnp.where(lane == 1, i2,
                                            jnp.where(lane == 2, rank1,
                                                      jnp.where(lane == 3, rank2, 0.0))))
    ri_ref[...] = ri.astype(jnp.int32)


def _mixer_out(xs, oa, ob, p, tm):
    x_ops, x_specs, na, t = _x_operands(xs, tm)
    nt = t // tm
    row = lambda i: (i, 0)
    full = lambda i: (0, 0)
    return pl.pallas_call(
        functools.partial(_mixer_out_kernel, na=na),
        grid=(nt,),
        in_specs=[
            *x_specs,
            pl.BlockSpec((tm, HA * D_V), row),
            pl.BlockSpec((tm, HB * HD), row),
            pl.BlockSpec((1, D_MODEL), full),
            pl.BlockSpec((D_MODEL, 2 * D_MODEL), full),
            pl.BlockSpec((HA * D_V, D_MODEL), full),
            pl.BlockSpec((HB * HD, D_MODEL), full),
            pl.BlockSpec((D_MODEL, D_MODEL), full),
            pl.BlockSpec((1, D_MODEL), full),
            pl.BlockSpec((D_MODEL, 2 * LANES), full),
            pl.BlockSpec((1, LANES), full),
        ],
        out_specs=[
            pl.BlockSpec((tm, D_MODEL), row),
            pl.BlockSpec((ROW_SLABS * tm, LANES), row),
            pl.BlockSpec((tm, LANES), row),
            pl.BlockSpec((tm, LANES), row),
            pl.BlockSpec((8, LANES), full),
        ],
        out_shape=[
            jax.ShapeDtypeStruct((t, D_MODEL), F32),
            jax.ShapeDtypeStruct((ROW_SLABS * t, LANES), U32),
            jax.ShapeDtypeStruct((t, LANES), F32),
            jax.ShapeDtypeStruct((t, LANES), jnp.int32),
            jax.ShapeDtypeStruct((8, LANES), F32),
        ],
        scratch_shapes=[pltpu.VMEM((1, LANES), F32)],
        compiler_params=pltpu.CompilerParams(
            dimension_semantics=("arbitrary",), vmem_limit_bytes=VMEM_LIMIT),
        name="mixer_out",
    )(*x_ops, oa, ob, p["ln_mix"], p["w_g"], p["w_oa"], p["w_ob"], p["w_out"], p["ln_ffn"],
      p["w_r"], p["b_r"])


_ZERO_ROWS = MOE_ROWS // 2
_ZERO_SIZES = tuple(_ZERO_ROWS >> s for s in range((_ZERO_ROWS // SUBLANES).bit_length()))


def _dispatch_kernel(zstart_ref, zlen_ref, nu_ref, dest_ref, h2_ref, xb_ref, zbuf_ref, sem, zsem, *, td):
    i = pl.program_id(0)

    def row_copy(t, d):
        return pltpu.make_async_copy(h2_ref.at[_row_at(t)], xb_ref.at[_row_at(d)], sem)

    def issue(g, c):
        for u in range(ROWS_PER_ISSUE):
            t = g * ROWS_PER_ISSUE + u
            for k in range(TOP_K):
                row_copy(t, dest_ref[0, 0, 2 * t + k]).start(
                    priority=(TOP_K * u + k) % DMA_QUEUES)
        return c

    lax.fori_loop(0, td // ROWS_PER_ISSUE, issue, 0)

    def zero_copy(start, size):
        first = pl.multiple_of(start * ROW_SLABS, ROW_SLABS if size == 1 else SUBLANES * ROW_SLABS)
        return pltpu.make_async_copy(zbuf_ref.at[pl.ds(0, size * ROW_SLABS)],
                                     xb_ref.at[pl.ds(first, size * ROW_SLABS)], zsem)

    def zero_pass(do):
        def per_expert(e, c):
            start = zstart_ref[e]
            head = (-start) & (SUBLANES - 1)
            for j in range(SUBLANES - 1):
                @pl.when(j < head)
                def _(row=start + j):
                    do(zero_copy(row, 1))

            start = start + head
            left = zlen_ref[e] - head
            for size in _ZERO_SIZES:
                take = (left & size) != 0

                @pl.when(take)
                def _(start=start, size=size):
                    do(zero_copy(pl.multiple_of(start, SUBLANES), size))

                start = start + jnp.where(take, size, 0)
            return c

        lax.fori_loop(0, N_EXPERTS, per_expert, 0)

        def unused_block(b, c):
            for part in range(MOE_ROWS // _ZERO_ROWS):
                row = pl.multiple_of(b * MOE_ROWS + part * _ZERO_ROWS, _ZERO_ROWS)
                do(zero_copy(row, _ZERO_ROWS))
            return c

        lax.fori_loop(nu_ref[0], xb_ref.shape[0] // (ROW_SLABS * MOE_ROWS), unused_block, 0)

    @pl.when(i == 0)
    def _():
        zbuf_ref[...] = jnp.zeros_like(zbuf_ref)
        zero_pass(lambda cp: cp.start())
        zero_pass(lambda cp: cp.wait())

    for _ in range(TOP_K):
        pltpu.make_async_copy(h2_ref, xb_ref.at[pl.ds(0, td * ROW_SLABS)], sem).wait()


def _moe_dispatch(h2, dest, zstart, zlen, n_used, n_rows, td):
    t = h2.shape[0] // ROW_SLABS
    nt = t // td
    dest3 = dest.reshape(nt, 1, 2 * td)
    return pl.pallas_call(
        functools.partial(_dispatch_kernel, td=td),
        grid_spec=pltpu.PrefetchScalarGridSpec(
            num_scalar_prefetch=3,
            grid=(nt,),
            in_specs=[
                pl.BlockSpec((1, 1, 2 * td), lambda i, zs, zl, nu: (i, 0, 0),
                             memory_space=pltpu.SMEM),
                pl.BlockSpec((ROW_SLABS * td, LANES), lambda i, zs, zl, nu: (i, 0)),
            ],
            out_specs=pl.BlockSpec(memory_space=pl.ANY),
            scratch_shapes=[pltpu.VMEM((ROW_SLABS * _ZERO_ROWS, LANES), U32),
                            pltpu.SemaphoreType.DMA, pltpu.SemaphoreType.DMA],
        ),
        out_shape=jax.ShapeDtypeStruct((ROW_SLABS * n_rows, LANES), U32),
        compiler_params=pltpu.CompilerParams(dimension_semantics=("arbitrary",)),
        name="moe_dispatch",
    )(zstart, zlen, n_used, dest3, h2)


def _experts_kernel(be_ref, nu_ref, x_ref, wg_ref, wu_ref, wd_ref, y_ref, wgu_s, wd_s):
    b = pl.program_id(0)

    @pl.when((b == 0) | (be_ref[b] != be_ref[jnp.maximum(b - 1, 0)]))
    def _():
        wgu_s[:, :D_EXPERT] = wg_ref[0, 0].astype(BF16)
        wgu_s[:, D_EXPERT:] = wu_ref[0, 0].astype(BF16)
        wd_s[...] = wd_ref[0, 0].astype(BF16)

    @pl.when(b < nu_ref[0])
    def _():
        lo, hi = _unpack_rows(_load_rows(x_ref, MOE_ROWS))
        x = jnp.concatenate([lo.astype(BF16), hi.astype(BF16)], axis=1)
        au = _dot(x, wgu_s[...])
        a, u = au[:, :D_EXPERT], au[:, D_EXPERT:]
        hid = (a * jax.nn.sigmoid(a) * u).astype(BF16)
        _store_rows(y_ref, _pack_rows(_dot(hid, wd_s[...])))

    @pl.when(b >= nu_ref[0])
    def _():
        y_ref[...] = jnp.zeros_like(y_ref)


def _moe_experts(xb, blk_expert, n_used, w_gate, w_up, w_down, layer):
    n_rows = xb.shape[0] // ROW_SLABS
    n_blocks = n_rows // MOE_ROWS
    live = lambda b, be, nu: (jnp.minimum(b, nu[0] - 1), 0)
    expert = lambda b, be, nu: (layer, be[b], 0, 0)
    return pl.pallas_call(
        _experts_kernel,
        grid_spec=pltpu.PrefetchScalarGridSpec(
            num_scalar_prefetch=2,
            grid=(n_blocks,),
            in_specs=[
                pl.BlockSpec((ROW_SLABS * MOE_ROWS, LANES), live),
                pl.BlockSpec((1, 1, D_MODEL, D_EXPERT), expert),
                pl.BlockSpec((1, 1, D_MODEL, D_EXPERT), expert),
                pl.BlockSpec((1, 1, D_EXPERT, D_MODEL), expert),
            ],
            out_specs=pl.BlockSpec((ROW_SLABS * MOE_ROWS, LANES), lambda b, be, nu: (b, 0)),
            scratch_shapes=[pltpu.VMEM((D_MODEL, 2 * D_EXPERT), BF16),
                            pltpu.VMEM((D_EXPERT, D_MODEL), BF16)],
        ),
        out_shape=jax.ShapeDtypeStruct((ROW_SLABS * n_rows, LANES), U32),
        compiler_params=pltpu.CompilerParams(
            dimension_semantics=("arbitrary",), vmem_limit_bytes=VMEM_LIMIT),
        name="moe_experts",
    )(blk_expert, n_used, xb, w_gate, w_up, w_down)


def _combine_kernel(dest_ref, dnext_ref, x_ref, rf_ref, lnf_ref, yb_ref, *rest, tc, split):
    outs, (ybuf_ref, sems) = rest[:-2], rest[-2:]
    i = pl.program_id(0)
    n = pl.num_programs(0)
    slot = i % 2

    def row_copy(src_row, s, k, t):
        return pltpu.make_async_copy(yb_ref.at[_row_at(src_row)],
                                     ybuf_ref.at[s, k, _row_at(t)], sems.at[s])

    def issue(d_ref, s):
        def body(g, c):
            for u in range(ROWS_PER_ISSUE):
                t = g * ROWS_PER_ISSUE + u
                for k in range(TOP_K):
                    row_copy(d_ref[0, 0, 2 * t + k], s, k, t).start(
                        priority=(TOP_K * u + k) % DMA_QUEUES)
            return c

        lax.fori_loop(0, tc // ROWS_PER_ISSUE, body, 0)

    @pl.when(i == 0)
    def _():
        issue(dest_ref, slot)

    @pl.when(i + 1 < n)
    def _():
        issue(dnext_ref, 1 - slot)

    for k in range(TOP_K):
        pltpu.make_async_copy(yb_ref.at[pl.ds(0, tc * ROW_SLABS)], ybuf_ref.at[slot, k],
                              sems.at[slot]).wait()

    rf = rf_ref[...]
    lo0, hi0 = _unpack_rows(_load_rows(ybuf_ref.at[slot, 0], tc))
    lo1, hi1 = _unpack_rows(_load_rows(ybuf_ref.at[slot, 1], tc))
    g0, g1 = rf[:, 0:1], rf[:, 1:2]
    out = x_ref[...] + jnp.concatenate([g0 * lo0 + g1 * lo1, g0 * hi0 + g1 * hi1], axis=1)
    if split is None:
        outs[0][...] = out
    else:
        out = _rms(out, lnf_ref[...])

        @pl.when(i < split)
        def _():
            outs[0][...] = out

        @pl.when(i >= split)
        def _():
            outs[1][...] = out


def _moe_combine(x, rf, dest, yb, ln_final, tc, final_rows):
    t = x.shape[0]
    nt = t // tc
    dest3 = dest.reshape(nt, 1, 2 * tc)
    row = lambda i: (i, 0)
    if final_rows is None:
        split = None
        out_specs = pl.BlockSpec((tc, D_MODEL), row)
        out_shape = jax.ShapeDtypeStruct((t, D_MODEL), F32)
    else:
        split = final_rows[0] // tc
        out_specs = [pl.BlockSpec((tc, D_MODEL), lambda i: (jnp.minimum(i, split - 1), 0)),
                     pl.BlockSpec((tc, D_MODEL), lambda i: (jnp.maximum(i - split, 0), 0))]
        out_shape = [jax.ShapeDtypeStruct((r, D_MODEL), F32) for r in final_rows]
    return pl.pallas_call(
        functools.partial(_combine_kernel, tc=tc, split=split),
        grid=(nt,),
        in_specs=[
            pl.BlockSpec((1, 1, 2 * tc), lambda i: (i, 0, 0), memory_space=pltpu.SMEM),
            pl.BlockSpec((1, 1, 2 * tc), lambda i: (jnp.minimum(i + 1, nt - 1), 0, 0),
                         memory_space=pltpu.SMEM),
            pl.BlockSpec((tc, D_MODEL), row),
            pl.BlockSpec((tc, LANES), row),
            pl.BlockSpec((1, D_MODEL), lambda i: (0, 0)),
            pl.BlockSpec(memory_space=pl.ANY),
        ],
        out_specs=out_specs,
        out_shape=out_shape,
        scratch_shapes=[pltpu.VMEM((2, TOP_K, ROW_SLABS * tc, LANES), U32),
                        pltpu.SemaphoreType.DMA((2,))],
        compiler_params=pltpu.CompilerParams(
            dimension_semantics=("arbitrary",), vmem_limit_bytes=VMEM_LIMIT),
        name="moe_combine",
    )(dest3, dest3, x, rf, ln_final, yb)


def _rope_tables(seq):
    inv = ROPE_BASE ** (-jnp.arange(0, D_ROPE, 2, dtype=F32) / D_ROPE)
    ang = jnp.arange(seq, dtype=F32)[:, None] * inv[None, :]
    cos, sin = jnp.cos(ang), jnp.sin(ang)
    z = lambda w: jnp.zeros((seq, w), F32)
    tail = LANES - D_NOPE - D_ROPE
    return {
        "cos": jnp.concatenate([jnp.ones((seq, D_NOPE), F32), cos, cos, z(tail)], axis=1),
        "sin_a": jnp.concatenate([z(D_NOPE), -sin, z(HALF_ROPE), z(tail)], axis=1),
        "sin_b": jnp.concatenate([z(D_NOPE), z(HALF_ROPE), sin, z(tail)], axis=1),
    }


def _t5_bucket(rel):
    half = NUM_BUCKETS // 2
    max_exact = half // 2
    n = jnp.abs(rel)
    nf = jnp.maximum(n, 1).astype(F32)
    large = max_exact + (jnp.log(nf / max_exact) / math.log(MAX_DIST / max_exact)
                         * (half - max_exact)).astype(jnp.int32)
    large = jnp.minimum(large, half - 1)
    return jnp.where(rel > 0, half, 0) + jnp.where(n < max_exact, n, large)


def _window_bias(rel_bias):
    qi = jnp.arange(WB)[:, None]
    kj = jnp.arange(3 * WB)[None, :]
    rel = kj - WB - qi
    onehot = _t5_bucket(rel)[:, :, None] == jnp.arange(NUM_BUCKETS)
    bias = jnp.sum(jnp.where(onehot[..., None], rel_bias.astype(F32), 0.0), axis=2) * LOG2E
    bias = jnp.where((jnp.abs(rel) <= WINDOW)[:, :, None], bias, NEG)
    bias = jnp.transpose(bias, (2, 0, 1)).reshape(KVB, GQ * WB, 3 * WB)
    bias = jnp.swapaxes(bias, 1, 2)
    key = jnp.arange(3 * WB)[None, :, None]
    return jnp.stack([jnp.where(key < WB, NEG, bias), bias, jnp.where(key >= 2 * WB, NEG, bias)])


def _pack_layer(l, w_in, ln_mix, ln_cq, ln_ckv, w_uq, w_ukv, w_oa, sink, w_ob, w_out, ln_ffn,
                w_gr, b_gr, w_er, b_er):
    wi = w_in[l]
    o = 0
    c_q, o = wi[:, o:o + D_CQ], o + D_CQ
    c_kv, o = wi[:, o:o + D_C], o + D_C
    k_r, o = wi[:, o:o + D_ROPE], o + D_ROPE
    q_b, o = wi[:, o:o + HB * HD], o + HB * HD
    k_b, o = wi[:, o:o + KVB * HD], o + KVB * HD
    v_b, o = wi[:, o:o + KVB * HD], o + KVB * HD
    g_ab = wi[:, o:]
    zc = lambda rows, w: jnp.zeros((rows, w), F32)
    dup = lambda w: jnp.concatenate([w[:, kv * HD:(kv + 1) * HD] for kv in range(KVB) for _ in (0, 1)],
                                    axis=1)
    kr_blk = jnp.concatenate([zc(D_MODEL, D_NOPE), k_r, zc(D_MODEL, LANES - D_NOPE - D_ROPE)], axis=1)
    w_mix = jnp.concatenate([c_q, c_kv, kr_blk, q_b * (WIN_SCALE * LOG2E), dup(k_b)], axis=1)

    uq = w_uq[l].reshape(D_CQ, HA, D_NOPE + D_ROPE)
    uq = jnp.pad(uq, ((0, 0), (0, 0), (0, LANES - D_NOPE - D_ROPE))).reshape(D_CQ, HA * LANES)
    ukv = w_ukv[l].reshape(D_C, HA, D_NOPE + D_V)
    uk = jnp.pad(ukv[:, :, :D_NOPE], ((0, 0), (0, 0), (0, LANES - D_NOPE))).reshape(D_C, HA * LANES)
    uv = ukv[:, :, D_NOPE:].reshape(D_C, HA * D_V)

    w_r = jnp.concatenate([w_er[l], w_gr[l], zc(D_MODEL, LANES - N_EXPERTS - N_GROUPS)], axis=1)
    w_r_hi = w_r.astype(BF16)
    b_r = jnp.concatenate([b_er[l], b_gr[l], jnp.zeros((LANES - N_EXPERTS - N_GROUPS,), F32)])
    sink_row = jnp.repeat(sink[l].astype(F32).reshape(KVB, GQ) * LOG2E, WB, axis=1)[:, None, :]
    return {
        "ln_mix": ln_mix[l][None, :], "w_mix": w_mix.astype(BF16), "w_vbt": v_b.T.astype(BF16),
        "ln_cq": ln_cq[l][None, :], "ln_ckv": ln_ckv[l][None, :],
        "w_uq": uq.astype(BF16), "w_uk": uk.astype(BF16), "w_uvt": uv.T.astype(BF16),
        "w_g": g_ab.astype(BF16), "w_oa": w_oa[l].astype(BF16), "w_ob": w_ob[l].astype(BF16),
        "w_out": w_out[l].astype(BF16), "ln_ffn": ln_ffn[l][None, :],
        "w_r": jnp.concatenate([w_r_hi, (w_r - w_r_hi.astype(F32)).astype(BF16)], axis=1),
        "b_r": b_r[None, :],
        "sink": sink_row,
    }


def _row_tile(n, pref):
    tile = pref
    while n % tile:
        tile //= 2
    return tile


def _trunk(xs, batch, seq, tabs, bias, layers, expert_w, ln_final):
    final_rows = tuple(x.shape[0] for x in xs)
    t = sum(final_rows)
    tm = _row_tile(seq, 512)
    tq = _row_tile(seq, 512)
    tc = _row_tile(t, 512)
    n_blocks = (t * TOP_K) // MOE_ROWS + N_EXPERTS
    n_rows = n_blocks * MOE_ROWS
    for l, p in enumerate(layers):
        qa, ka, vat, qb, kb, vbt = _mixer_in(xs, p, tabs, seq, tm)
        oa = _mla_attention(qa, ka, vat, batch, seq, tq)
        ob = _window_attention(qb, kb, vbt, bias, p["sink"], batch, seq)
        x, h2, rf, ri, cnt = _mixer_out(xs, oa, ob, p, tm)
        counts = cnt[0, :N_EXPERTS].astype(jnp.int32)
        padded = ((counts + MOE_ROWS - 1) // MOE_ROWS) * MOE_ROWS
        pad_end = jnp.cumsum(padded)
        pad_start = pad_end - padded
        onehot = ri[:, 0:TOP_K, None] == jnp.arange(N_EXPERTS, dtype=jnp.int32)
        dest = (jnp.sum(jnp.where(onehot, pad_start, 0), axis=-1) + ri[:, TOP_K:2 * TOP_K]).reshape(-1)
        blk_row = jnp.arange(n_blocks, dtype=jnp.int32)[:, None] * MOE_ROWS
        blk_expert = jnp.minimum(jnp.sum((pad_end[None, :] <= blk_row).astype(jnp.int32), axis=1),
                                 N_EXPERTS - 1)
        n_used = (pad_end[-1:] // MOE_ROWS).astype(jnp.int32)
        xb = _moe_dispatch(h2, dest, (pad_start + counts).astype(jnp.int32),
                           (padded - counts).astype(jnp.int32), n_used, n_rows, tc)
        yb = _moe_experts(xb, blk_expert, n_used, *expert_w, l)
        last = l == len(layers) - 1
        x = _moe_combine(x, rf, dest, yb, ln_final, tc, final_rows if last else None)
        xs = (x, None)
    return x


def kernel(x_prompt, x_sample, rel_bias, ln_mix, w_in, ln_cq, ln_ckv, w_uq, w_ukv, w_oa, sink, w_ob,
           w_out, ln_ffn, w_gr, b_gr, w_er, b_er, w_gate, w_up, w_down, ln_final):
    bp, seq, d = x_prompt.shape
    bs = x_sample.shape[0]
    assert x_sample.shape[1] == seq and seq % WB == 0 and seq >= 3 * WB
    tabs = _rope_tables(seq)
    bias = _window_bias(rel_bias)
    layers = [_pack_layer(l, w_in, ln_mix, ln_cq, ln_ckv, w_uq, w_ukv, w_oa, sink, w_ob, w_out,
                          ln_ffn, w_gr, b_gr, w_er, b_er)
              for l in range(w_in.shape[0])]
    xs = (x_prompt.reshape(bp * seq, d), x_sample.reshape(bs * seq, d))
    y_prompt, y_sample = _trunk(xs, bp + bs, seq, tabs, bias, layers, (w_gate, w_up, w_down),
                                ln_final[None, :])
    return (y_prompt.reshape(bp, seq, d), y_sample.reshape(bs, seq, d))
```

```python
import functools
import math

import jax
import jax.numpy as jnp
from jax import lax
from jax.experimental import pallas as pl
from jax.experimental.pallas import tpu as pltpu

D_MODEL = 1024
DEPTH = 2
HA = 8
D_NOPE = 64
D_ROPE = 32
D_V = 64
D_CQ = 384
D_C = 256
ROPE_BASE = 10000.0
MLA_SCALE = (D_NOPE + D_ROPE) ** -0.5
HB = 8
KVB = 2
GQ = HB // KVB
HD = 64
WINDOW = 128
WB = 128
WIN_SCALE = HD ** -0.5
NUM_BUCKETS = 32
MAX_DIST = 128
N_GROUPS = 4
EPG = 8
N_EXPERTS = N_GROUPS * EPG
TOP_K = 2
D_EXPERT = 256
EPS = 1e-6
NEG = -1e30
LOG2E = math.log2(math.e)

LANES = 128
SUBLANES = 8
HALF_ROPE = D_ROPE // 2
MOE_ROWS = 512
MLA_HEADS = 4
MLA_KEYS = 256
DMA_QUEUES = 2
ROWS_PER_ISSUE = 8
VMEM_LIMIT = 48 * 1024 * 1024

C_CQ = 0
C_CKV = C_CQ + D_CQ
C_KR = C_CKV + D_C
C_QB = C_KR + LANES
C_KB = C_QB + HB * HD
N_MIX = C_KB + 2 * KVB * HD

F32 = jnp.float32
BF16 = jnp.bfloat16
U32 = jnp.uint32
D_PACK = D_MODEL // 2
ROW_SLABS = D_PACK // LANES


def _rms(x, g):
    ms = jnp.mean(x * x, axis=-1, keepdims=True)
    return x * lax.rsqrt(ms + EPS) * g


def _dot(a, b):
    return jnp.dot(a, b, preferred_element_type=F32)


def _dot_nt(a, b):
    return lax.dot_general(a, b, (((1,), (1,)), ((), ())), preferred_element_type=F32)


def _pack_rows(x):
    n = x.shape[1] // 2
    lo = lax.bitcast_convert_type(x[:, :n].astype(BF16).astype(F32), U32)
    hi = lax.bitcast_convert_type(x[:, n:].astype(BF16).astype(F32), U32)
    return hi | (lo >> 16)


def _unpack_rows(p):
    lo = lax.bitcast_convert_type(p << 16, F32)
    hi = lax.bitcast_convert_type(p & jnp.uint32(0xFFFF0000), F32)
    return lo, hi


def _load_rows(ref, n):
    return jnp.concatenate([ref[pl.ds(c, n, stride=ROW_SLABS), :] for c in range(ROW_SLABS)], axis=1)


def _store_rows(ref, rows):
    n = rows.shape[0]
    for c in range(ROW_SLABS):
        ref[pl.ds(c, n, stride=ROW_SLABS), :] = rows[:, c * LANES:(c + 1) * LANES]


def _row_at(i):
    return pl.ds(pl.multiple_of(i * ROW_SLABS, ROW_SLABS), ROW_SLABS)


def _x_operands(xs, tile):
    xa, xb = xs
    na = xa.shape[0] // tile
    rows = xa.shape[0] + (0 if xb is None else xb.shape[0])
    specs = (pl.BlockSpec((tile, D_MODEL), lambda i, *_: (jnp.minimum(i, na - 1), 0)),
             pl.BlockSpec((tile, D_MODEL), lambda i, *_: (jnp.maximum(i - na, 0), 0)))
    return (xa, xa if xb is None else xb), specs, na, rows


def _x_tile(xa_ref, xb_ref, na):
    return jnp.where(pl.program_id(0) < na, xa_ref[...], xb_ref[...])


def _rope(t, cos, sin_a, sin_b):
    return (t * cos + pltpu.roll(t, LANES - HALF_ROPE, 1) * sin_a
            + pltpu.roll(t, HALF_ROPE, 1) * sin_b)


def _mixer_in_kernel(xa_ref, xb_ref, lnmix_ref, win_ref, wvbt_ref, lncq_ref, lnckv_ref, wuq_ref,
                     wuk_ref, wuvt_ref, cos_ref, sina_ref, sinb_ref,
                     qa_ref, ka_ref, vat_ref, qb_ref, kb_ref, vbt_ref, *, na):
    h = _rms(_x_tile(xa_ref, xb_ref, na), lnmix_ref[...]).astype(BF16)
    cq = _dot(h, win_ref[:, C_CQ:C_CKV])
    ckv = _dot(h, win_ref[:, C_CKV:C_KR])
    kr = _dot(h, win_ref[:, C_KR:C_QB])
    qb_ref[...] = _dot(h, win_ref[:, C_QB:C_KB]).astype(BF16)
    kb_ref[...] = _dot(h, win_ref[:, C_KB:N_MIX]).astype(BF16)
    vbt = _dot_nt(wvbt_ref[...], h).astype(BF16)
    for c in range(vbt_ref.shape[0]):
        vbt_ref[c] = vbt[:, c * WB:(c + 1) * WB]
    cqn = _rms(cq, lncq_ref[...]).astype(BF16)
    ckvn = _rms(ckv, lnckv_ref[...]).astype(BF16)
    q = _dot(cqn, wuq_ref[...])
    k = _dot(ckvn, wuk_ref[...])
    vat_ref[...] = _dot_nt(wuvt_ref[...], ckvn).astype(BF16)
    cos, sin_a, sin_b = cos_ref[...], sina_ref[...], sinb_ref[...]
    kr_rot = _rope(kr, cos, sin_a, sin_b)
    for hd in range(HA):
        sl = slice(hd * LANES, (hd + 1) * LANES)
        qa_ref[:, sl] = (_rope(q[:, sl], cos, sin_a, sin_b) * (MLA_SCALE * LOG2E)).astype(BF16)
        ka_ref[:, sl] = (k[:, sl] + kr_rot).astype(BF16)


def _mixer_in(xs, p, tabs, seq, tm):
    x_ops, x_specs, na, t = _x_operands(xs, tm)
    nt = t // tm
    spt = seq // tm
    row = lambda i: (i, 0)
    full = lambda i: (0, 0)
    tab = lambda i: (i % spt, 0)
    rows = lambda w: (pl.BlockSpec((tm, w), row), jax.ShapeDtypeStruct((t, w), BF16))
    outs = [
        rows(HA * LANES),
        rows(HA * LANES),
        (pl.BlockSpec((HA * D_V, tm), lambda i: (0, i)),
         jax.ShapeDtypeStruct((HA * D_V, t), BF16)),
        rows(HB * HD),
        rows(2 * KVB * HD),
        (pl.BlockSpec((tm // WB, KVB * HD, WB), lambda i: (i, 0, 0)),
         jax.ShapeDtypeStruct((t // WB, KVB * HD, WB), BF16)),
    ]
    out_specs = [o[0] for o in outs]
    out_shape = [o[1] for o in outs]
    return pl.pallas_call(
        functools.partial(_mixer_in_kernel, na=na),
        grid=(nt,),
        in_specs=[
            *x_specs,
            pl.BlockSpec((1, D_MODEL), full),
            pl.BlockSpec((D_MODEL, N_MIX), full),
            pl.BlockSpec((KVB * HD, D_MODEL), full),
            pl.BlockSpec((1, D_CQ), full),
            pl.BlockSpec((1, D_C), full),
            pl.BlockSpec((D_CQ, HA * LANES), full),
            pl.BlockSpec((D_C, HA * LANES), full),
            pl.BlockSpec((HA * D_V, D_C), full),
            pl.BlockSpec((tm, LANES), tab),
            pl.BlockSpec((tm, LANES), tab),
            pl.BlockSpec((tm, LANES), tab),
        ],
        out_specs=out_specs,
        out_shape=out_shape,
        compiler_params=pltpu.CompilerParams(
            dimension_semantics=("parallel",), vmem_limit_bytes=VMEM_LIMIT),
        name="mixer_in",
    )(*x_ops, p["ln_mix"], p["w_mix"], p["w_vbt"], p["ln_cq"], p["ln_ckv"], p["w_uq"], p["w_uk"], p["w_uvt"],
      tabs["cos"], tabs["sin_a"], tabs["sin_b"])


def _mla_kernel(q_ref, k_ref, vt_ref, o_ref):
    heads = range(MLA_HEADS)
    nch = k_ref.shape[0] // MLA_KEYS

    def scores(j, hh):
        sl = slice(hh * LANES, (hh + 1) * LANES)
        return _dot_nt(k_ref[j * MLA_KEYS:(j + 1) * MLA_KEYS, sl], q_ref[:, sl])

    ones = jnp.ones((2 * SUBLANES, MLA_KEYS), BF16)

    def weighted_values(hh, j, p):
        vt = vt_ref[hh * D_V:(hh + 1) * D_V, j * MLA_KEYS:(j + 1) * MLA_KEYS]
        return _dot(jnp.concatenate([vt, ones], axis=0), p)

    s_next = [scores(0, hh) for hh in heads]
    m = [None] * MLA_HEADS
    acc = [None] * MLA_HEADS
    pending = [None] * MLA_HEADS
    for j in range(nch):
        s_cur = s_next
        if j + 1 < nch:
            s_next = [scores(j + 1, hh) for hh in heads]
        for hh in heads:
            if pending[hh] is not None:
                p_prev, alpha_prev = pending[hh]
                pv = weighted_values(hh, j - 1, p_prev)
                acc[hh] = pv if acc[hh] is None else acc[hh] * alpha_prev + pv
            s = s_cur[hh]
            cmax = jnp.max(s, axis=0, keepdims=True)
            m_new = cmax if m[hh] is None else jnp.maximum(m[hh], cmax)
            alpha = None if m[hh] is None else jnp.exp2(m[hh] - m_new)
            p = jnp.exp2(s - m_new)
            m[hh] = m_new
            pending[hh] = (p.astype(BF16), alpha)
    outs = []
    for hh in heads:
        p_prev, alpha_prev = pending[hh]
        pv = weighted_values(hh, nch - 1, p_prev)
        total = pv if acc[hh] is None else acc[hh] * alpha_prev + pv
        outs.append(total[:D_V] / total[D_V:D_V + 1])
    o_ref[...] = jnp.concatenate(outs, axis=0).T.astype(BF16)


def _mla_attention(qa, ka, va, batch, seq, tq):
    t = qa.shape[0]
    nq = seq // tq
    return pl.pallas_call(
        _mla_kernel,
        grid=(batch, HA // MLA_HEADS, nq),
        in_specs=[
            pl.BlockSpec((tq, MLA_HEADS * LANES), lambda b, j, i: (b * nq + i, j)),
            pl.BlockSpec((seq, MLA_HEADS * LANES), lambda b, j, i: (b, j)),
            pl.BlockSpec((MLA_HEADS * D_V, seq), lambda b, j, i: (j, b)),
        ],
        out_specs=pl.BlockSpec((tq, MLA_HEADS * D_V), lambda b, j, i: (b * nq + i, j)),
        out_shape=jax.ShapeDtypeStruct((t, HA * D_V), BF16),
        compiler_params=pltpu.CompilerParams(
            dimension_semantics=("parallel", "parallel", "parallel"),
            vmem_limit_bytes=VMEM_LIMIT),
        name="mla_attention",
    )(qa, ka, va)


def _window_kernel(q_ref, k_ref, vt_ref, bias_ref, sink_ref, o_ref, *, seq, qblocks):
    nb = seq // WB
    lane = lax.broadcasted_iota(jnp.int32, (WB, LANES), 1)
    lo = lane < HD
    for u in range(qblocks):
        n = pl.program_id(1) * qblocks + u
        rows = slice(u * WB, (u + 1) * WB)
        blocks = (jnp.maximum(n - 1, 0), n, jnp.minimum(n + 1, nb - 1))
        kw = jnp.concatenate([k_ref[pl.ds(pl.multiple_of(b * WB, WB), WB), :] for b in blocks],
                             axis=0)
        vts = [vt_ref[b] for b in blocks]
        edge = jnp.where(n == 0, 0, jnp.where(n == nb - 1, 2, 1))
        heads = []
        for kv in range(KVB):
            kd = kw[:, kv * LANES:(kv + 1) * LANES]
            qs = []
            for jj in range(GQ // 2):
                qp = q_ref[rows, (kv * 2 + jj) * LANES:(kv * 2 + jj + 1) * LANES]
                qs.append(jnp.where(lo, qp, jnp.zeros_like(qp)))
                qs.append(jnp.where(lo, jnp.zeros_like(qp), qp))
            q4 = jnp.concatenate(qs, axis=0)
            st = _dot_nt(kd, q4) + bias_ref[edge, kv]
            sink = sink_ref[kv]
            m = jnp.maximum(jnp.max(st, axis=0, keepdims=True), sink)
            e = jnp.exp2(st - m)
            denom = jnp.sum(e, axis=0, keepdims=True) + jnp.exp2(sink - m)
            vt = jnp.concatenate([v[kv * HD:(kv + 1) * HD, :] for v in vts], axis=1)
            ot = _dot(vt, e.astype(BF16)) / denom
            heads += [ot[:, g * WB:(g + 1) * WB] for g in range(GQ)]
        o_ref[rows, :] = jnp.concatenate(heads, axis=0).T.astype(BF16)


def _window_attention(qb, kb, vbt, bias, sink, batch, seq):
    t = qb.shape[0]
    nb = seq // WB
    qblocks = 8 if nb % 8 == 0 else (4 if nb % 4 == 0 else 1)
    nb //= qblocks
    return pl.pallas_call(
        functools.partial(_window_kernel, seq=seq, qblocks=qblocks),
        grid=(batch, nb),
        in_specs=[
            pl.BlockSpec((qblocks * WB, HB * HD), lambda b, n: (b * nb + n, 0)),
            pl.BlockSpec((seq, 2 * KVB * HD), lambda b, n: (b, 0)),
            pl.BlockSpec((seq // WB, KVB * HD, WB), lambda b, n: (b, 0, 0)),
            pl.BlockSpec((3, KVB, 3 * WB, GQ * WB), lambda b, n: (0, 0, 0, 0)),
            pl.BlockSpec((KVB, 1, GQ * WB), lambda b, n: (0, 0, 0)),
        ],
        out_specs=pl.BlockSpec((qblocks * WB, HB * HD), lambda b, n: (b * nb + n, 0)),
        out_shape=jax.ShapeDtypeStruct((t, HB * HD), BF16),
        compiler_params=pltpu.CompilerParams(
            dimension_semantics=("parallel", "parallel"), vmem_limit_bytes=VMEM_LIMIT),
        name="window_attention",
    )(qb, kb, vbt, bias, sink)


def _mixer_out_kernel(xa_ref, xb_ref, oa_ref, ob_ref, lnmix_ref, wg_ref, woa_ref, wob_ref, wout_ref,
                      lnffn_ref, wrh_ref, br_ref,
                      xn_ref, h2_ref, rf_ref, ri_ref, cnt_ref, carry_ref, *, na):
    i = pl.program_id(0)

    @pl.when(i == 0)
    def _():
        carry_ref[...] = jnp.zeros_like(carry_ref)

    x = _x_tile(xa_ref, xb_ref, na)
    tm = x.shape[0]
    h = _rms(x, lnmix_ref[...]).astype(BF16)
    g = _dot(h, wg_ref[...])
    ya = _dot(oa_ref[...], woa_ref[...])
    yb = _dot(ob_ref[...], wob_ref[...])
    merged = jax.nn.sigmoid(g[:, :D_MODEL]) * ya + jax.nn.sigmoid(g[:, D_MODEL:]) * yb
    xn = x + _dot(merged.astype(BF16), wout_ref[...])
    xn_ref[...] = xn
    h2 = _rms(xn, lnffn_ref[...])
    _store_rows(h2_ref, _pack_rows(h2))

    h2_hi = h2.astype(BF16)
    h2_lo = (h2 - h2_hi.astype(F32)).astype(BF16)
    hi_terms = _dot(h2_hi, wrh_ref[...])
    logits = (hi_terms[:, :LANES] + hi_terms[:, LANES:] + _dot(h2_lo, wrh_ref[:, :LANES])
              + br_ref[...])

    lane = lax.broadcasted_iota(jnp.int32, (tm, LANES), 1)
    lanef = lane.astype(F32)
    big = float(LANES)
    is_g = (lane >= N_EXPERTS) & (lane < N_EXPERTS + N_GROUPS)
    gl = jnp.where(is_g, logits, NEG)
    gmax = jnp.max(gl, axis=1, keepdims=True)
    grp = jnp.min(jnp.where(gl == gmax, lanef, big), axis=1, keepdims=True) - N_EXPERTS
    gsum = jnp.sum(jnp.where(is_g, jnp.exp(gl - gmax), 0.0), axis=1, keepdims=True)
    grp_w = 1.0 / gsum
    is_e = (lanef >= grp * EPG) & (lanef < grp * EPG + EPG)
    el = jnp.where(is_e, logits, NEG)
    v1 = jnp.max(el, axis=1, keepdims=True)
    i1 = jnp.min(jnp.where((el == v1) & is_e, lanef, big), axis=1, keepdims=True)
    a1 = lanef == i1
    el2 = jnp.where(a1, NEG, el)
    v2 = jnp.max(el2, axis=1, keepdims=True)
    i2 = jnp.min(jnp.where((el2 == v2) & is_e & (~a1), lanef, big), axis=1, keepdims=True)
    a2 = lanef == i2
    ex = jnp.exp(v2 - v1)
    p1 = 1.0 / (1.0 + ex)
    gate1 = grp_w * p1
    gate2 = grp_w * (ex * p1)

    a1f = a1.astype(F32)
    a2f = a2.astype(F32)
    cnt = a1f + a2f
    r_i = lax.broadcasted_iota(jnp.int32, (tm, tm), 0)
    c_i = lax.broadcasted_iota(jnp.int32, (tm, tm), 1)
    tri = jnp.where(r_i > c_i, 1.0, 0.0).astype(BF16)
    before = _dot(tri, cnt.astype(BF16)) + carry_ref[...]
    rank1 = jnp.sum(a1f * before, axis=1, keepdims=True)
    rank2 = jnp.sum(a2f * before, axis=1, keepdims=True)
    carry = carry_ref[...] + jnp.sum(cnt, axis=0, keepdims=True)
    carry_ref[...] = carry
    cnt_ref[...] = jnp.broadcast_to(carry, cnt_ref.shape)

    rf_ref[...] = jnp.where(lane == 0, gate1, jnp.where(lane == 1, gate2, 0.0))
    ri = jnp.where(lane == 0, i1, jnp.where(lane == 1, i2,
                                            jnp.where(lane == 2, rank1,
                                                      jnp.where(lane == 3, rank2, 0.0))))
    ri_ref[...] = ri.astype(jnp.int32)


def _mixer_out(xs, oa, ob, p, tm):
    x_ops, x_specs, na, t = _x_operands(xs, tm)
    nt = t // tm
    row = lambda i: (i, 0)
    full = lambda i: (0, 0)
    return pl.pallas_call(
        functools.partial(_mixer_out_kernel, na=na),
        grid=(nt,),
        in_specs=[
            *x_specs,
            pl.BlockSpec((tm, HA * D_V), row),
            pl.BlockSpec((tm, HB * HD), row),
            pl.BlockSpec((1, D_MODEL), full),
            pl.BlockSpec((D_MODEL, 2 * D_MODEL), full),
            pl.BlockSpec((HA * D_V, D_MODEL), full),
            pl.BlockSpec((HB * HD, D_MODEL), full),
            pl.BlockSpec((D_MODEL, D_MODEL), full),
            pl.BlockSpec((1, D_MODEL), full),
            pl.BlockSpec((D_MODEL, 2 * LANES), full),
            pl.BlockSpec((1, LANES), full),
        ],
        out_specs=[
            pl.BlockSpec((tm, D_MODEL), row),
            pl.BlockSpec((ROW_SLABS * tm, LANES), row),
            pl.BlockSpec((tm, LANES), row),
            pl.BlockSpec((tm, LANES), row),
            pl.BlockSpec((8, LANES), full),
        ],
        out_shape=[
            jax.ShapeDtypeStruct((t, D_MODEL), F32),
            jax.ShapeDtypeStruct((ROW_SLABS * t, LANES), U32),
            jax.ShapeDtypeStruct((t, LANES), F32),
            jax.ShapeDtypeStruct((t, LANES), jnp.int32),
            jax.ShapeDtypeStruct((8, LANES), F32),
        ],
        scratch_shapes=[pltpu.VMEM((1, LANES), F32)],
        compiler_params=pltpu.CompilerParams(
            dimension_semantics=("arbitrary",), vmem_limit_bytes=VMEM_LIMIT),
        name="mixer_out",
    )(*x_ops, oa, ob, p["ln_mix"], p["w_g"], p["w_oa"], p["w_ob"], p["w_out"], p["ln_ffn"],
      p["w_r"], p["b_r"])


_ZERO_ROWS = MOE_ROWS // 2
_ZERO_SIZES = tuple(_ZERO_ROWS >> s for s in range((_ZERO_ROWS // SUBLANES).bit_length()))


def _dispatch_kernel(zstart_ref, zlen_ref, nu_ref, dest_ref, h2_ref, xb_ref, zbuf_ref, sem, zsem, *, td):
    i = pl.program_id(0)

    def row_copy(t, d):
        return pltpu.make_async_copy(h2_ref.at[_row_at(t)], xb_ref.at[_row_at(d)], sem)

    def issue(g, c):
        for u in range(ROWS_PER_ISSUE):
            t = g * ROWS_PER_ISSUE + u
            for k in range(TOP_K):
                row_copy(t, dest_ref[0, 0, 2 * t + k]).start(
                    priority=(TOP_K * u + k) % DMA_QUEUES)
        return c

    lax.fori_loop(0, td // ROWS_PER_ISSUE, issue, 0)

    def zero_copy(start, size):
        first = pl.multiple_of(start * ROW_SLABS, ROW_SLABS if size == 1 else SUBLANES * ROW_SLABS)
        return pltpu.make_async_copy(zbuf_ref.at[pl.ds(0, size * ROW_SLABS)],
                                     xb_ref.at[pl.ds(first, size * ROW_SLABS)], zsem)

    def zero_pass(do):
        def per_expert(e, c):
            start = zstart_ref[e]
            head = (-start) & (SUBLANES - 1)
            for j in range(SUBLANES - 1):
                @pl.when(j < head)
                def _(row=start + j):
                    do(zero_copy(row, 1))

            start = start + head
            left = zlen_ref[e] - head
            for size in _ZERO_SIZES:
                take = (left & size) != 0

                @pl.when(take)
                def _(start=start, size=size):
                    do(zero_copy(pl.multiple_of(start, SUBLANES), size))

                start = start + jnp.where(take, size, 0)
            return c

        lax.fori_loop(0, N_EXPERTS, per_expert, 0)

        def unused_block(b, c):
            for part in range(MOE_ROWS // _ZERO_ROWS):
                row = pl.multiple_of(b * MOE_ROWS + part * _ZERO_ROWS, _ZERO_ROWS)
                do(zero_copy(row, _ZERO_ROWS))
            return c

        lax.fori_loop(nu_ref[0], xb_ref.shape[0] // (ROW_SLABS * MOE_ROWS), unused_block, 0)

    @pl.when(i == 0)
    def _():
        zbuf_ref[...] = jnp.zeros_like(zbuf_ref)
        zero_pass(lambda cp: cp.start())
        zero_pass(lambda cp: cp.wait())

    for _ in range(TOP_K):
        pltpu.make_async_copy(h2_ref, xb_ref.at[pl.ds(0, td * ROW_SLABS)], sem).wait()


def _moe_dispatch(h2, dest, zstart, zlen, n_used, n_rows, td):
    t = h2.shape[0] // ROW_SLABS
    nt = t // td
    dest3 = dest.reshape(nt, 1, 2 * td)
    return pl.pallas_call(
        functools.partial(_dispatch_kernel, td=td),
        grid_spec=pltpu.PrefetchScalarGridSpec(
            num_scalar_prefetch=3,
            grid=(nt,),
            in_specs=[
                pl.BlockSpec((1, 1, 2 * td), lambda i, zs, zl, nu: (i, 0, 0),
                             memory_space=pltpu.SMEM),
                pl.BlockSpec((ROW_SLABS * td, LANES), lambda i, zs, zl, nu: (i, 0)),
            ],
            out_specs=pl.BlockSpec(memory_space=pl.ANY),
            scratch_shapes=[pltpu.VMEM((ROW_SLABS * _ZERO_ROWS, LANES), U32),
                            pltpu.SemaphoreType.DMA, pltpu.SemaphoreType.DMA],
        ),
        out_shape=jax.ShapeDtypeStruct((ROW_SLABS * n_rows, LANES), U32),
        compiler_params=pltpu.CompilerParams(dimension_semantics=("arbitrary",)),
        name="moe_dispatch",
    )(zstart, zlen, n_used, dest3, h2)


def _experts_kernel(be_ref, nu_ref, x_ref, wg_ref, wu_ref, wd_ref, y_ref, wgu_s, wd_s):
    b = pl.program_id(0)

    @pl.when((b == 0) | (be_ref[b] != be_ref[jnp.maximum(b - 1, 0)]))
    def _():
        wgu_s[:, :D_EXPERT] = wg_ref[0, 0].astype(BF16)
        wgu_s[:, D_EXPERT:] = wu_ref[0, 0].astype(BF16)
        wd_s[...] = wd_ref[0, 0].astype(BF16)

    @pl.when(b < nu_ref[0])
    def _():
        lo, hi = _unpack_rows(_load_rows(x_ref, MOE_ROWS))
        x = jnp.concatenate([lo.astype(BF16), hi.astype(BF16)], axis=1)
        au = _dot(x, wgu_s[...])
        a, u = au[:, :D_EXPERT], au[:, D_EXPERT:]
        hid = (a * jax.nn.sigmoid(a) * u).astype(BF16)
        _store_rows(y_ref, _pack_rows(_dot(hid, wd_s[...])))

    @pl.when(b >= nu_ref[0])
    def _():
        y_ref[...] = jnp.zeros_like(y_ref)


def _moe_experts(xb, blk_expert, n_used, w_gate, w_up, w_down, layer):
    n_rows = xb.shape[0] // ROW_SLABS
    n_blocks = n_rows // MOE_ROWS
    live = lambda b, be, nu: (jnp.minimum(b, nu[0] - 1), 0)
    expert = lambda b, be, nu: (layer, be[b], 0, 0)
    return pl.pallas_call(
        _experts_kernel,
        grid_spec=pltpu.PrefetchScalarGridSpec(
            num_scalar_prefetch=2,
            grid=(n_blocks,),
            in_specs=[
                pl.BlockSpec((ROW_SLABS * MOE_ROWS, LANES), live),
                pl.BlockSpec((1, 1, D_MODEL, D_EXPERT), expert),
                pl.BlockSpec((1, 1, D_MODEL, D_EXPERT), expert),
                pl.BlockSpec((1, 1, D_EXPERT, D_MODEL), expert),
            ],
            out_specs=pl.BlockSpec((ROW_SLABS * MOE_ROWS, LANES), lambda b, be, nu: (b, 0)),
            scratch_shapes=[pltpu.VMEM((D_MODEL, 2 * D_EXPERT), BF16),
                            pltpu.VMEM((D_EXPERT, D_MODEL), BF16)],
        ),
        out_shape=jax.ShapeDtypeStruct((ROW_SLABS * n_rows, LANES), U32),
        compiler_params=pltpu.CompilerParams(
            dimension_semantics=("arbitrary",), vmem_limit_bytes=VMEM_LIMIT),
        name="moe_experts",
    )(blk_expert, n_used, xb, w_gate, w_up, w_down)


def _combine_kernel(dest_ref, dnext_ref, x_ref, rf_ref, lnf_ref, yb_ref, *rest, tc, split):
    outs, (ybuf_ref, sems) = rest[:-2], rest[-2:]
    i = pl.program_id(0)
    n = pl.num_programs(0)
    slot = i % 2

    def row_copy(src_row, s, k, t):
        return pltpu.make_async_copy(yb_ref.at[_row_at(src_row)],
                                     ybuf_ref.at[s, k, _row_at(t)], sems.at[s])

    def issue(d_ref, s):
        def body(g, c):
            for u in range(ROWS_PER_ISSUE):
                t = g * ROWS_PER_ISSUE + u
                for k in range(TOP_K):
                    row_copy(d_ref[0, 0, 2 * t + k], s, k, t).start(
                        priority=(TOP_K * u + k) % DMA_QUEUES)
            return c

        lax.fori_loop(0, tc // ROWS_PER_ISSUE, body, 0)

    @pl.when(i == 0)
    def _():
        issue(dest_ref, slot)

    @pl.when(i + 1 < n)
    def _():
        issue(dnext_ref, 1 - slot)

    for k in range(TOP_K):
        pltpu.make_async_copy(yb_ref.at[pl.ds(0, tc * ROW_SLABS)], ybuf_ref.at[slot, k],
                              sems.at[slot]).wait()

    rf = rf_ref[...]
    lo0, hi0 = _unpack_rows(_load_rows(ybuf_ref.at[slot, 0], tc))
    lo1, hi1 = _unpack_rows(_load_rows(ybuf_ref.at[slot, 1], tc))
    g0, g1 = rf[:, 0:1], rf[:, 1:2]
    out = x_ref[...] + jnp.concatenate([g0 * lo0 + g1 * lo1, g0 * hi0 + g1 * hi1], axis=1)
    if split is None:
        outs[0][...] = out
    else:
        out = _rms(out, lnf_ref[...])

        @pl.when(i < split)
        def _():
            outs[0][...] = out

        @pl.when(i >= split)
        def _():
            outs[1][...] = out


def _moe_combine(x, rf, dest, yb, ln_final, tc, final_rows):
    t = x.shape[0]
    nt = t // tc
    dest3 = dest.reshape(nt, 1, 2 * tc)
    row = lambda i: (i, 0)
    if final_rows is None:
        split = None
        out_specs = pl.BlockSpec((tc, D_MODEL), row)
        out_shape = jax.ShapeDtypeStruct((t, D_MODEL), F32)
    else:
        split = final_rows[0] // tc
        out_specs = [pl.BlockSpec((tc, D_MODEL), lambda i: (jnp.minimum(i, split - 1), 0)),
                     pl.BlockSpec((tc, D_MODEL), lambda i: (jnp.maximum(i - split, 0), 0))]
        out_shape = [jax.ShapeDtypeStruct((r, D_MODEL), F32) for r in final_rows]
    return pl.pallas_call(
        functools.partial(_combine_kernel, tc=tc, split=split),
        grid=(nt,),
        in_specs=[
            pl.BlockSpec((1, 1, 2 * tc), lambda i: (i, 0, 0), memory_space=pltpu.SMEM),
            pl.BlockSpec((1, 1, 2 * tc), lambda i: (jnp.minimum(i + 1, nt - 1), 0, 0),
                         memory_space=pltpu.SMEM),
            pl.BlockSpec((tc, D_MODEL), row),
            pl.BlockSpec((tc, LANES), row),
            pl.BlockSpec((1, D_MODEL), lambda i: (0, 0)),
            pl.BlockSpec(memory_space=pl.ANY),
        ],
        out_specs=out_specs,
        out_shape=out_shape,
        scratch_shapes=[pltpu.VMEM((2, TOP_K, ROW_SLABS * tc, LANES), U32),
                        pltpu.SemaphoreType.DMA((2,))],
        compiler_params=pltpu.CompilerParams(
            dimension_semantics=("arbitrary",), vmem_limit_bytes=VMEM_LIMIT),
        name="moe_combine",
    )(dest3, dest3, x, rf, ln_final, yb)


def _rope_tables(seq):
    inv = ROPE_BASE ** (-jnp.arange(0, D_ROPE, 2, dtype=F32) / D_ROPE)
    ang = jnp.arange(seq, dtype=F32)[:, None] * inv[None, :]
    cos, sin = jnp.cos(ang), jnp.sin(ang)
    z = lambda w: jnp.zeros((seq, w), F32)
    tail = LANES - D_NOPE - D_ROPE
    return {
        "cos": jnp.concatenate([jnp.ones((seq, D_NOPE), F32), cos, cos, z(tail)], axis=1),
        "sin_a": jnp.concatenate([z(D_NOPE), -sin, z(HALF_ROPE), z(tail)], axis=1),
        "sin_b": jnp.concatenate([z(D_NOPE), z(HALF_ROPE), sin, z(tail)], axis=1),
    }


def _t5_bucket(rel):
    half = NUM_BUCKETS // 2
    max_exact = half // 2
    n = jnp.abs(rel)
    nf = jnp.maximum(n, 1).astype(F32)
    large = max_exact + (jnp.log(nf / max_exact) / math.log(MAX_DIST / max_exact)
                         * (half - max_exact)).astype(jnp.int32)
    large = jnp.minimum(large, half - 1)
    return jnp.where(rel > 0, half, 0) + jnp.where(n < max_exact, n, large)


def _window_bias(rel_bias):
    qi = jnp.arange(WB)[:, None]
    kj = jnp.arange(3 * WB)[None, :]
    rel = kj - WB - qi
    onehot = _t5_bucket(rel)[:, :, None] == jnp.arange(NUM_BUCKETS)
    bias = jnp.sum(jnp.where(onehot[..., None], rel_bias.astype(F32), 0.0), axis=2) * LOG2E
    bias = jnp.where((jnp.abs(rel) <= WINDOW)[:, :, None], bias, NEG)
    bias = jnp.transpose(bias, (2, 0, 1)).reshape(KVB, GQ * WB, 3 * WB)
    bias = jnp.swapaxes(bias, 1, 2)
    key = jnp.arange(3 * WB)[None, :, None]
    return jnp.stack([jnp.where(key < WB, NEG, bias), bias, jnp.where(key >= 2 * WB, NEG, bias)])


def _pack_layer(l, w_in, ln_mix, ln_cq, ln_ckv, w_uq, w_ukv, w_oa, sink, w_ob, w_out, ln_ffn,
                w_gr, b_gr, w_er, b_er):
    wi = w_in[l]
    o = 0
    c_q, o = wi[:, o:o + D_CQ], o + D_CQ
    c_kv, o = wi[:, o:o + D_C], o + D_C
    k_r, o = wi[:, o:o + D_ROPE], o + D_ROPE
    q_b, o = wi[:, o:o + HB * HD], o + HB * HD
    k_b, o = wi[:, o:o + KVB * HD], o + KVB * HD
    v_b, o = wi[:, o:o + KVB * HD], o + KVB * HD
    g_ab = wi[:, o:]
    zc = lambda rows, w: jnp.zeros((rows, w), F32)
    dup = lambda w: jnp.concatenate([w[:, kv * HD:(kv + 1) * HD] for kv in range(KVB) for _ in (0, 1)],
                                    axis=1)
    kr_blk = jnp.concatenate([zc(D_MODEL, D_NOPE), k_r, zc(D_MODEL, LANES - D_NOPE - D_ROPE)], axis=1)
    w_mix = jnp.concatenate([c_q, c_kv, kr_blk, q_b * (WIN_SCALE * LOG2E), dup(k_b)], axis=1)

    uq = w_uq[l].reshape(D_CQ, HA, D_NOPE + D_ROPE)
    uq = jnp.pad(uq, ((0, 0), (0, 0), (0, LANES - D_NOPE - D_ROPE))).reshape(D_CQ, HA * LANES)
    ukv = w_ukv[l].reshape(D_C, HA, D_NOPE + D_V)
    uk = jnp.pad(ukv[:, :, :D_NOPE], ((0, 0), (0, 0), (0, LANES - D_NOPE))).reshape(D_C, HA * LANES)
    uv = ukv[:, :, D_NOPE:].reshape(D_C, HA * D_V)

    w_r = jnp.concatenate([w_er[l], w_gr[l], zc(D_MODEL, LANES - N_EXPERTS - N_GROUPS)], axis=1)
    w_r_hi = w_r.astype(BF16)
    b_r = jnp.concatenate([b_er[l], b_gr[l], jnp.zeros((LANES - N_EXPERTS - N_GROUPS,), F32)])
    sink_row = jnp.repeat(sink[l].astype(F32).reshape(KVB, GQ) * LOG2E, WB, axis=1)[:, None, :]
    return {
        "ln_mix": ln_mix[l][None, :], "w_mix": w_mix.astype(BF16), "w_vbt": v_b.T.astype(BF16),
        "ln_cq": ln_cq[l][None, :], "ln_ckv": ln_ckv[l][None, :],
        "w_uq": uq.astype(BF16), "w_uk": uk.astype(BF16), "w_uvt": uv.T.astype(BF16),
        "w_g": g_ab.astype(BF16), "w_oa": w_oa[l].astype(BF16), "w_ob": w_ob[l].astype(BF16),
        "w_out": w_out[l].astype(BF16), "ln_ffn": ln_ffn[l][None, :],
        "w_r": jnp.concatenate([w_r_hi, (w_r - w_r_hi.astype(F32)).astype(BF16)], axis=1),
        "b_r": b_r[None, :],
        "sink": sink_row,
    }


def _row_tile(n, pref):
    tile = pref
    while n % tile:
        tile //= 2
    return tile


def _trunk(xs, batch, seq, tabs, bias, layers, expert_w, ln_final):
    final_rows = tuple(x.shape[0] for x in xs)
    t = sum(final_rows)
    tm = _row_tile(seq, 512)
    tq = _row_tile(seq, 512)
    tc = _row_tile(t, 1024)
    n_blocks = (t * TOP_K) // MOE_ROWS + N_EXPERTS
    n_rows = n_blocks * MOE_ROWS
    for l, p in enumerate(layers):
        qa, ka, vat, qb, kb, vbt = _mixer_in(xs, p, tabs, seq, tm)
        oa = _mla_attention(qa, ka, vat, batch, seq, tq)
        ob = _window_attention(qb, kb, vbt, bias, p["sink"], batch, seq)
        x, h2, rf, ri, cnt = _mixer_out(xs, oa, ob, p, tm)
        counts = cnt[0, :N_EXPERTS].astype(jnp.int32)
        padded = ((counts + MOE_ROWS - 1) // MOE_ROWS) * MOE_ROWS
        pad_end = jnp.cumsum(padded)
        pad_start = pad_end - padded
        onehot = ri[:, 0:TOP_K, None] == jnp.arange(N_EXPERTS, dtype=jnp.int32)
        dest = (jnp.sum(jnp.where(onehot, pad_start, 0), axis=-1) + ri[:, TOP_K:2 * TOP_K]).reshape(-1)
        blk_row = jnp.arange(n_blocks, dtype=jnp.int32)[:, None] * MOE_ROWS
        blk_expert = jnp.minimum(jnp.sum((pad_end[None, :] <= blk_row).astype(jnp.int32), axis=1),
                                 N_EXPERTS - 1)
        n_used = (pad_end[-1:] // MOE_ROWS).astype(jnp.int32)
        xb = _moe_dispatch(h2, dest, (pad_start + counts).astype(jnp.int32),
                           (padded - counts).astype(jnp.int32), n_used, n_rows, tc)
        yb = _moe_experts(xb, blk_expert, n_used, *expert_w, l)
        last = l == len(layers) - 1
        x = _moe_combine(x, rf, dest, yb, ln_final, tc, final_rows if last else None)
        xs = (x, None)
    return x


def kernel(x_prompt, x_sample, rel_bias, ln_mix, w_in, ln_cq, ln_ckv, w_uq, w_ukv, w_oa, sink, w_ob,
           w_out, ln_ffn, w_gr, b_gr, w_er, b_er, w_gate, w_up, w_down, ln_final):
    bp, seq, d = x_prompt.shape
    bs = x_sample.shape[0]
    assert x_sample.shape[1] == seq and seq % WB == 0 and seq >= 3 * WB
    tabs = _rope_tables(seq)
    bias = _window_bias(rel_bias)
    layers = [_pack_layer(l, w_in, ln_mix, ln_cq, ln_ckv, w_uq, w_ukv, w_oa, sink, w_ob, w_out,
                          ln_ffn, w_gr, b_gr, w_er, b_er)
              for l in range(w_in.shape[0])]
    xs = (x_prompt.reshape(bp * seq, d), x_sample.reshape(bs * seq, d))
    y_prompt, y_sample = _trunk(xs, bp + bs, seq, tabs, bias, layers, (w_gate, w_up, w_down),
                                ln_final[None, :])
    return (y_prompt.reshape(bp, seq, d), y_sample.reshape(bs, seq, d))
```

```python
import functools
import math

import jax
import jax.numpy as jnp
from jax import lax
from jax.experimental import pallas as pl
from jax.experimental.pallas import tpu as pltpu

D_MODEL = 1024
DEPTH = 2
HA = 8
D_NOPE = 64
D_ROPE = 32
D_V = 64
D_CQ = 384
D_C = 256
ROPE_BASE = 10000.0
MLA_SCALE = (D_NOPE + D_ROPE) ** -0.5
HB = 8
KVB = 2
GQ = HB // KVB
HD = 64
WINDOW = 128
WB = 128
WIN_SCALE = HD ** -0.5
NUM_BUCKETS = 32
MAX_DIST = 128
N_GROUPS = 4
EPG = 8
N_EXPERTS = N_GROUPS * EPG
TOP_K = 2
D_EXPERT = 256
EPS = 1e-6
NEG = -1e30
LOG2E = math.log2(math.e)

LANES = 128
SUBLANES = 8
HALF_ROPE = D_ROPE // 2
MOE_ROWS = 512
MLA_HEADS = 4
MLA_KEYS = 256
DMA_QUEUES = 2
ROWS_PER_ISSUE = 8
VMEM_LIMIT = 48 * 1024 * 1024

C_CQ = 0
C_CKV = C_CQ + D_CQ
C_KR = C_CKV + D_C
C_QB = C_KR + LANES
C_KB = C_QB + HB * HD
N_MIX = C_KB + 2 * KVB * HD

F32 = jnp.float32
BF16 = jnp.bfloat16
U32 = jnp.uint32
D_PACK = D_MODEL // 2
ROW_SLABS = D_PACK // LANES


def _rms(x, g):
    ms = jnp.mean(x * x, axis=-1, keepdims=True)
    return x * lax.rsqrt(ms + EPS) * g


def _dot(a, b):
    return jnp.dot(a, b, preferred_element_type=F32)


def _dot_nt(a, b):
    return lax.dot_general(a, b, (((1,), (1,)), ((), ())), preferred_element_type=F32)


def _pack_rows(x):
    n = x.shape[1] // 2
    lo = lax.bitcast_convert_type(x[:, :n].astype(BF16).astype(F32), U32)
    hi = lax.bitcast_convert_type(x[:, n:].astype(BF16).astype(F32), U32)
    return hi | (lo >> 16)


def _unpack_rows(p):
    lo = lax.bitcast_convert_type(p << 16, F32)
    hi = lax.bitcast_convert_type(p & jnp.uint32(0xFFFF0000), F32)
    return lo, hi


def _load_rows(ref, n):
    return jnp.concatenate([ref[pl.ds(c, n, stride=ROW_SLABS), :] for c in range(ROW_SLABS)], axis=1)


def _store_rows(ref, rows):
    n = rows.shape[0]
    for c in range(ROW_SLABS):
        ref[pl.ds(c, n, stride=ROW_SLABS), :] = rows[:, c * LANES:(c + 1) * LANES]


def _row_at(i):
    return pl.ds(pl.multiple_of(i * ROW_SLABS, ROW_SLABS), ROW_SLABS)


def _x_operands(xs, tile):
    xa, xb = xs
    na = xa.shape[0] // tile
    rows = xa.shape[0] + (0 if xb is None else xb.shape[0])
    specs = (pl.BlockSpec((tile, D_MODEL), lambda i, *_: (jnp.minimum(i, na - 1), 0)),
             pl.BlockSpec((tile, D_MODEL), lambda i, *_: (jnp.maximum(i - na, 0), 0)))
    return (xa, xa if xb is None else xb), specs, na, rows


def _x_tile(xa_ref, xb_ref, na):
    return jnp.where(pl.program_id(0) < na, xa_ref[...], xb_ref[...])


def _rope(t, cos, sin_a, sin_b):
    return (t * cos + pltpu.roll(t, LANES - HALF_ROPE, 1) * sin_a
            + pltpu.roll(t, HALF_ROPE, 1) * sin_b)


def _mixer_in_kernel(xa_ref, xb_ref, lnmix_ref, win_ref, wvbt_ref, lncq_ref, lnckv_ref, wuq_ref,
                     wuk_ref, wuvt_ref, cos_ref, sina_ref, sinb_ref,
                     qa_ref, ka_ref, vat_ref, qb_ref, kb_ref, vbt_ref, *, na):
    h = _rms(_x_tile(xa_ref, xb_ref, na), lnmix_ref[...]).astype(BF16)
    cq = _dot(h, win_ref[:, C_CQ:C_CKV])
    ckv = _dot(h, win_ref[:, C_CKV:C_KR])
    kr = _dot(h, win_ref[:, C_KR:C_QB])
    qb_ref[...] = _dot(h, win_ref[:, C_QB:C_KB]).astype(BF16)
    kb_ref[...] = _dot(h, win_ref[:, C_KB:N_MIX]).astype(BF16)
    vbt = _dot_nt(wvbt_ref[...], h).astype(BF16)
    for c in range(vbt_ref.shape[0]):
        vbt_ref[c] = vbt[:, c * WB:(c + 1) * WB]
    cqn = _rms(cq, lncq_ref[...]).astype(BF16)
    ckvn = _rms(ckv, lnckv_ref[...]).astype(BF16)
    q = _dot(cqn, wuq_ref[...])
    k = _dot(ckvn, wuk_ref[...])
    vat_ref[...] = _dot_nt(wuvt_ref[...], ckvn).astype(BF16)
    cos, sin_a, sin_b = cos_ref[...], sina_ref[...], sinb_ref[...]
    kr_rot = _rope(kr, cos, sin_a, sin_b)
    for hd in range(HA):
        sl = slice(hd * LANES, (hd + 1) * LANES)
        qa_ref[:, sl] = (_rope(q[:, sl], cos, sin_a, sin_b) * (MLA_SCALE * LOG2E)).astype(BF16)
        ka_ref[:, sl] = (k[:, sl] + kr_rot).astype(BF16)


def _mixer_in(xs, p, tabs, seq, tm):
    x_ops, x_specs, na, t = _x_operands(xs, tm)
    nt = t // tm
    spt = seq // tm
    row = lambda i: (i, 0)
    full = lambda i: (0, 0)
    tab = lambda i: (i % spt, 0)
    rows = lambda w: (pl.BlockSpec((tm, w), row), jax.ShapeDtypeStruct((t, w), BF16))
    outs = [
        rows(HA * LANES),
        rows(HA * LANES),
        (pl.BlockSpec((HA * D_V, tm), lambda i: (0, i)),
         jax.ShapeDtypeStruct((HA * D_V, t), BF16)),
        rows(HB * HD),
        rows(2 * KVB * HD),
        (pl.BlockSpec((tm // WB, KVB * HD, WB), lambda i: (i, 0, 0)),
         jax.ShapeDtypeStruct((t // WB, KVB * HD, WB), BF16)),
    ]
    out_specs = [o[0] for o in outs]
    out_shape = [o[1] for o in outs]
    return pl.pallas_call(
        functools.partial(_mixer_in_kernel, na=na),
        grid=(nt,),
        in_specs=[
            *x_specs,
            pl.BlockSpec((1, D_MODEL), full),
            pl.BlockSpec((D_MODEL, N_MIX), full),
            pl.BlockSpec((KVB * HD, D_MODEL), full),
            pl.BlockSpec((1, D_CQ), full),
            pl.BlockSpec((1, D_C), full),
            pl.BlockSpec((D_CQ, HA * LANES), full),
            pl.BlockSpec((D_C, HA * LANES), full),
            pl.BlockSpec((HA * D_V, D_C), full),
            pl.BlockSpec((tm, LANES), tab),
            pl.BlockSpec((tm, LANES), tab),
            pl.BlockSpec((tm, LANES), tab),
        ],
        out_specs=out_specs,
        out_shape=out_shape,
        compiler_params=pltpu.CompilerParams(
            dimension_semantics=("parallel",), vmem_limit_bytes=VMEM_LIMIT),
        name="mixer_in",
    )(*x_ops, p["ln_mix"], p["w_mix"], p["w_vbt"], p["ln_cq"], p["ln_ckv"], p["w_uq"], p["w_uk"], p["w_uvt"],
      tabs["cos"], tabs["sin_a"], tabs["sin_b"])


def _mla_kernel(q_ref, k_ref, vt_ref, o_ref):
    heads = range(MLA_HEADS)
    nch = k_ref.shape[0] // MLA_KEYS

    def scores(j, hh):
        sl = slice(hh * LANES, (hh + 1) * LANES)
        return _dot_nt(k_ref[j * MLA_KEYS:(j + 1) * MLA_KEYS, sl], q_ref[:, sl])

    ones = jnp.ones((2 * SUBLANES, MLA_KEYS), BF16)

    def weighted_values(hh, j, p):
        vt = vt_ref[hh * D_V:(hh + 1) * D_V, j * MLA_KEYS:(j + 1) * MLA_KEYS]
        return _dot(jnp.concatenate([vt, ones], axis=0), p)

    s_next = [scores(0, hh) for hh in heads]
    m = [None] * MLA_HEADS
    acc = [None] * MLA_HEADS
    pending = [None] * MLA_HEADS
    for j in range(nch):
        s_cur = s_next
        if j + 1 < nch:
            s_next = [scores(j + 1, hh) for hh in heads]
        for hh in heads:
            if pending[hh] is not None:
                p_prev, alpha_prev = pending[hh]
                pv = weighted_values(hh, j - 1, p_prev)
                acc[hh] = pv if acc[hh] is None else acc[hh] * alpha_prev + pv
            s = s_cur[hh]
            cmax = jnp.max(s, axis=0, keepdims=True)
            m_new = cmax if m[hh] is None else jnp.maximum(m[hh], cmax)
            alpha = None if m[hh] is None else jnp.exp2(m[hh] - m_new)
            p = jnp.exp2(s - m_new)
            m[hh] = m_new
            pending[hh] = (p.astype(BF16), alpha)
    outs = []
    for hh in heads:
        p_prev, alpha_prev = pending[hh]
        pv = weighted_values(hh, nch - 1, p_prev)
        total = pv if acc[hh] is None else acc[hh] * alpha_prev + pv
        outs.append(total[:D_V] / total[D_V:D_V + 1])
    o_ref[...] = jnp.concatenate(outs, axis=0).T.astype(BF16)


def _mla_attention(qa, ka, va, batch, seq, tq):
    t = qa.shape[0]
    nq = seq // tq
    return pl.pallas_call(
        _mla_kernel,
        grid=(batch, HA // MLA_HEADS, nq),
        in_specs=[
            pl.BlockSpec((tq, MLA_HEADS * LANES), lambda b, j, i: (b * nq + i, j)),
            pl.BlockSpec((seq, MLA_HEADS * LANES), lambda b, j, i: (b, j)),
            pl.BlockSpec((MLA_HEADS * D_V, seq), lambda b, j, i: (j, b)),
        ],
        out_specs=pl.BlockSpec((tq, MLA_HEADS * D_V), lambda b, j, i: (b * nq + i, j)),
        out_shape=jax.ShapeDtypeStruct((t, HA * D_V), BF16),
        compiler_params=pltpu.CompilerParams(
            dimension_semantics=("parallel", "parallel", "parallel"),
            vmem_limit_bytes=VMEM_LIMIT),
        name="mla_attention",
    )(qa, ka, va)


def _window_kernel(q_ref, k_ref, vt_ref, bias_ref, sink_ref, o_ref, *, seq, qblocks):
    nb = seq // WB
    lane = lax.broadcasted_iota(jnp.int32, (WB, LANES), 1)
    lo = lane < HD
    for u in range(qblocks):
        n = pl.program_id(1) * qblocks + u
        rows = slice(u * WB, (u + 1) * WB)
        blocks = (jnp.maximum(n - 1, 0), n, jnp.minimum(n + 1, nb - 1))
        kw = jnp.concatenate([k_ref[pl.ds(pl.multiple_of(b * WB, WB), WB), :] for b in blocks],
                             axis=0)
        vts = [vt_ref[b] for b in blocks]
        edge = jnp.where(n == 0, 0, jnp.where(n == nb - 1, 2, 1))
        heads = []
        for kv in range(KVB):
            kd = kw[:, kv * LANES:(kv + 1) * LANES]
            qs = []
            for jj in range(GQ // 2):
                qp = q_ref[rows, (kv * 2 + jj) * LANES:(kv * 2 + jj + 1) * LANES]
                qs.append(jnp.where(lo, qp, jnp.zeros_like(qp)))
                qs.append(jnp.where(lo, jnp.zeros_like(qp), qp))
            q4 = jnp.concatenate(qs, axis=0)
            st = _dot_nt(kd, q4) + bias_ref[edge, kv]
            sink = sink_ref[kv]
            m = jnp.maximum(jnp.max(st, axis=0, keepdims=True), sink)
            e = jnp.exp2(st - m)
            denom = jnp.sum(e, axis=0, keepdims=True) + jnp.exp2(sink - m)
            vt = jnp.concatenate([v[kv * HD:(kv + 1) * HD, :] for v in vts], axis=1)
            ot = _dot(vt, e.astype(BF16)) / denom
            heads += [ot[:, g * WB:(g + 1) * WB] for g in range(GQ)]
        o_ref[rows, :] = jnp.concatenate(heads, axis=0).T.astype(BF16)


def _window_attention(qb, kb, vbt, bias, sink, batch, seq):
    t = qb.shape[0]
    nb = seq // WB
    qblocks = 8 if nb % 8 == 0 else (4 if nb % 4 == 0 else 1)
    nb //= qblocks
    return pl.pallas_call(
        functools.partial(_window_kernel, seq=seq, qblocks=qblocks),
        grid=(batch, nb),
        in_specs=[
            pl.BlockSpec((qblocks * WB, HB * HD), lambda b, n: (b * nb + n, 0)),
            pl.BlockSpec((seq, 2 * KVB * HD), lambda b, n: (b, 0)),
            pl.BlockSpec((seq // WB, KVB * HD, WB), lambda b, n: (b, 0, 0)),
            pl.BlockSpec((3, KVB, 3 * WB, GQ * WB), lambda b, n: (0, 0, 0, 0)),
            pl.BlockSpec((KVB, 1, GQ * WB), lambda b, n: (0, 0, 0)),
        ],
        out_specs=pl.BlockSpec((qblocks * WB, HB * HD), lambda b, n: (b * nb + n, 0)),
        out_shape=jax.ShapeDtypeStruct((t, HB * HD), BF16),
        compiler_params=pltpu.CompilerParams(
            dimension_semantics=("parallel", "parallel"), vmem_limit_bytes=VMEM_LIMIT),
        name="window_attention",
    )(qb, kb, vbt, bias, sink)


def _mixer_out_kernel(xa_ref, xb_ref, oa_ref, ob_ref, lnmix_ref, wg_ref, woa_ref, wob_ref, wout_ref,
                      lnffn_ref, wrh_ref, br_ref,
                      xn_ref, h2_ref, rf_ref, ri_ref, cnt_ref, carry_ref, *, na):
    i = pl.program_id(0)

    @pl.when(i == 0)
    def _():
        carry_ref[...] = jnp.zeros_like(carry_ref)

    x = _x_tile(xa_ref, xb_ref, na)
    tm = x.shape[0]
    h = _rms(x, lnmix_ref[...]).astype(BF16)
    g = _dot(h, wg_ref[...])
    ya = _dot(oa_ref[...], woa_ref[...])
    yb = _dot(ob_ref[...], wob_ref[...])
    merged = jax.nn.sigmoid(g[:, :D_MODEL]) * ya + jax.nn.sigmoid(g[:, D_MODEL:]) * yb
    xn = x + _dot(merged.astype(BF16), wout_ref[...])
    xn_ref[...] = xn
    h2 = _rms(xn, lnffn_ref[...])
    _store_rows(h2_ref, _pack_rows(h2))

    h2_hi = h2.astype(BF16)
    h2_lo = (h2 - h2_hi.astype(F32)).astype(BF16)
    hi_terms = _dot(h2_hi, wrh_ref[...])
    logits = (hi_terms[:, :LANES] + hi_terms[:, LANES:] + _dot(h2_lo, wrh_ref[:, :LANES])
              + br_ref[...])

    lane = lax.broadcasted_iota(jnp.int32, (tm, LANES), 1)
    lanef = lane.astype(F32)
    big = float(LANES)
    is_g = (lane >= N_EXPERTS) & (lane < N_EXPERTS + N_GROUPS)
    gl = jnp.where(is_g, logits, NEG)
    gmax = jnp.max(gl, axis=1, keepdims=True)
    grp = jnp.min(jnp.where(gl == gmax, lanef, big), axis=1, keepdims=True) - N_EXPERTS
    gsum = jnp.sum(jnp.where(is_g, jnp.exp(gl - gmax), 0.0), axis=1, keepdims=True)
    grp_w = 1.0 / gsum
    is_e = (lanef >= grp * EPG) & (lanef < grp * EPG + EPG)
    el = jnp.where(is_e, logits, NEG)
    v1 = jnp.max(el, axis=1, keepdims=True)
    i1 = jnp.min(jnp.where((el == v1) & is_e, lanef, big), axis=1, keepdims=True)
    a1 = lanef == i1
    el2 = jnp.where(a1, NEG, el)
    v2 = jnp.max(el2, axis=1, keepdims=True)
    i2 = jnp.min(jnp.where((el2 == v2) & is_e & (~a1), lanef, big), axis=1, keepdims=True)
    a2 = lanef == i2
    ex = jnp.exp(v2 - v1)
    p1 = 1.0 / (1.0 + ex)
    gate1 = grp_w * p1
    gate2 = grp_w * (ex * p1)

    a1f = a1.astype(F32)
    a2f = a2.astype(F32)
    cnt = a1f + a2f
    r_i = lax.broadcasted_iota(jnp.int32, (tm, tm), 0)
    c_i = lax.broadcasted_iota(jnp.int32, (tm, tm), 1)
    tri = jnp.where(r_i > c_i, 1.0, 0.0).astype(BF16)
    before = _dot(tri, cnt.astype(BF16)) + carry_ref[...]
    rank1 = jnp.sum(a1f * before, axis=1, keepdims=True)
    rank2 = jnp.sum(a2f * before, axis=1, keepdims=True)
    carry = carry_ref[...] + jnp.sum(cnt, axis=0, keepdims=True)
    carry_ref[...] = carry
    cnt_ref[...] = jnp.broadcast_to(carry, cnt_ref.shape)

    rf_ref[...] = jnp.where(lane == 0, gate1, jnp.where(lane == 1, gate2, 0.0))
    ri = jnp.where(lane == 0, i1, jnp.where(lane == 1, i2,
                                            jnp.where(lane == 2, rank1,
                                                      jnp.where(lane == 3, rank2, 0.0))))
    ri_ref[...] = ri.astype(jnp.int32)


def _mixer_out(xs, oa, ob, p, tm):
    x_ops, x_specs, na, t = _x_operands(xs, tm)
    nt = t // tm
    row = lambda i: (i, 0)
    full = lambda i: (0, 0)
    return pl.pallas_call(
        functools.partial(_mixer_out_kernel, na=na),
        grid=(nt,),
        in_specs=[
            *x_specs,
            pl.BlockSpec((tm, HA * D_V), row),
            pl.BlockSpec((tm, HB * HD), row),
            pl.BlockSpec((1, D_MODEL), full),
            pl.BlockSpec((D_MODEL, 2 * D_MODEL), full),
            pl.BlockSpec((HA * D_V, D_MODEL), full),
            pl.BlockSpec((HB * HD, D_MODEL), full),
            pl.BlockSpec((D_MODEL, D_MODEL), full),
            pl.BlockSpec((1, D_MODEL), full),
            pl.BlockSpec((D_MODEL, 2 * LANES), full),
            pl.BlockSpec((1, LANES), full),
        ],
        out_specs=[
            pl.BlockSpec((tm, D_MODEL), row),
            pl.BlockSpec((ROW_SLABS * tm, LANES), row),
            pl.BlockSpec((tm, LANES), row),
            pl.BlockSpec((tm, LANES), row),
            pl.BlockSpec((8, LANES), full),
        ],
        out_shape=[
            jax.ShapeDtypeStruct((t, D_MODEL), F32),
            jax.ShapeDtypeStruct((ROW_SLABS * t, LANES), U32),
            jax.ShapeDtypeStruct((t, LANES), F32),
            jax.ShapeDtypeStruct((t, LANES), jnp.int32),
            jax.ShapeDtypeStruct((8, LANES), F32),
        ],
        scratch_shapes=[pltpu.VMEM((1, LANES), F32)],
        compiler_params=pltpu.CompilerParams(
            dimension_semantics=("arbitrary",), vmem_limit_bytes=VMEM_LIMIT),
        name="mixer_out",
    )(*x_ops, oa, ob, p["ln_mix"], p["w_g"], p["w_oa"], p["w_ob"], p["w_out"], p["ln_ffn"],
      p["w_r"], p["b_r"])


_ZERO_ROWS = MOE_ROWS // 2
_ZERO_SIZES = tuple(_ZERO_ROWS >> s for s in range((_ZERO_ROWS // SUBLANES).bit_length()))


def _dispatch_kernel(zstart_ref, zlen_ref, nu_ref, dest_ref, h2_ref, xb_ref, zbuf_ref, sem, zsem, *, td):
    i = pl.program_id(0)

    def row_copy(t, d):
        return pltpu.make_async_copy(h2_ref.at[_row_at(t)], xb_ref.at[_row_at(d)], sem)

    def issue(g, c):
        for u in range(ROWS_PER_ISSUE):
            t = g * ROWS_PER_ISSUE + u
            for k in range(TOP_K):
                row_copy(t, dest_ref[0, 0, 2 * t + k]).start(
                    priority=(TOP_K * u + k) % DMA_QUEUES)
        return c

    lax.fori_loop(0, td // ROWS_PER_ISSUE, issue, 0)

    def zero_copy(start, size):
        first = pl.multiple_of(start * ROW_SLABS, ROW_SLABS if size == 1 else SUBLANES * ROW_SLABS)
        return pltpu.make_async_copy(zbuf_ref.at[pl.ds(0, size * ROW_SLABS)],
                                     xb_ref.at[pl.ds(first, size * ROW_SLABS)], zsem)

    def zero_pass(do):
        def per_expert(e, c):
            start = zstart_ref[e]
            head = (-start) & (SUBLANES - 1)
            for j in range(SUBLANES - 1):
                @pl.when(j < head)
                def _(row=start + j):
                    do(zero_copy(row, 1))

            start = start + head
            left = zlen_ref[e] - head
            for size in _ZERO_SIZES:
                take = (left & size) != 0

                @pl.when(take)
                def _(start=start, size=size):
                    do(zero_copy(pl.multiple_of(start, SUBLANES), size))

                start = start + jnp.where(take, size, 0)
            return c

        lax.fori_loop(0, N_EXPERTS, per_expert, 0)

        def unused_block(b, c):
            for part in range(MOE_ROWS // _ZERO_ROWS):
                row = pl.multiple_of(b * MOE_ROWS + part * _ZERO_ROWS, _ZERO_ROWS)
                do(zero_copy(row, _ZERO_ROWS))
            return c

        lax.fori_loop(nu_ref[0], xb_ref.shape[0] // (ROW_SLABS * MOE_ROWS), unused_block, 0)

    @pl.when(i == 0)
    def _():
        zbuf_ref[...] = jnp.zeros_like(zbuf_ref)
        zero_pass(lambda cp: cp.start())
        zero_pass(lambda cp: cp.wait())

    for _ in range(TOP_K):
        pltpu.make_async_copy(h2_ref, xb_ref.at[pl.ds(0, td * ROW_SLABS)], sem).wait()


def _moe_dispatch(h2, dest, zstart, zlen, n_used, n_rows, td):
    t = h2.shape[0] // ROW_SLABS
    nt = t // td
    dest3 = dest.reshape(nt, 1, 2 * td)
    return pl.pallas_call(
        functools.partial(_dispatch_kernel, td=td),
        grid_spec=pltpu.PrefetchScalarGridSpec(
            num_scalar_prefetch=3,
            grid=(nt,),
            in_specs=[
                pl.BlockSpec((1, 1, 2 * td), lambda i, zs, zl, nu: (i, 0, 0),
                             memory_space=pltpu.SMEM),
                pl.BlockSpec((ROW_SLABS * td, LANES), lambda i, zs, zl, nu: (i, 0)),
            ],
            out_specs=pl.BlockSpec(memory_space=pl.ANY),
            scratch_shapes=[pltpu.VMEM((ROW_SLABS * _ZERO_ROWS, LANES), U32),
                            pltpu.SemaphoreType.DMA, pltpu.SemaphoreType.DMA],
        ),
        out_shape=jax.ShapeDtypeStruct((ROW_SLABS * n_rows, LANES), U32),
        compiler_params=pltpu.CompilerParams(dimension_semantics=("arbitrary",)),
        name="moe_dispatch",
    )(zstart, zlen, n_used, dest3, h2)


def _experts_kernel(be_ref, nu_ref, x_ref, wg_ref, wu_ref, wd_ref, y_ref, wgu_s, wd_s):
    b = pl.program_id(0)

    @pl.when((b == 0) | (be_ref[b] != be_ref[jnp.maximum(b - 1, 0)]))
    def _():
        wgu_s[:, :D_EXPERT] = wg_ref[0, 0].astype(BF16)
        wgu_s[:, D_EXPERT:] = wu_ref[0, 0].astype(BF16)
        wd_s[...] = wd_ref[0, 0].astype(BF16)

    @pl.when(b < nu_ref[0])
    def _():
        lo, hi = _unpack_rows(_load_rows(x_ref, MOE_ROWS))
        x = jnp.concatenate([lo.astype(BF16), hi.astype(BF16)], axis=1)
        au = _dot(x, wgu_s[...])
        a, u = au[:, :D_EXPERT], au[:, D_EXPERT:]
        hid = (a * jax.nn.sigmoid(a) * u).astype(BF16)
        _store_rows(y_ref, _pack_rows(_dot(hid, wd_s[...])))

    @pl.when(b >= nu_ref[0])
    def _():
        y_ref[...] = jnp.zeros_like(y_ref)


def _moe_experts(xb, blk_expert, n_used, w_gate, w_up, w_down, layer):
    n_rows = xb.shape[0] // ROW_SLABS
    n_blocks = n_rows // MOE_ROWS
    live = lambda b, be, nu: (jnp.minimum(b, nu[0] - 1), 0)
    expert = lambda b, be, nu: (layer, be[b], 0, 0)
    return pl.pallas_call(
        _experts_kernel,
        grid_spec=pltpu.PrefetchScalarGridSpec(
            num_scalar_prefetch=2,
            grid=(n_blocks,),
            in_specs=[
                pl.BlockSpec((ROW_SLABS * MOE_ROWS, LANES), live),
                pl.BlockSpec((1, 1, D_MODEL, D_EXPERT), expert),
                pl.BlockSpec((1, 1, D_MODEL, D_EXPERT), expert),
                pl.BlockSpec((1, 1, D_EXPERT, D_MODEL), expert),
            ],
            out_specs=pl.BlockSpec((ROW_SLABS * MOE_ROWS, LANES), lambda b, be, nu: (b, 0)),
            scratch_shapes=[pltpu.VMEM((D_MODEL, 2 * D_EXPERT), BF16),
                            pltpu.VMEM((D_EXPERT, D_MODEL), BF16)],
        ),
        out_shape=jax.ShapeDtypeStruct((ROW_SLABS * n_rows, LANES), U32),
        compiler_params=pltpu.CompilerParams(
            dimension_semantics=("arbitrary",), vmem_limit_bytes=VMEM_LIMIT),
        name="moe_experts",
    )(blk_expert, n_used, xb, w_gate, w_up, w_down)


def _combine_kernel(dest_ref, dnext_ref, x_ref, rf_ref, lnf_ref, yb_ref, *rest, tc, split):
    outs, (ybuf_ref, sems) = rest[:-2], rest[-2:]
    i = pl.program_id(0)
    n = pl.num_programs(0)
    slot = i % 2

    def row_copy(src_row, s, k, t):
        return pltpu.make_async_copy(yb_ref.at[_row_at(src_row)],
                                     ybuf_ref.at[s, k, _row_at(t)], sems.at[s])

    def issue(d_ref, s):
        def body(g, c):
            for u in range(ROWS_PER_ISSUE):
                t = g * ROWS_PER_ISSUE + u
                for k in range(TOP_K):
                    row_copy(d_ref[0, 0, 2 * t + k], s, k, t).start(
                        priority=(TOP_K * u + k) % DMA_QUEUES)
            return c

        lax.fori_loop(0, tc // ROWS_PER_ISSUE, body, 0)

    @pl.when(i == 0)
    def _():
        issue(dest_ref, slot)

    @pl.when(i + 1 < n)
    def _():
        issue(dnext_ref, 1 - slot)

    for k in range(TOP_K):
        pltpu.make_async_copy(yb_ref.at[pl.ds(0, tc * ROW_SLABS)], ybuf_ref.at[slot, k],
                              sems.at[slot]).wait()

    rf = rf_ref[...]
    lo0, hi0 = _unpack_rows(_load_rows(ybuf_ref.at[slot, 0], tc))
    lo1, hi1 = _unpack_rows(_load_rows(ybuf_ref.at[slot, 1], tc))
    g0, g1 = rf[:, 0:1], rf[:, 1:2]
    out = x_ref[...] + jnp.concatenate([g0 * lo0 + g1 * lo1, g0 * hi0 + g1 * hi1], axis=1)
    if split is None:
        outs[0][...] = out
    else:
        out = _rms(out, lnf_ref[...])

        @pl.when(i < split)
        def _():
            outs[0][...] = out

        @pl.when(i >= split)
        def _():
            outs[1][...] = out


def _moe_combine(x, rf, dest, yb, ln_final, tc, final_rows):
    t = x.shape[0]
    nt = t // tc
    dest3 = dest.reshape(nt, 1, 2 * tc)
    row = lambda i: (i, 0)
    if final_rows is None:
        split = None
        out_specs = pl.BlockSpec((tc, D_MODEL), row)
        out_shape = jax.ShapeDtypeStruct((t, D_MODEL), F32)
    else:
        split = final_rows[0] // tc
        out_specs = [pl.BlockSpec((tc, D_MODEL), lambda i: (jnp.minimum(i, split - 1), 0)),
                     pl.BlockSpec((tc, D_MODEL), lambda i: (jnp.maximum(i - split, 0), 0))]
        out_shape = [jax.ShapeDtypeStruct((r, D_MODEL), F32) for r in final_rows]
    return pl.pallas_call(
        functools.partial(_combine_kernel, tc=tc, split=split),
        grid=(nt,),
        in_specs=[
            pl.BlockSpec((1, 1, 2 * tc), lambda i: (i, 0, 0), memory_space=pltpu.SMEM),
            pl.BlockSpec((1, 1, 2 * tc), lambda i: (jnp.minimum(i + 1, nt - 1), 0, 0),
                         memory_space=pltpu.SMEM),
            pl.BlockSpec((tc, D_MODEL), row),
            pl.BlockSpec((tc, LANES), row),
            pl.BlockSpec((1, D_MODEL), lambda i: (0, 0)),
            pl.BlockSpec(memory_space=pl.ANY),
        ],
        out_specs=out_specs,
        out_shape=out_shape,
        scratch_shapes=[pltpu.VMEM((2, TOP_K, ROW_SLABS * tc, LANES), U32),
                        pltpu.SemaphoreType.DMA((2,))],
        compiler_params=pltpu.CompilerParams(
            dimension_semantics=("arbitrary",), vmem_limit_bytes=VMEM_LIMIT),
        name="moe_combine",
    )(dest3, dest3, x, rf, ln_final, yb)


def _rope_tables(seq):
    inv = ROPE_BASE ** (-jnp.arange(0, D_ROPE, 2, dtype=F32) / D_ROPE)
    ang = jnp.arange(seq, dtype=F32)[:, None] * inv[None, :]
    cos, sin = jnp.cos(ang), jnp.sin(ang)
    z = lambda w: jnp.zeros((seq, w), F32)
    tail = LANES - D_NOPE - D_ROPE
    return {
        "cos": jnp.concatenate([jnp.ones((seq, D_NOPE), F32), cos, cos, z(tail)], axis=1),
        "sin_a": jnp.concatenate([z(D_NOPE), -sin, z(HALF_ROPE), z(tail)], axis=1),
        "sin_b": jnp.concatenate([z(D_NOPE), z(HALF_ROPE), sin, z(tail)], axis=1),
    }


def _t5_bucket(rel):
    half = NUM_BUCKETS // 2
    max_exact = half // 2
    n = jnp.abs(rel)
    nf = jnp.maximum(n, 1).astype(F32)
    large = max_exact + (jnp.log(nf / max_exact) / math.log(MAX_DIST / max_exact)
                         * (half - max_exact)).astype(jnp.int32)
    large = jnp.minimum(large, half - 1)
    return jnp.where(rel > 0, half, 0) + jnp.where(n < max_exact, n, large)


def _window_bias(rel_bias):
    qi = jnp.arange(WB)[:, None]
    kj = jnp.arange(3 * WB)[None, :]
    rel = kj - WB - qi
    onehot = _t5_bucket(rel)[:, :, None] == jnp.arange(NUM_BUCKETS)
    bias = jnp.sum(jnp.where(onehot[..., None], rel_bias.astype(F32), 0.0), axis=2) * LOG2E
    bias = jnp.where((jnp.abs(rel) <= WINDOW)[:, :, None], bias, NEG)
    bias = jnp.transpose(bias, (2, 0, 1)).reshape(KVB, GQ * WB, 3 * WB)
    bias = jnp.swapaxes(bias, 1, 2)
    key = jnp.arange(3 * WB)[None, :, None]
    return jnp.stack([jnp.where(key < WB, NEG, bias), bias, jnp.where(key >= 2 * WB, NEG, bias)])


def _pack_layer(l, w_in, ln_mix, ln_cq, ln_ckv, w_uq, w_ukv, w_oa, sink, w_ob, w_out, ln_ffn,
                w_gr, b_gr, w_er, b_er):
    wi = w_in[l]
    o = 0
    c_q, o = wi[:, o:o + D_CQ], o + D_CQ
    c_kv, o = wi[:, o:o + D_C], o + D_C
    k_r, o = wi[:, o:o + D_ROPE], o + D_ROPE
    q_b, o = wi[:, o:o + HB * HD], o + HB * HD
    k_b, o = wi[:, o:o + KVB * HD], o + KVB * HD
    v_b, o = wi[:, o:o + KVB * HD], o + KVB * HD
    g_ab = wi[:, o:]
    zc = lambda rows, w: jnp.zeros((rows, w), F32)
    dup = lambda w: jnp.concatenate([w[:, kv * HD:(kv + 1) * HD] for kv in range(KVB) for _ in (0, 1)],
                                    axis=1)
    kr_blk = jnp.concatenate([zc(D_MODEL, D_NOPE), k_r, zc(D_MODEL, LANES - D_NOPE - D_ROPE)], axis=1)
    w_mix = jnp.concatenate([c_q, c_kv, kr_blk, q_b * (WIN_SCALE * LOG2E), dup(k_b)], axis=1)

    uq = w_uq[l].reshape(D_CQ, HA, D_NOPE + D_ROPE)
    uq = jnp.pad(uq, ((0, 0), (0, 0), (0, LANES - D_NOPE - D_ROPE))).reshape(D_CQ, HA * LANES)
    ukv = w_ukv[l].reshape(D_C, HA, D_NOPE + D_V)
    uk = jnp.pad(ukv[:, :, :D_NOPE], ((0, 0), (0, 0), (0, LANES - D_NOPE))).reshape(D_C, HA * LANES)
    uv = ukv[:, :, D_NOPE:].reshape(D_C, HA * D_V)

    w_r = jnp.concatenate([w_er[l], w_gr[l], zc(D_MODEL, LANES - N_EXPERTS - N_GROUPS)], axis=1)
    w_r_hi = w_r.astype(BF16)
    b_r = jnp.concatenate([b_er[l], b_gr[l], jnp.zeros((LANES - N_EXPERTS - N_GROUPS,), F32)])
    sink_row = jnp.repeat(sink[l].astype(F32).reshape(KVB, GQ) * LOG2E, WB, axis=1)[:, None, :]
    return {
        "ln_mix": ln_mix[l][None, :], "w_mix": w_mix.astype(BF16), "w_vbt": v_b.T.astype(BF16),
        "ln_cq": ln_cq[l][None, :], "ln_ckv": ln_ckv[l][None, :],
        "w_uq": uq.astype(BF16), "w_uk": uk.astype(BF16), "w_uvt": uv.T.astype(BF16),
        "w_g": g_ab.astype(BF16), "w_oa": w_oa[l].astype(BF16), "w_ob": w_ob[l].astype(BF16),
        "w_out": w_out[l].astype(BF16), "ln_ffn": ln_ffn[l][None, :],
        "w_r": jnp.concatenate([w_r_hi, (w_r - w_r_hi.astype(F32)).astype(BF16)], axis=1),
        "b_r": b_r[None, :],
        "sink": sink_row,
    }


def _row_tile(n, pref):
    tile = pref
    while n % tile:
        tile //= 2
    return tile


def _trunk(xs, batch, seq, tabs, bias, layers, expert_w, ln_final):
    final_rows = tuple(x.shape[0] for x in xs)
    t = sum(final_rows)
    tm = _row_tile(seq, 512)
    tq = _row_tile(seq, 512)
    td = _row_tile(t, 1024)
    tc = _row_tile(t, 512)
    n_blocks = (t * TOP_K) // MOE_ROWS + N_EXPERTS
    n_rows = n_blocks * MOE_ROWS
    for l, p in enumerate(layers):
        qa, ka, vat, qb, kb, vbt = _mixer_in(xs, p, tabs, seq, tm)
        oa = _mla_attention(qa, ka, vat, batch, seq, tq)
        ob = _window_attention(qb, kb, vbt, bias, p["sink"], batch, seq)
        x, h2, rf, ri, cnt = _mixer_out(xs, oa, ob, p, tm)
        counts = cnt[0, :N_EXPERTS].astype(jnp.int32)
        padded = ((counts + MOE_ROWS - 1) // MOE_ROWS) * MOE_ROWS
        pad_end = jnp.cumsum(padded)
        pad_start = pad_end - padded
        onehot = ri[:, 0:TOP_K, None] == jnp.arange(N_EXPERTS, dtype=jnp.int32)
        dest = (jnp.sum(jnp.where(onehot, pad_start, 0), axis=-1) + ri[:, TOP_K:2 * TOP_K]).reshape(-1)
        blk_row = jnp.arange(n_blocks, dtype=jnp.int32)[:, None] * MOE_ROWS
        blk_expert = jnp.minimum(jnp.sum((pad_end[None, :] <= blk_row).astype(jnp.int32), axis=1),
                                 N_EXPERTS - 1)
        n_used = (pad_end[-1:] // MOE_ROWS).astype(jnp.int32)
        xb = _moe_dispatch(h2, dest, (pad_start + counts).astype(jnp.int32),
                           (padded - counts).astype(jnp.int32), n_used, n_rows, td)
        yb = _moe_experts(xb, blk_expert, n_used, *expert_w, l)
        last = l == len(layers) - 1
        x = _moe_combine(x, rf, dest, yb, ln_final, tc, final_rows if last else None)
        xs = (x, None)
    return x


def kernel(x_prompt, x_sample, rel_bias, ln_mix, w_in, ln_cq, ln_ckv, w_uq, w_ukv, w_oa, sink, w_ob,
           w_out, ln_ffn, w_gr, b_gr, w_er, b_er, w_gate, w_up, w_down, ln_final):
    bp, seq, d = x_prompt.shape
    bs = x_sample.shape[0]
    assert x_sample.shape[1] == seq and seq % WB == 0 and seq >= 3 * WB
    tabs = _rope_tables(seq)
    bias = _window_bias(rel_bias)
    layers = [_pack_layer(l, w_in, ln_mix, ln_cq, ln_ckv, w_uq, w_ukv, w_oa, sink, w_ob, w_out,
                          ln_ffn, w_gr, b_gr, w_er, b_er)
              for l in range(w_in.shape[0])]
    xs = (x_prompt.reshape(bp * seq, d), x_sample.reshape(bs * seq, d))
    y_prompt, y_sample = _trunk(xs, bp + bs, seq, tabs, bias, layers, (w_gate, w_up, w_down),
                                ln_final[None, :])
    return (y_prompt.reshape(bp, seq, d), y_sample.reshape(bs, seq, d))
```

```python
import functools
import math

import jax
import jax.numpy as jnp
from jax import lax
from jax.experimental import pallas as pl
from jax.experimental.pallas import tpu as pltpu

D_MODEL = 1024
DEPTH = 2
HA = 8
D_NOPE = 64
D_ROPE = 32
D_V = 64
D_CQ = 384
D_C = 256
ROPE_BASE = 10000.0
MLA_SCALE = (D_NOPE + D_ROPE) ** -0.5
HB = 8
KVB = 2
GQ = HB // KVB
HD = 64
WINDOW = 128
WB = 128
WIN_SCALE = HD ** -0.5
NUM_BUCKETS = 32
MAX_DIST = 128
N_GROUPS = 4
EPG = 8
N_EXPERTS = N_GROUPS * EPG
TOP_K = 2
D_EXPERT = 256
EPS = 1e-6
NEG = -1e30
LOG2E = math.log2(math.e)

LANES = 128
SUBLANES = 8
HALF_ROPE = D_ROPE // 2
MOE_ROWS = 512
MLA_HEADS = 4
MLA_KEYS = 256
DMA_QUEUES = 2
ROWS_PER_ISSUE = 8
VMEM_LIMIT = 48 * 1024 * 1024

C_CQ = 0
C_CKV = C_CQ + D_CQ
C_KR = C_CKV + D_C
C_QB = C_KR + LANES
C_KB = C_QB + HB * HD
N_MIX = C_KB + 2 * KVB * HD

F32 = jnp.float32
BF16 = jnp.bfloat16
U32 = jnp.uint32
D_PACK = D_MODEL // 2
ROW_SLABS = D_PACK // LANES


def _rms(x, g):
    ms = jnp.mean(x * x, axis=-1, keepdims=True)
    return x * lax.rsqrt(ms + EPS) * g


def _dot(a, b):
    return jnp.dot(a, b, preferred_element_type=F32)


def _dot_nt(a, b):
    return lax.dot_general(a, b, (((1,), (1,)), ((), ())), preferred_element_type=F32)


def _pack_rows(x):
    n = x.shape[1] // 2
    lo = lax.bitcast_convert_type(x[:, :n].astype(BF16).astype(F32), U32)
    hi = lax.bitcast_convert_type(x[:, n:].astype(BF16).astype(F32), U32)
    return hi | (lo >> 16)


def _unpack_rows(p):
    lo = lax.bitcast_convert_type(p << 16, F32)
    hi = lax.bitcast_convert_type(p & jnp.uint32(0xFFFF0000), F32)
    return lo, hi


def _load_rows(ref, n):
    return jnp.concatenate([ref[pl.ds(c, n, stride=ROW_SLABS), :] for c in range(ROW_SLABS)], axis=1)


def _store_rows(ref, rows):
    n = rows.shape[0]
    for c in range(ROW_SLABS):
        ref[pl.ds(c, n, stride=ROW_SLABS), :] = rows[:, c * LANES:(c + 1) * LANES]


def _row_at(i):
    return pl.ds(pl.multiple_of(i * ROW_SLABS, ROW_SLABS), ROW_SLABS)


def _x_operands(xs, tile):
    xa, xb = xs
    na = xa.shape[0] // tile
    rows = xa.shape[0] + (0 if xb is None else xb.shape[0])
    specs = (pl.BlockSpec((tile, D_MODEL), lambda i, *_: (jnp.minimum(i, na - 1), 0)),
             pl.BlockSpec((tile, D_MODEL), lambda i, *_: (jnp.maximum(i - na, 0), 0)))
    return (xa, xa if xb is None else xb), specs, na, rows


def _x_tile(xa_ref, xb_ref, na):
    return jnp.where(pl.program_id(0) < na, xa_ref[...], xb_ref[...])


def _rope(t, cos, sin_a, sin_b):
    return (t * cos + pltpu.roll(t, LANES - HALF_ROPE, 1) * sin_a
            + pltpu.roll(t, HALF_ROPE, 1) * sin_b)


def _mixer_in_kernel(xa_ref, xb_ref, lnmix_ref, win_ref, wvbt_ref, lncq_ref, lnckv_ref, wuq_ref,
                     wuk_ref, wuvt_ref, cos_ref, sina_ref, sinb_ref,
                     qa_ref, ka_ref, vat_ref, qb_ref, kb_ref, vbt_ref, *, na):
    h = _rms(_x_tile(xa_ref, xb_ref, na), lnmix_ref[...]).astype(BF16)
    cq = _dot(h, win_ref[:, C_CQ:C_CKV])
    ckv = _dot(h, win_ref[:, C_CKV:C_KR])
    kr = _dot(h, win_ref[:, C_KR:C_QB])
    qb_ref[...] = _dot(h, win_ref[:, C_QB:C_KB]).astype(BF16)
    kb_ref[...] = _dot(h, win_ref[:, C_KB:N_MIX]).astype(BF16)
    vbt = _dot_nt(wvbt_ref[...], h).astype(BF16)
    for c in range(vbt_ref.shape[0]):
        vbt_ref[c] = vbt[:, c * WB:(c + 1) * WB]
    cqn = _rms(cq, lncq_ref[...]).astype(BF16)
    ckvn = _rms(ckv, lnckv_ref[...]).astype(BF16)
    q = _dot(cqn, wuq_ref[...])
    k = _dot(ckvn, wuk_ref[...])
    vat_ref[...] = _dot_nt(wuvt_ref[...], ckvn).astype(BF16)
    cos, sin_a, sin_b = cos_ref[...], sina_ref[...], sinb_ref[...]
    kr_rot = _rope(kr, cos, sin_a, sin_b)
    for hd in range(HA):
        sl = slice(hd * LANES, (hd + 1) * LANES)
        qa_ref[:, sl] = (_rope(q[:, sl], cos, sin_a, sin_b) * (MLA_SCALE * LOG2E)).astype(BF16)
        ka_ref[:, sl] = (k[:, sl] + kr_rot).astype(BF16)


def _mixer_in(xs, p, tabs, seq, tm):
    x_ops, x_specs, na, t = _x_operands(xs, tm)
    nt = t // tm
    spt = seq // tm
    row = lambda i: (i, 0)
    full = lambda i: (0, 0)
    tab = lambda i: (i % spt, 0)
    rows = lambda w: (pl.BlockSpec((tm, w), row), jax.ShapeDtypeStruct((t, w), BF16))
    outs = [
        rows(HA * LANES),
        rows(HA * LANES),
        (pl.BlockSpec((HA * D_V, tm), lambda i: (0, i)),
         jax.ShapeDtypeStruct((HA * D_V, t), BF16)),
        rows(HB * HD),
        rows(2 * KVB * HD),
        (pl.BlockSpec((tm // WB, KVB * HD, WB), lambda i: (i, 0, 0)),
         jax.ShapeDtypeStruct((t // WB, KVB * HD, WB), BF16)),
    ]
    out_specs = [o[0] for o in outs]
    out_shape = [o[1] for o in outs]
    return pl.pallas_call(
        functools.partial(_mixer_in_kernel, na=na),
        grid=(nt,),
        in_specs=[
            *x_specs,
            pl.BlockSpec((1, D_MODEL), full),
            pl.BlockSpec((D_MODEL, N_MIX), full),
            pl.BlockSpec((KVB * HD, D_MODEL), full),
            pl.BlockSpec((1, D_CQ), full),
            pl.BlockSpec((1, D_C), full),
            pl.BlockSpec((D_CQ, HA * LANES), full),
            pl.BlockSpec((D_C, HA * LANES), full),
            pl.BlockSpec((HA * D_V, D_C), full),
            pl.BlockSpec((tm, LANES), tab),
            pl.BlockSpec((tm, LANES), tab),
            pl.BlockSpec((tm, LANES), tab),
        ],
        out_specs=out_specs,
        out_shape=out_shape,
        compiler_params=pltpu.CompilerParams(
            dimension_semantics=("parallel",), vmem_limit_bytes=VMEM_LIMIT),
        name="mixer_in",
    )(*x_ops, p["ln_mix"], p["w_mix"], p["w_vbt"], p["ln_cq"], p["ln_ckv"], p["w_uq"], p["w_uk"], p["w_uvt"],
      tabs["cos"], tabs["sin_a"], tabs["sin_b"])


def _mla_kernel(q_ref, k_ref, vt_ref, o_ref):
    heads = range(MLA_HEADS)
    nch = k_ref.shape[0] // MLA_KEYS

    def scores(j, hh):
        sl = slice(hh * LANES, (hh + 1) * LANES)
        return _dot_nt(k_ref[j * MLA_KEYS:(j + 1) * MLA_KEYS, sl], q_ref[:, sl])

    ones = jnp.ones((2 * SUBLANES, MLA_KEYS), BF16)

    def weighted_values(hh, j, p):
        vt = vt_ref[hh * D_V:(hh + 1) * D_V, j * MLA_KEYS:(j + 1) * MLA_KEYS]
        return _dot(jnp.concatenate([vt, ones], axis=0), p)

    s_next = [scores(0, hh) for hh in heads]
    m = [None] * MLA_HEADS
    acc = [None] * MLA_HEADS
    pending = [None] * MLA_HEADS
    for j in range(nch):
        s_cur = s_next
        if j + 1 < nch:
            s_next = [scores(j + 1, hh) for hh in heads]
        for hh in heads:
            if pending[hh] is not None:
                p_prev, alpha_prev = pending[hh]
                pv = weighted_values(hh, j - 1, p_prev)
                acc[hh] = pv if acc[hh] is None else acc[hh] * alpha_prev + pv
            s = s_cur[hh]
            cmax = jnp.max(s, axis=0, keepdims=True)
            m_new = cmax if m[hh] is None else jnp.maximum(m[hh], cmax)
            alpha = None if m[hh] is None else jnp.exp2(m[hh] - m_new)
            p = jnp.exp2(s - m_new)
            m[hh] = m_new
            pending[hh] = (p.astype(BF16), alpha)
    outs = []
    for hh in heads:
        p_prev, alpha_prev = pending[hh]
        pv = weighted_values(hh, nch - 1, p_prev)
        total = pv if acc[hh] is None else acc[hh] * alpha_prev + pv
        outs.append(total[:D_V] / total[D_V:D_V + 1])
    o_ref[...] = jnp.concatenate(outs, axis=0).T.astype(BF16)


def _mla_attention(qa, ka, va, batch, seq, tq):
    t = qa.shape[0]
    nq = seq // tq
    return pl.pallas_call(
        _mla_kernel,
        grid=(batch, HA // MLA_HEADS, nq),
        in_specs=[
            pl.BlockSpec((tq, MLA_HEADS * LANES), lambda b, j, i: (b * nq + i, j)),
            pl.BlockSpec((seq, MLA_HEADS * LANES), lambda b, j, i: (b, j)),
            pl.BlockSpec((MLA_HEADS * D_V, seq), lambda b, j, i: (j, b)),
        ],
        out_specs=pl.BlockSpec((tq, MLA_HEADS * D_V), lambda b, j, i: (b * nq + i, j)),
        out_shape=jax.ShapeDtypeStruct((t, HA * D_V), BF16),
        compiler_params=pltpu.CompilerParams(
            dimension_semantics=("parallel", "parallel", "parallel"),
            vmem_limit_bytes=VMEM_LIMIT),
        name="mla_attention",
    )(qa, ka, va)


def _window_kernel(q_ref, k_ref, vt_ref, bias_ref, sink_ref, o_ref, *, seq, qblocks):
    nb = seq // WB
    lane = lax.broadcasted_iota(jnp.int32, (WB, LANES), 1)
    lo = lane < HD
    chains = [(u, kv) for u in range(qblocks) for kv in range(KVB)]

    def key_blocks(u):
        n = pl.program_id(1) * qblocks + u
        return n, (jnp.maximum(n - 1, 0), n, jnp.minimum(n + 1, nb - 1))

    def scores(u, kv):
        n, blocks = key_blocks(u)
        kd = jnp.concatenate([k_ref[pl.ds(pl.multiple_of(b * WB, WB), WB), kv * LANES:(kv + 1) * LANES]
                              for b in blocks], axis=0)
        qs = []
        for jj in range(GQ // 2):
            qp = q_ref[u * WB:(u + 1) * WB, (kv * 2 + jj) * LANES:(kv * 2 + jj + 1) * LANES]
            qs.append(jnp.where(lo, qp, jnp.zeros_like(qp)))
            qs.append(jnp.where(lo, jnp.zeros_like(qp), qp))
        q4 = jnp.concatenate(qs, axis=0)
        edge = jnp.where(n == 0, 0, jnp.where(n == nb - 1, 2, 1))
        return _dot_nt(kd, q4) + bias_ref[edge, kv]

    def probabilities(st, kv):
        sink = sink_ref[kv]
        m = jnp.maximum(jnp.max(st, axis=0, keepdims=True), sink)
        e = jnp.exp2(st - m)
        return e.astype(BF16), jnp.sum(e, axis=0, keepdims=True) + jnp.exp2(sink - m)

    def weighted_values(u, kv, e, denom):
        _, blocks = key_blocks(u)
        vt = jnp.concatenate([vt_ref[b][kv * HD:(kv + 1) * HD, :] for b in blocks], axis=1)
        ot = _dot(vt, e) / denom
        return [ot[:, g * WB:(g + 1) * WB] for g in range(GQ)]

    heads = {}
    pending = None
    s_next = scores(*chains[0])
    for c, (u, kv) in enumerate(chains):
        s_cur = s_next
        if c + 1 < len(chains):
            s_next = scores(*chains[c + 1])
        if pending is not None:
            heads[pending[:2]] = weighted_values(*pending)
        pending = (u, kv, *probabilities(s_cur, kv))
    heads[pending[:2]] = weighted_values(*pending)
    for u in range(qblocks):
        out = jnp.concatenate([h for kv in range(KVB) for h in heads[u, kv]], axis=0)
        o_ref[u * WB:(u + 1) * WB, :] = out.T.astype(BF16)


def _window_attention(qb, kb, vbt, bias, sink, batch, seq):
    t = qb.shape[0]
    nb = seq // WB
    qblocks = 8 if nb % 8 == 0 else (4 if nb % 4 == 0 else 1)
    nb //= qblocks
    return pl.pallas_call(
        functools.partial(_window_kernel, seq=seq, qblocks=qblocks),
        grid=(batch, nb),
        in_specs=[
            pl.BlockSpec((qblocks * WB, HB * HD), lambda b, n: (b * nb + n, 0)),
            pl.BlockSpec((seq, 2 * KVB * HD), lambda b, n: (b, 0)),
            pl.BlockSpec((seq // WB, KVB * HD, WB), lambda b, n: (b, 0, 0)),
            pl.BlockSpec((3, KVB, 3 * WB, GQ * WB), lambda b, n: (0, 0, 0, 0)),
            pl.BlockSpec((KVB, 1, GQ * WB), lambda b, n: (0, 0, 0)),
        ],
        out_specs=pl.BlockSpec((qblocks * WB, HB * HD), lambda b, n: (b * nb + n, 0)),
        out_shape=jax.ShapeDtypeStruct((t, HB * HD), BF16),
        compiler_params=pltpu.CompilerParams(
            dimension_semantics=("parallel", "parallel"), vmem_limit_bytes=VMEM_LIMIT),
        name="window_attention",
    )(qb, kb, vbt, bias, sink)


def _mixer_out_kernel(xa_ref, xb_ref, oa_ref, ob_ref, lnmix_ref, wg_ref, woa_ref, wob_ref, wout_ref,
                      lnffn_ref, wrh_ref, br_ref,
                      xn_ref, h2_ref, rf_ref, ri_ref, cnt_ref, carry_ref, *, na):
    i = pl.program_id(0)

    @pl.when(i == 0)
    def _():
        carry_ref[...] = jnp.zeros_like(carry_ref)

    x = _x_tile(xa_ref, xb_ref, na)
    tm = x.shape[0]
    h = _rms(x, lnmix_ref[...]).astype(BF16)
    g = _dot(h, wg_ref[...])
    ya = _dot(oa_ref[...], woa_ref[...])
    yb = _dot(ob_ref[...], wob_ref[...])
    merged = jax.nn.sigmoid(g[:, :D_MODEL]) * ya + jax.nn.sigmoid(g[:, D_MODEL:]) * yb
    xn = x + _dot(merged.astype(BF16), wout_ref[...])
    xn_ref[...] = xn
    h2 = _rms(xn, lnffn_ref[...])
    _store_rows(h2_ref, _pack_rows(h2))

    h2_hi = h2.astype(BF16)
    h2_lo = (h2 - h2_hi.astype(F32)).astype(BF16)
    hi_terms = _dot(h2_hi, wrh_ref[...])
    logits = (hi_terms[:, :LANES] + hi_terms[:, LANES:] + _dot(h2_lo, wrh_ref[:, :LANES])
              + br_ref[...])

    lane = lax.broadcasted_iota(jnp.int32, (tm, LANES), 1)
    lanef = lane.astype(F32)
    big = float(LANES)
    is_g = (lane >= N_EXPERTS) & (lane < N_EXPERTS + N_GROUPS)
    gl = jnp.where(is_g, logits, NEG)
    gmax = jnp.max(gl, axis=1, keepdims=True)
    grp = jnp.min(jnp.where(gl == gmax, lanef, big), axis=1, keepdims=True) - N_EXPERTS
    gsum = jnp.sum(jnp.where(is_g, jnp.exp(gl - gmax), 0.0), axis=1, keepdims=True)
    grp_w = 1.0 / gsum
    is_e = (lanef >= grp * EPG) & (lanef < grp * EPG + EPG)
    el = jnp.where(is_e, logits, NEG)
    v1 = jnp.max(el, axis=1, keepdims=True)
    i1 = jnp.min(jnp.where((el == v1) & is_e, lanef, big), axis=1, keepdims=True)
    a1 = lanef == i1
    el2 = jnp.where(a1, NEG, el)
    v2 = jnp.max(el2, axis=1, keepdims=True)
    i2 = jnp.min(jnp.where((el2 == v2) & is_e & (~a1), lanef, big), axis=1, keepdims=True)
    a2 = lanef == i2
    ex = jnp.exp(v2 - v1)
    p1 = 1.0 / (1.0 + ex)
    gate1 = grp_w * p1
    gate2 = grp_w * (ex * p1)

    a1f = a1.astype(F32)
    a2f = a2.astype(F32)
    cnt = a1f + a2f
    r_i = lax.broadcasted_iota(jnp.int32, (tm, tm), 0)
    c_i = lax.broadcasted_iota(jnp.int32, (tm, tm), 1)
    tri = jnp.where(r_i > c_i, 1.0, 0.0).astype(BF16)
    before = _dot(tri, cnt.astype(BF16)) + carry_ref[...]
    rank1 = jnp.sum(a1f * before, axis=1, keepdims=True)
    rank2 = jnp.sum(a2f * before, axis=1, keepdims=True)
    carry = carry_ref[...] + jnp.sum(cnt, axis=0, keepdims=True)
    carry_ref[...] = carry
    cnt_ref[...] = jnp.broadcast_to(carry, cnt_ref.shape)

    rf_ref[...] = jnp.where(lane == 0, gate1, jnp.where(lane == 1, gate2, 0.0))
    ri = jnp.where(lane == 0, i1, jnp.where(lane == 1, i2,
                                            jnp.where(lane == 2, rank1,
                                                      jnp.where(lane == 3, rank2, 0.0))))
    ri_ref[...] = ri.astype(jnp.int32)


def _mixer_out(xs, oa, ob, p, tm):
    x_ops, x_specs, na, t = _x_operands(xs, tm)
    nt = t // tm
    row = lambda i: (i, 0)
    full = lambda i: (0, 0)
    return pl.pallas_call(
        functools.partial(_mixer_out_kernel, na=na),
        grid=(nt,),
        in_specs=[
            *x_specs,
            pl.BlockSpec((tm, HA * D_V), row),
            pl.BlockSpec((tm, HB * HD), row),
            pl.BlockSpec((1, D_MODEL), full),
            pl.BlockSpec((D_MODEL, 2 * D_MODEL), full),
            pl.BlockSpec((HA * D_V, D_MODEL), full),
            pl.BlockSpec((HB * HD, D_MODEL), full),
            pl.BlockSpec((D_MODEL, D_MODEL), full),
            pl.BlockSpec((1, D_MODEL), full),
            pl.BlockSpec((D_MODEL, 2 * LANES), full),
            pl.BlockSpec((1, LANES), full),
        ],
        out_specs=[
            pl.BlockSpec((tm, D_MODEL), row),
            pl.BlockSpec((ROW_SLABS * tm, LANES), row),
            pl.BlockSpec((tm, LANES), row),
            pl.BlockSpec((tm, LANES), row),
            pl.BlockSpec((8, LANES), full),
        ],
        out_shape=[
            jax.ShapeDtypeStruct((t, D_MODEL), F32),
            jax.ShapeDtypeStruct((ROW_SLABS * t, LANES), U32),
            jax.ShapeDtypeStruct((t, LANES), F32),
            jax.ShapeDtypeStruct((t, LANES), jnp.int32),
            jax.ShapeDtypeStruct((8, LANES), F32),
        ],
        scratch_shapes=[pltpu.VMEM((1, LANES), F32)],
        compiler_params=pltpu.CompilerParams(
            dimension_semantics=("arbitrary",), vmem_limit_bytes=VMEM_LIMIT),
        name="mixer_out",
    )(*x_ops, oa, ob, p["ln_mix"], p["w_g"], p["w_oa"], p["w_ob"], p["w_out"], p["ln_ffn"],
      p["w_r"], p["b_r"])


_ZERO_ROWS = MOE_ROWS // 2
_ZERO_SIZES = tuple(_ZERO_ROWS >> s for s in range((_ZERO_ROWS // SUBLANES).bit_length()))


def _dispatch_kernel(zstart_ref, zlen_ref, nu_ref, dest_ref, h2_ref, xb_ref, zbuf_ref, sem, zsem, *, td):
    i = pl.program_id(0)

    def row_copy(t, d):
        return pltpu.make_async_copy(h2_ref.at[_row_at(t)], xb_ref.at[_row_at(d)], sem)

    def issue(g, c):
        for u in range(ROWS_PER_ISSUE):
            t = g * ROWS_PER_ISSUE + u
            for k in range(TOP_K):
                row_copy(t, dest_ref[0, 0, 2 * t + k]).start(
                    priority=(TOP_K * u + k) % DMA_QUEUES)
        return c

    lax.fori_loop(0, td // ROWS_PER_ISSUE, issue, 0)

    def zero_copy(start, size):
        first = pl.multiple_of(start * ROW_SLABS, ROW_SLABS if size == 1 else SUBLANES * ROW_SLABS)
        return pltpu.make_async_copy(zbuf_ref.at[pl.ds(0, size * ROW_SLABS)],
                                     xb_ref.at[pl.ds(first, size * ROW_SLABS)], zsem)

    def zero_pass(do):
        def per_expert(e, c):
            start = zstart_ref[e]
            head = (-start) & (SUBLANES - 1)
            for j in range(SUBLANES - 1):
                @pl.when(j < head)
                def _(row=start + j):
                    do(zero_copy(row, 1))

            start = start + head
            left = zlen_ref[e] - head
            for size in _ZERO_SIZES:
                take = (left & size) != 0

                @pl.when(take)
                def _(start=start, size=size):
                    do(zero_copy(pl.multiple_of(start, SUBLANES), size))

                start = start + jnp.where(take, size, 0)
            return c

        lax.fori_loop(0, N_EXPERTS, per_expert, 0)

        def unused_block(b, c):
            for part in range(MOE_ROWS // _ZERO_ROWS):
                row = pl.multiple_of(b * MOE_ROWS + part * _ZERO_ROWS, _ZERO_ROWS)
                do(zero_copy(row, _ZERO_ROWS))
            return c

        lax.fori_loop(nu_ref[0], xb_ref.shape[0] // (ROW_SLABS * MOE_ROWS), unused_block, 0)

    @pl.when(i == 0)
    def _():
        zbuf_ref[...] = jnp.zeros_like(zbuf_ref)
        zero_pass(lambda cp: cp.start())
        zero_pass(lambda cp: cp.wait())

    for _ in range(TOP_K):
        pltpu.make_async_copy(h2_ref, xb_ref.at[pl.ds(0, td * ROW_SLABS)], sem).wait()


def _moe_dispatch(h2, dest, zstart, zlen, n_used, n_rows, td):
    t = h2.shape[0] // ROW_SLABS
    nt = t // td
    dest3 = dest.reshape(nt, 1, 2 * td)
    return pl.pallas_call(
        functools.partial(_dispatch_kernel, td=td),
        grid_spec=pltpu.PrefetchScalarGridSpec(
            num_scalar_prefetch=3,
            grid=(nt,),
            in_specs=[
                pl.BlockSpec((1, 1, 2 * td), lambda i, zs, zl, nu: (i, 0, 0),
                             memory_space=pltpu.SMEM),
                pl.BlockSpec((ROW_SLABS * td, LANES), lambda i, zs, zl, nu: (i, 0)),
            ],
            out_specs=pl.BlockSpec(memory_space=pl.ANY),
            scratch_shapes=[pltpu.VMEM((ROW_SLABS * _ZERO_ROWS, LANES), U32),
                            pltpu.SemaphoreType.DMA, pltpu.SemaphoreType.DMA],
        ),
        out_shape=jax.ShapeDtypeStruct((ROW_SLABS * n_rows, LANES), U32),
        compiler_params=pltpu.CompilerParams(dimension_semantics=("arbitrary",)),
        name="moe_dispatch",
    )(zstart, zlen, n_used, dest3, h2)


def _experts_kernel(be_ref, nu_ref, x_ref, wg_ref, wu_ref, wd_ref, y_ref, wgu_s, wd_s):
    b = pl.program_id(0)

    @pl.when((b == 0) | (be_ref[b] != be_ref[jnp.maximum(b - 1, 0)]))
    def _():
        wgu_s[:, :D_EXPERT] = wg_ref[0, 0].astype(BF16)
        wgu_s[:, D_EXPERT:] = wu_ref[0, 0].astype(BF16)
        wd_s[...] = wd_ref[0, 0].astype(BF16)

    @pl.when(b < nu_ref[0])
    def _():
        lo, hi = _unpack_rows(_load_rows(x_ref, MOE_ROWS))
        x = jnp.concatenate([lo.astype(BF16), hi.astype(BF16)], axis=1)
        au = _dot(x, wgu_s[...])
        a, u = au[:, :D_EXPERT], au[:, D_EXPERT:]
        hid = (a * jax.nn.sigmoid(a) * u).astype(BF16)
        _store_rows(y_ref, _pack_rows(_dot(hid, wd_s[...])))

    @pl.when(b >= nu_ref[0])
    def _():
        y_ref[...] = jnp.zeros_like(y_ref)


def _moe_experts(xb, blk_expert, n_used, w_gate, w_up, w_down, layer):
    n_rows = xb.shape[0] // ROW_SLABS
    n_blocks = n_rows // MOE_ROWS
    live = lambda b, be, nu: (jnp.minimum(b, nu[0] - 1), 0)
    expert = lambda b, be, nu: (layer, be[b], 0, 0)
    return pl.pallas_call(
        _experts_kernel,
        grid_spec=pltpu.PrefetchScalarGridSpec(
            num_scalar_prefetch=2,
            grid=(n_blocks,),
            in_specs=[
                pl.BlockSpec((ROW_SLABS * MOE_ROWS, LANES), live),
                pl.BlockSpec((1, 1, D_MODEL, D_EXPERT), expert),
                pl.BlockSpec((1, 1, D_MODEL, D_EXPERT), expert),
                pl.BlockSpec((1, 1, D_EXPERT, D_MODEL), expert),
            ],
            out_specs=pl.BlockSpec((ROW_SLABS * MOE_ROWS, LANES), lambda b, be, nu: (b, 0)),
            scratch_shapes=[pltpu.VMEM((D_MODEL, 2 * D_EXPERT), BF16),
                            pltpu.VMEM((D_EXPERT, D_MODEL), BF16)],
        ),
        out_shape=jax.ShapeDtypeStruct((ROW_SLABS * n_rows, LANES), U32),
        compiler_params=pltpu.CompilerParams(
            dimension_semantics=("arbitrary",), vmem_limit_bytes=VMEM_LIMIT),
        name="moe_experts",
    )(blk_expert, n_used, xb, w_gate, w_up, w_down)


def _combine_kernel(dest_ref, dnext_ref, x_ref, rf_ref, lnf_ref, yb_ref, *rest, tc, split):
    outs, (ybuf_ref, sems) = rest[:-2], rest[-2:]
    i = pl.program_id(0)
    n = pl.num_programs(0)
    slot = i % 2

    def row_copy(src_row, s, k, t):
        return pltpu.make_async_copy(yb_ref.at[_row_at(src_row)],
                                     ybuf_ref.at[s, k, _row_at(t)], sems.at[s])

    def issue(d_ref, s):
        def body(g, c):
            for u in range(ROWS_PER_ISSUE):
                t = g * ROWS_PER_ISSUE + u
                for k in range(TOP_K):
                    row_copy(d_ref[0, 0, 2 * t + k], s, k, t).start(
                        priority=(TOP_K * u + k) % DMA_QUEUES)
            return c

        lax.fori_loop(0, tc // ROWS_PER_ISSUE, body, 0)

    @pl.when(i == 0)
    def _():
        issue(dest_ref, slot)

    @pl.when(i + 1 < n)
    def _():
        issue(dnext_ref, 1 - slot)

    for k in range(TOP_K):
        pltpu.make_async_copy(yb_ref.at[pl.ds(0, tc * ROW_SLABS)], ybuf_ref.at[slot, k],
                              sems.at[slot]).wait()

    rf = rf_ref[...]
    lo0, hi0 = _unpack_rows(_load_rows(ybuf_ref.at[slot, 0], tc))
    lo1, hi1 = _unpack_rows(_load_rows(ybuf_ref.at[slot, 1], tc))
    g0, g1 = rf[:, 0:1], rf[:, 1:2]
    out = x_ref[...] + jnp.concatenate([g0 * lo0 + g1 * lo1, g0 * hi0 + g1 * hi1], axis=1)
    if split is None:
        outs[0][...] = out
    else:
        out = _rms(out, lnf_ref[...])

        @pl.when(i < split)
        def _():
            outs[0][...] = out

        @pl.when(i >= split)
        def _():
            outs[1][...] = out


def _moe_combine(x, rf, dest, yb, ln_final, tc, final_rows):
    t = x.shape[0]
    nt = t // tc
    dest3 = dest.reshape(nt, 1, 2 * tc)
    row = lambda i: (i, 0)
    if final_rows is None:
        split = None
        out_specs = pl.BlockSpec((tc, D_MODEL), row)
        out_shape = jax.ShapeDtypeStruct((t, D_MODEL), F32)
    else:
        split = final_rows[0] // tc
        out_specs = [pl.BlockSpec((tc, D_MODEL), lambda i: (jnp.minimum(i, split - 1), 0)),
                     pl.BlockSpec((tc, D_MODEL), lambda i: (jnp.maximum(i - split, 0), 0))]
        out_shape = [jax.ShapeDtypeStruct((r, D_MODEL), F32) for r in final_rows]
    return pl.pallas_call(
        functools.partial(_combine_kernel, tc=tc, split=split),
        grid=(nt,),
        in_specs=[
            pl.BlockSpec((1, 1, 2 * tc), lambda i: (i, 0, 0), memory_space=pltpu.SMEM),
            pl.BlockSpec((1, 1, 2 * tc), lambda i: (jnp.minimum(i + 1, nt - 1), 0, 0),
                         memory_space=pltpu.SMEM),
            pl.BlockSpec((tc, D_MODEL), row),
            pl.BlockSpec((tc, LANES), row),
            pl.BlockSpec((1, D_MODEL), lambda i: (0, 0)),
            pl.BlockSpec(memory_space=pl.ANY),
        ],
        out_specs=out_specs,
        out_shape=out_shape,
        scratch_shapes=[pltpu.VMEM((2, TOP_K, ROW_SLABS * tc, LANES), U32),
                        pltpu.SemaphoreType.DMA((2,))],
        compiler_params=pltpu.CompilerParams(
            dimension_semantics=("arbitrary",), vmem_limit_bytes=VMEM_LIMIT),
        name="moe_combine",
    )(dest3, dest3, x, rf, ln_final, yb)


def _rope_tables(seq):
    inv = ROPE_BASE ** (-jnp.arange(0, D_ROPE, 2, dtype=F32) / D_ROPE)
    ang = jnp.arange(seq, dtype=F32)[:, None] * inv[None, :]
    cos, sin = jnp.cos(ang), jnp.sin(ang)
    z = lambda w: jnp.zeros((seq, w), F32)
    tail = LANES - D_NOPE - D_ROPE
    return {
        "cos": jnp.concatenate([jnp.ones((seq, D_NOPE), F32), cos, cos, z(tail)], axis=1),
        "sin_a": jnp.concatenate([z(D_NOPE), -sin, z(HALF_ROPE), z(tail)], axis=1),
        "sin_b": jnp.concatenate([z(D_NOPE), z(HALF_ROPE), sin, z(tail)], axis=1),
    }


def _t5_bucket(rel):
    half = NUM_BUCKETS // 2
    max_exact = half // 2
    n = jnp.abs(rel)
    nf = jnp.maximum(n, 1).astype(F32)
    large = max_exact + (jnp.log(nf / max_exact) / math.log(MAX_DIST / max_exact)
                         * (half - max_exact)).astype(jnp.int32)
    large = jnp.minimum(large, half - 1)
    return jnp.where(rel > 0, half, 0) + jnp.where(n < max_exact, n, large)


def _window_bias(rel_bias):
    qi = jnp.arange(WB)[:, None]
    kj = jnp.arange(3 * WB)[None, :]
    rel = kj - WB - qi
    onehot = _t5_bucket(rel)[:, :, None] == jnp.arange(NUM_BUCKETS)
    bias = jnp.sum(jnp.where(onehot[..., None], rel_bias.astype(F32), 0.0), axis=2) * LOG2E
    bias = jnp.where((jnp.abs(rel) <= WINDOW)[:, :, None], bias, NEG)
    bias = jnp.transpose(bias, (2, 0, 1)).reshape(KVB, GQ * WB, 3 * WB)
    bias = jnp.swapaxes(bias, 1, 2)
    key = jnp.arange(3 * WB)[None, :, None]
    return jnp.stack([jnp.where(key < WB, NEG, bias), bias, jnp.where(key >= 2 * WB, NEG, bias)])


def _pack_layer(l, w_in, ln_mix, ln_cq, ln_ckv, w_uq, w_ukv, w_oa, sink, w_ob, w_out, ln_ffn,
                w_gr, b_gr, w_er, b_er):
    wi = w_in[l]
    o = 0
    c_q, o = wi[:, o:o + D_CQ], o + D_CQ
    c_kv, o = wi[:, o:o + D_C], o + D_C
    k_r, o = wi[:, o:o + D_ROPE], o + D_ROPE
    q_b, o = wi[:, o:o + HB * HD], o + HB * HD
    k_b, o = wi[:, o:o + KVB * HD], o + KVB * HD
    v_b, o = wi[:, o:o + KVB * HD], o + KVB * HD
    g_ab = wi[:, o:]
    zc = lambda rows, w: jnp.zeros((rows, w), F32)
    dup = lambda w: jnp.concatenate([w[:, kv * HD:(kv + 1) * HD] for kv in range(KVB) for _ in (0, 1)],
                                    axis=1)
    kr_blk = jnp.concatenate([zc(D_MODEL, D_NOPE), k_r, zc(D_MODEL, LANES - D_NOPE - D_ROPE)], axis=1)
    w_mix = jnp.concatenate([c_q, c_kv, kr_blk, q_b * (WIN_SCALE * LOG2E), dup(k_b)], axis=1)

    uq = w_uq[l].reshape(D_CQ, HA, D_NOPE + D_ROPE)
    uq = jnp.pad(uq, ((0, 0), (0, 0), (0, LANES - D_NOPE - D_ROPE))).reshape(D_CQ, HA * LANES)
    ukv = w_ukv[l].reshape(D_C, HA, D_NOPE + D_V)
    uk = jnp.pad(ukv[:, :, :D_NOPE], ((0, 0), (0, 0), (0, LANES - D_NOPE))).reshape(D_C, HA * LANES)
    uv = ukv[:, :, D_NOPE:].reshape(D_C, HA * D_V)

    w_r = jnp.concatenate([w_er[l], w_gr[l], zc(D_MODEL, LANES - N_EXPERTS - N_GROUPS)], axis=1)
    w_r_hi = w_r.astype(BF16)
    b_r = jnp.concatenate([b_er[l], b_gr[l], jnp.zeros((LANES - N_EXPERTS - N_GROUPS,), F32)])
    sink_row = jnp.repeat(sink[l].astype(F32).reshape(KVB, GQ) * LOG2E, WB, axis=1)[:, None, :]
    return {
        "ln_mix": ln_mix[l][None, :], "w_mix": w_mix.astype(BF16), "w_vbt": v_b.T.astype(BF16),
        "ln_cq": ln_cq[l][None, :], "ln_ckv": ln_ckv[l][None, :],
        "w_uq": uq.astype(BF16), "w_uk": uk.astype(BF16), "w_uvt": uv.T.astype(BF16),
        "w_g": g_ab.astype(BF16), "w_oa": w_oa[l].astype(BF16), "w_ob": w_ob[l].astype(BF16),
        "w_out": w_out[l].astype(BF16), "ln_ffn": ln_ffn[l][None, :],
        "w_r": jnp.concatenate([w_r_hi, (w_r - w_r_hi.astype(F32)).astype(BF16)], axis=1),
        "b_r": b_r[None, :],
        "sink": sink_row,
    }


def _row_tile(n, pref):
    tile = pref
    while n % tile:
        tile //= 2
    return tile


def _trunk(xs, batch, seq, tabs, bias, layers, expert_w, ln_final):
    final_rows = tuple(x.shape[0] for x in xs)
    t = sum(final_rows)
    tm = _row_tile(seq, 512)
    tq = _row_tile(seq, 512)
    td = _row_tile(t, 1024)
    tc = _row_tile(t, 512)
    n_blocks = (t * TOP_K) // MOE_ROWS + N_EXPERTS
    n_rows = n_blocks * MOE_ROWS
    for l, p in enumerate(layers):
        qa, ka, vat, qb, kb, vbt = _mixer_in(xs, p, tabs, seq, tm)
        oa = _mla_attention(qa, ka, vat, batch, seq, tq)
        ob = _window_attention(qb, kb, vbt, bias, p["sink"], batch, seq)
        x, h2, rf, ri, cnt = _mixer_out(xs, oa, ob, p, tm)
        counts = cnt[0, :N_EXPERTS].astype(jnp.int32)
        padded = ((counts + MOE_ROWS - 1) // MOE_ROWS) * MOE_ROWS
        pad_end = jnp.cumsum(padded)
        pad_start = pad_end - padded
        onehot = ri[:, 0:TOP_K, None] == jnp.arange(N_EXPERTS, dtype=jnp.int32)
        dest = (jnp.sum(jnp.where(onehot, pad_start, 0), axis=-1) + ri[:, TOP_K:2 * TOP_K]).reshape(-1)
        blk_row = jnp.arange(n_blocks, dtype=jnp.int32)[:, None] * MOE_ROWS
        blk_expert = jnp.minimum(jnp.sum((pad_end[None, :] <= blk_row).astype(jnp.int32), axis=1),
                                 N_EXPERTS - 1)
        n_used = (pad_end[-1:] // MOE_ROWS).astype(jnp.int32)
        xb = _moe_dispatch(h2, dest, (pad_start + counts).astype(jnp.int32),
                           (padded - counts).astype(jnp.int32), n_used, n_rows, td)
        yb = _moe_experts(xb, blk_expert, n_used, *expert_w, l)
        last = l == len(layers) - 1
        x = _moe_combine(x, rf, dest, yb, ln_final, tc, final_rows if last else None)
        xs = (x, None)
    return x


def kernel(x_prompt, x_sample, rel_bias, ln_mix, w_in, ln_cq, ln_ckv, w_uq, w_ukv, w_oa, sink, w_ob,
           w_out, ln_ffn, w_gr, b_gr, w_er, b_er, w_gate, w_up, w_down, ln_final):
    bp, seq, d = x_prompt.shape
    bs = x_sample.shape[0]
    assert x_sample.shape[1] == seq and seq % WB == 0 and seq >= 3 * WB
    tabs = _rope_tables(seq)
    bias = _window_bias(rel_bias)
    layers = [_pack_layer(l, w_in, ln_mix, ln_cq, ln_ckv, w_uq, w_ukv, w_oa, sink, w_ob, w_out,
                          ln_ffn, w_gr, b_gr, w_er, b_er)
              for l in range(w_in.shape[0])]
    xs = (x_prompt.reshape(bp * seq, d), x_sample.reshape(bs * seq, d))
    y_prompt, y_sample = _trunk(xs, bp + bs, seq, tabs, bias, layers, (w_gate, w_up, w_down),
                                ln_final[None, :])
    return (y_prompt.reshape(bp, seq, d), y_sample.reshape(bs, seq, d))
```

```python
import functools
import math

import jax
import jax.numpy as jnp
from jax import lax
from jax.experimental import pallas as pl
from jax.experimental.pallas import tpu as pltpu

D_MODEL = 1024
DEPTH = 2
HA = 8
D_NOPE = 64
D_ROPE = 32
D_V = 64
D_CQ = 384
D_C = 256
ROPE_BASE = 10000.0
MLA_SCALE = (D_NOPE + D_ROPE) ** -0.5
HB = 8
KVB = 2
GQ = HB // KVB
HD = 64
WINDOW = 128
WB = 128
WIN_SCALE = HD ** -0.5
NUM_BUCKETS = 32
MAX_DIST = 128
N_GROUPS = 4
EPG = 8
N_EXPERTS = N_GROUPS * EPG
TOP_K = 2
D_EXPERT = 256
EPS = 1e-6
NEG = -1e30
LOG2E = math.log2(math.e)

LANES = 128
SUBLANES = 8
HALF_ROPE = D_ROPE // 2
MOE_ROWS = 512
MLA_HEADS = 4
MLA_KEYS = 256
DMA_QUEUES = 2
ROWS_PER_ISSUE = 8
VMEM_LIMIT = 48 * 1024 * 1024

C_CQ = 0
C_CKV = C_CQ + D_CQ
C_KR = C_CKV + D_C
C_QB = C_KR + LANES
C_KB = C_QB + HB * HD
N_MIX = C_KB + 2 * KVB * HD

F32 = jnp.float32
BF16 = jnp.bfloat16
U32 = jnp.uint32
D_PACK = D_MODEL // 2
ROW_SLABS = D_PACK // LANES


def _rms(x, g):
    ms = jnp.mean(x * x, axis=-1, keepdims=True)
    return x * lax.rsqrt(ms + EPS) * g


def _dot(a, b):
    return jnp.dot(a, b, preferred_element_type=F32)


def _dot_nt(a, b):
    return lax.dot_general(a, b, (((1,), (1,)), ((), ())), preferred_element_type=F32)


def _pack_rows(x):
    n = x.shape[1] // 2
    lo = lax.bitcast_convert_type(x[:, :n].astype(BF16).astype(F32), U32)
    hi = lax.bitcast_convert_type(x[:, n:].astype(BF16).astype(F32), U32)
    return hi | (lo >> 16)


def _unpack_rows(p):
    lo = lax.bitcast_convert_type(p << 16, F32)
    hi = lax.bitcast_convert_type(p & jnp.uint32(0xFFFF0000), F32)
    return lo, hi


def _load_rows(ref, n):
    return jnp.concatenate([ref[pl.ds(c, n, stride=ROW_SLABS), :] for c in range(ROW_SLABS)], axis=1)


def _store_rows(ref, rows):
    n = rows.shape[0]
    for c in range(ROW_SLABS):
        ref[pl.ds(c, n, stride=ROW_SLABS), :] = rows[:, c * LANES:(c + 1) * LANES]


def _row_at(i):
    return pl.ds(pl.multiple_of(i * ROW_SLABS, ROW_SLABS), ROW_SLABS)


def _x_operands(xs, tile):
    xa, xb = xs
    na = xa.shape[0] // tile
    rows = xa.shape[0] + (0 if xb is None else xb.shape[0])
    specs = (pl.BlockSpec((tile, D_MODEL), lambda i, *_: (jnp.minimum(i, na - 1), 0)),
             pl.BlockSpec((tile, D_MODEL), lambda i, *_: (jnp.maximum(i - na, 0), 0)))
    return (xa, xa if xb is None else xb), specs, na, rows


def _x_tile(xa_ref, xb_ref, na):
    return jnp.where(pl.program_id(0) < na, xa_ref[...], xb_ref[...])


def _rope(t, cos, sin_a, sin_b):
    return (t * cos + pltpu.roll(t, LANES - HALF_ROPE, 1) * sin_a
            + pltpu.roll(t, HALF_ROPE, 1) * sin_b)


def _mixer_in_kernel(xa_ref, xb_ref, lnmix_ref, win_ref, wvbt_ref, lncq_ref, lnckv_ref, wuq_ref,
                     wuk_ref, wuvt_ref, cos_ref, sina_ref, sinb_ref,
                     qa_ref, ka_ref, vat_ref, qb_ref, kb_ref, vbt_ref, *, na):
    h = _rms(_x_tile(xa_ref, xb_ref, na), lnmix_ref[...]).astype(BF16)
    cq = _dot(h, win_ref[:, C_CQ:C_CKV])
    ckv = _dot(h, win_ref[:, C_CKV:C_KR])
    kr = _dot(h, win_ref[:, C_KR:C_QB])
    qb_ref[...] = _dot(h, win_ref[:, C_QB:C_KB]).astype(BF16)
    kb_ref[...] = _dot(h, win_ref[:, C_KB:N_MIX]).astype(BF16)
    vbt = _dot_nt(wvbt_ref[...], h).astype(BF16)
    for c in range(vbt_ref.shape[0]):
        vbt_ref[c] = vbt[:, c * WB:(c + 1) * WB]
    cqn = _rms(cq, lncq_ref[...]).astype(BF16)
    ckvn = _rms(ckv, lnckv_ref[...]).astype(BF16)
    q = _dot(cqn, wuq_ref[...])
    k = _dot(ckvn, wuk_ref[...])
    vat_ref[...] = _dot_nt(wuvt_ref[...], ckvn).astype(BF16)
    cos, sin_a, sin_b = cos_ref[...], sina_ref[...], sinb_ref[...]
    kr_rot = _rope(kr, cos, sin_a, sin_b)
    for hd in range(HA):
        sl = slice(hd * LANES, (hd + 1) * LANES)
        qa_ref[:, sl] = (_rope(q[:, sl], cos, sin_a, sin_b) * (MLA_SCALE * LOG2E)).astype(BF16)
        ka_ref[:, sl] = (k[:, sl] + kr_rot).astype(BF16)


def _mixer_in(xs, p, tabs, seq, tm):
    x_ops, x_specs, na, t = _x_operands(xs, tm)
    nt = t // tm
    spt = seq // tm
    row = lambda i: (i, 0)
    full = lambda i: (0, 0)
    tab = lambda i: (i % spt, 0)
    rows = lambda w: (pl.BlockSpec((tm, w), row), jax.ShapeDtypeStruct((t, w), BF16))
    outs = [
        rows(HA * LANES),
        rows(HA * LANES),
        (pl.BlockSpec((HA * D_V, tm), lambda i: (0, i)),
         jax.ShapeDtypeStruct((HA * D_V, t), BF16)),
        rows(HB * HD),
        rows(2 * KVB * HD),
        (pl.BlockSpec((tm // WB, KVB * HD, WB), lambda i: (i, 0, 0)),
         jax.ShapeDtypeStruct((t // WB, KVB * HD, WB), BF16)),
    ]
    out_specs = [o[0] for o in outs]
    out_shape = [o[1] for o in outs]
    return pl.pallas_call(
        functools.partial(_mixer_in_kernel, na=na),
        grid=(nt,),
        in_specs=[
            *x_specs,
            pl.BlockSpec((1, D_MODEL), full),
            pl.BlockSpec((D_MODEL, N_MIX), full),
            pl.BlockSpec((KVB * HD, D_MODEL), full),
            pl.BlockSpec((1, D_CQ), full),
            pl.BlockSpec((1, D_C), full),
            pl.BlockSpec((D_CQ, HA * LANES), full),
            pl.BlockSpec((D_C, HA * LANES), full),
            pl.BlockSpec((HA * D_V, D_C), full),
            pl.BlockSpec((tm, LANES), tab),
            pl.BlockSpec((tm, LANES), tab),
            pl.BlockSpec((tm, LANES), tab),
        ],
        out_specs=out_specs,
        out_shape=out_shape,
        compiler_params=pltpu.CompilerParams(
            dimension_semantics=("parallel",), vmem_limit_bytes=VMEM_LIMIT),
        name="mixer_in",
    )(*x_ops, p["ln_mix"], p["w_mix"], p["w_vbt"], p["ln_cq"], p["ln_ckv"], p["w_uq"], p["w_uk"], p["w_uvt"],
      tabs["cos"], tabs["sin_a"], tabs["sin_b"])


def _mla_kernel(q_ref, k_ref, vt_ref, o_ref):
    heads = range(MLA_HEADS)
    nch = k_ref.shape[0] // MLA_KEYS

    def scores(j, hh):
        sl = slice(hh * LANES, (hh + 1) * LANES)
        return _dot_nt(k_ref[j * MLA_KEYS:(j + 1) * MLA_KEYS, sl], q_ref[:, sl])

    ones = jnp.ones((2 * SUBLANES, MLA_KEYS), BF16)

    def weighted_values(hh, j, p):
        vt = vt_ref[hh * D_V:(hh + 1) * D_V, j * MLA_KEYS:(j + 1) * MLA_KEYS]
        return _dot(jnp.concatenate([vt, ones], axis=0), p)

    m = [None] * MLA_HEADS
    acc = [None] * MLA_HEADS

    def accumulate(hh, j, p, alpha):
        pv = weighted_values(hh, j, p)
        acc[hh] = pv if acc[hh] is None else acc[hh] * alpha + pv

    chain = [(j, hh) for j in range(nch) for hh in heads]
    pending = None
    s_next = scores(*chain[0])
    for c, (j, hh) in enumerate(chain):
        s = s_next
        if c + 1 < len(chain):
            s_next = scores(*chain[c + 1])
        if pending is not None:
            accumulate(*pending)
        cmax = jnp.max(s, axis=0, keepdims=True)
        m_new = cmax if m[hh] is None else jnp.maximum(m[hh], cmax)
        alpha = None if m[hh] is None else jnp.exp2(m[hh] - m_new)
        m[hh] = m_new
        pending = (hh, j, jnp.exp2(s - m_new).astype(BF16), alpha)
    accumulate(*pending)
    outs = [a[:D_V] / a[D_V:D_V + 1] for a in acc]
    o_ref[...] = jnp.concatenate(outs, axis=0).T.astype(BF16)


def _mla_attention(qa, ka, va, batch, seq, tq):
    t = qa.shape[0]
    nq = seq // tq
    return pl.pallas_call(
        _mla_kernel,
        grid=(batch, HA // MLA_HEADS, nq),
        in_specs=[
            pl.BlockSpec((tq, MLA_HEADS * LANES), lambda b, j, i: (b * nq + i, j)),
            pl.BlockSpec((seq, MLA_HEADS * LANES), lambda b, j, i: (b, j)),
            pl.BlockSpec((MLA_HEADS * D_V, seq), lambda b, j, i: (j, b)),
        ],
        out_specs=pl.BlockSpec((tq, MLA_HEADS * D_V), lambda b, j, i: (b * nq + i, j)),
        out_shape=jax.ShapeDtypeStruct((t, HA * D_V), BF16),
        compiler_params=pltpu.CompilerParams(
            dimension_semantics=("parallel", "parallel", "parallel"),
            vmem_limit_bytes=VMEM_LIMIT),
        name="mla_attention",
    )(qa, ka, va)


def _window_kernel(q_ref, k_ref, vt_ref, bias_ref, sink_ref, o_ref, *, seq, qblocks):
    nb = seq // WB
    lane = lax.broadcasted_iota(jnp.int32, (WB, LANES), 1)
    lo = lane < HD
    chains = [(u, kv) for u in range(qblocks) for kv in range(KVB)]

    def key_blocks(u):
        n = pl.program_id(1) * qblocks + u
        return n, (jnp.maximum(n - 1, 0), n, jnp.minimum(n + 1, nb - 1))

    def scores(u, kv):
        n, blocks = key_blocks(u)
        kd = jnp.concatenate([k_ref[pl.ds(pl.multiple_of(b * WB, WB), WB), kv * LANES:(kv + 1) * LANES]
                              for b in blocks], axis=0)
        qs = []
        for jj in range(GQ // 2):
            qp = q_ref[u * WB:(u + 1) * WB, (kv * 2 + jj) * LANES:(kv * 2 + jj + 1) * LANES]
            qs.append(jnp.where(lo, qp, jnp.zeros_like(qp)))
            qs.append(jnp.where(lo, jnp.zeros_like(qp), qp))
        q4 = jnp.concatenate(qs, axis=0)
        edge = jnp.where(n == 0, 0, jnp.where(n == nb - 1, 2, 1))
        return _dot_nt(kd, q4) + bias_ref[edge, kv]

    def probabilities(st, kv):
        sink = sink_ref[kv]
        m = jnp.maximum(jnp.max(st, axis=0, keepdims=True), sink)
        e = jnp.exp2(st - m)
        return e.astype(BF16), jnp.sum(e, axis=0, keepdims=True) + jnp.exp2(sink - m)

    def weighted_values(u, kv, e, denom):
        _, blocks = key_blocks(u)
        vt = jnp.concatenate([vt_ref[b][kv * HD:(kv + 1) * HD, :] for b in blocks], axis=1)
        ot = _dot(vt, e) / denom
        return [ot[:, g * WB:(g + 1) * WB] for g in range(GQ)]

    heads = {}
    pending = None
    s_next = scores(*chains[0])
    for c, (u, kv) in enumerate(chains):
        s_cur = s_next
        if c + 1 < len(chains):
            s_next = scores(*chains[c + 1])
        if pending is not None:
            heads[pending[:2]] = weighted_values(*pending)
        pending = (u, kv, *probabilities(s_cur, kv))
    heads[pending[:2]] = weighted_values(*pending)
    for u in range(qblocks):
        out = jnp.concatenate([h for kv in range(KVB) for h in heads[u, kv]], axis=0)
        o_ref[u * WB:(u + 1) * WB, :] = out.T.astype(BF16)


def _window_attention(qb, kb, vbt, bias, sink, batch, seq):
    t = qb.shape[0]
    nb = seq // WB
    qblocks = 8 if nb % 8 == 0 else (4 if nb % 4 == 0 else 1)
    nb //= qblocks
    return pl.pallas_call(
        functools.partial(_window_kernel, seq=seq, qblocks=qblocks),
        grid=(batch, nb),
        in_specs=[
            pl.BlockSpec((qblocks * WB, HB * HD), lambda b, n: (b * nb + n, 0)),
            pl.BlockSpec((seq, 2 * KVB * HD), lambda b, n: (b, 0)),
            pl.BlockSpec((seq // WB, KVB * HD, WB), lambda b, n: (b, 0, 0)),
            pl.BlockSpec((3, KVB, 3 * WB, GQ * WB), lambda b, n: (0, 0, 0, 0)),
            pl.BlockSpec((KVB, 1, GQ * WB), lambda b, n: (0, 0, 0)),
        ],
        out_specs=pl.BlockSpec((qblocks * WB, HB * HD), lambda b, n: (b * nb + n, 0)),
        out_shape=jax.ShapeDtypeStruct((t, HB * HD), BF16),
        compiler_params=pltpu.CompilerParams(
            dimension_semantics=("parallel", "parallel"), vmem_limit_bytes=VMEM_LIMIT),
        name="window_attention",
    )(qb, kb, vbt, bias, sink)


def _mixer_out_kernel(xa_ref, xb_ref, oa_ref, ob_ref, lnmix_ref, wg_ref, woa_ref, wob_ref, wout_ref,
                      lnffn_ref, wrh_ref, br_ref,
                      xn_ref, h2_ref, rf_ref, ri_ref, cnt_ref, carry_ref, *, na):
    i = pl.program_id(0)

    @pl.when(i == 0)
    def _():
        carry_ref[...] = jnp.zeros_like(carry_ref)

    x = _x_tile(xa_ref, xb_ref, na)
    tm = x.shape[0]
    h = _rms(x, lnmix_ref[...]).astype(BF16)
    g = _dot(h, wg_ref[...])
    ya = _dot(oa_ref[...], woa_ref[...])
    yb = _dot(ob_ref[...], wob_ref[...])
    merged = jax.nn.sigmoid(g[:, :D_MODEL]) * ya + jax.nn.sigmoid(g[:, D_MODEL:]) * yb
    xn = x + _dot(merged.astype(BF16), wout_ref[...])
    xn_ref[...] = xn
    h2 = _rms(xn, lnffn_ref[...])
    _store_rows(h2_ref, _pack_rows(h2))

    h2_hi = h2.astype(BF16)
    h2_lo = (h2 - h2_hi.astype(F32)).astype(BF16)
    hi_terms = _dot(h2_hi, wrh_ref[...])
    logits = (hi_terms[:, :LANES] + hi_terms[:, LANES:] + _dot(h2_lo, wrh_ref[:, :LANES])
              + br_ref[...])

    lane = lax.broadcasted_iota(jnp.int32, (tm, LANES), 1)
    lanef = lane.astype(F32)
    big = float(LANES)
    is_g = (lane >= N_EXPERTS) & (lane < N_EXPERTS + N_GROUPS)
    gl = jnp.where(is_g, logits, NEG)
    gmax = jnp.max(gl, axis=1, keepdims=True)
    grp = jnp.min(jnp.where(gl == gmax, lanef, big), axis=1, keepdims=True) - N_EXPERTS
    gsum = jnp.sum(jnp.where(is_g, jnp.exp(gl - gmax), 0.0), axis=1, keepdims=True)
    grp_w = 1.0 / gsum
    is_e = (lanef >= grp * EPG) & (lanef < grp * EPG + EPG)
    el = jnp.where(is_e, logits, NEG)
    v1 = jnp.max(el, axis=1, keepdims=True)
    i1 = jnp.min(jnp.where((el == v1) & is_e, lanef, big), axis=1, keepdims=True)
    a1 = lanef == i1
    el2 = jnp.where(a1, NEG, el)
    v2 = jnp.max(el2, axis=1, keepdims=True)
    i2 = jnp.min(jnp.where((el2 == v2) & is_e & (~a1), lanef, big), axis=1, keepdims=True)
    a2 = lanef == i2
    ex = jnp.exp(v2 - v1)
    p1 = 1.0 / (1.0 + ex)
    gate1 = grp_w * p1
    gate2 = grp_w * (ex * p1)

    a1f = a1.astype(F32)
    a2f = a2.astype(F32)
    cnt = a1f + a2f
    r_i = lax.broadcasted_iota(jnp.int32, (tm, tm), 0)
    c_i = lax.broadcasted_iota(jnp.int32, (tm, tm), 1)
    tri = jnp.where(r_i > c_i, 1.0, 0.0).astype(BF16)
    before = _dot(tri, cnt.astype(BF16)) + carry_ref[...]
    rank1 = jnp.sum(a1f * before, axis=1, keepdims=True)
    rank2 = jnp.sum(a2f * before, axis=1, keepdims=True)
    carry = carry_ref[...] + jnp.sum(cnt, axis=0, keepdims=True)
    carry_ref[...] = carry
    cnt_ref[...] = jnp.broadcast_to(carry, cnt_ref.shape)

    rf_ref[...] = jnp.where(lane == 0, gate1, jnp.where(lane == 1, gate2, 0.0))
    ri = jnp.where(lane == 0, i1, jnp.where(lane == 1, i2,
                                            jnp.where(lane == 2, rank1,
                                                      jnp.where(lane == 3, rank2, 0.0))))
    ri_ref[...] = ri.astype(jnp.int32)


def _mixer_out(xs, oa, ob, p, tm):
    x_ops, x_specs, na, t = _x_operands(xs, tm)
    nt = t // tm
    row = lambda i: (i, 0)
    full = lambda i: (0, 0)
    return pl.pallas_call(
        functools.partial(_mixer_out_kernel, na=na),
        grid=(nt,),
        in_specs=[
            *x_specs,
            pl.BlockSpec((tm, HA * D_V), row),
            pl.BlockSpec((tm, HB * HD), row),
            pl.BlockSpec((1, D_MODEL), full),
            pl.BlockSpec((D_MODEL, 2 * D_MODEL), full),
            pl.BlockSpec((HA * D_V, D_MODEL), full),
            pl.BlockSpec((HB * HD, D_MODEL), full),
            pl.BlockSpec((D_MODEL, D_MODEL), full),
            pl.BlockSpec((1, D_MODEL), full),
            pl.BlockSpec((D_MODEL, 2 * LANES), full),
            pl.BlockSpec((1, LANES), full),
        ],
        out_specs=[
            pl.BlockSpec((tm, D_MODEL), row),
            pl.BlockSpec((ROW_SLABS * tm, LANES), row),
            pl.BlockSpec((tm, LANES), row),
            pl.BlockSpec((tm, LANES), row),
            pl.BlockSpec((8, LANES), full),
        ],
        out_shape=[
            jax.ShapeDtypeStruct((t, D_MODEL), F32),
            jax.ShapeDtypeStruct((ROW_SLABS * t, LANES), U32),
            jax.ShapeDtypeStruct((t, LANES), F32),
            jax.ShapeDtypeStruct((t, LANES), jnp.int32),
            jax.ShapeDtypeStruct((8, LANES), F32),
        ],
        scratch_shapes=[pltpu.VMEM((1, LANES), F32)],
        compiler_params=pltpu.CompilerParams(
            dimension_semantics=("arbitrary",), vmem_limit_bytes=VMEM_LIMIT),
        name="mixer_out",
    )(*x_ops, oa, ob, p["ln_mix"], p["w_g"], p["w_oa"], p["w_ob"], p["w_out"], p["ln_ffn"],
      p["w_r"], p["b_r"])


_ZERO_ROWS = MOE_ROWS // 2
_ZERO_SIZES = tuple(_ZERO_ROWS >> s for s in range((_ZERO_ROWS // SUBLANES).bit_length()))


def _dispatch_kernel(zstart_ref, zlen_ref, nu_ref, dest_ref, h2_ref, xb_ref, zbuf_ref, sem, zsem, *, td):
    i = pl.program_id(0)

    def row_copy(t, d):
        return pltpu.make_async_copy(h2_ref.at[_row_at(t)], xb_ref.at[_row_at(d)], sem)

    def issue(g, c):
        for u in range(ROWS_PER_ISSUE):
            t = g * ROWS_PER_ISSUE + u
            for k in range(TOP_K):
                row_copy(t, dest_ref[0, 0, 2 * t + k]).start(
                    priority=(TOP_K * u + k) % DMA_QUEUES)
        return c

    lax.fori_loop(0, td // ROWS_PER_ISSUE, issue, 0)

    def zero_copy(start, size):
        first = pl.multiple_of(start * ROW_SLABS, ROW_SLABS if size == 1 else SUBLANES * ROW_SLABS)
        return pltpu.make_async_copy(zbuf_ref.at[pl.ds(0, size * ROW_SLABS)],
                                     xb_ref.at[pl.ds(first, size * ROW_SLABS)], zsem)

    def zero_pass(do):
        def per_expert(e, c):
            start = zstart_ref[e]
            head = (-start) & (SUBLANES - 1)
            for j in range(SUBLANES - 1):
                @pl.when(j < head)
                def _(row=start + j):
                    do(zero_copy(row, 1))

            start = start + head
            left = zlen_ref[e] - head
            for size in _ZERO_SIZES:
                take = (left & size) != 0

                @pl.when(take)
                def _(start=start, size=size):
                    do(zero_copy(pl.multiple_of(start, SUBLANES), size))

                start = start + jnp.where(take, size, 0)
            return c

        lax.fori_loop(0, N_EXPERTS, per_expert, 0)

        def unused_block(b, c):
            for part in range(MOE_ROWS // _ZERO_ROWS):
                row = pl.multiple_of(b * MOE_ROWS + part * _ZERO_ROWS, _ZERO_ROWS)
                do(zero_copy(row, _ZERO_ROWS))
            return c

        lax.fori_loop(nu_ref[0], xb_ref.shape[0] // (ROW_SLABS * MOE_ROWS), unused_block, 0)

    @pl.when(i == 0)
    def _():
        zbuf_ref[...] = jnp.zeros_like(zbuf_ref)
        zero_pass(lambda cp: cp.start())
        zero_pass(lambda cp: cp.wait())

    for _ in range(TOP_K):
        pltpu.make_async_copy(h2_ref, xb_ref.at[pl.ds(0, td * ROW_SLABS)], sem).wait()


def _moe_dispatch(h2, dest, zstart, zlen, n_used, n_rows, td):
    t = h2.shape[0] // ROW_SLABS
    nt = t // td
    dest3 = dest.reshape(nt, 1, 2 * td)
    return pl.pallas_call(
        functools.partial(_dispatch_kernel, td=td),
        grid_spec=pltpu.PrefetchScalarGridSpec(
            num_scalar_prefetch=3,
            grid=(nt,),
            in_specs=[
                pl.BlockSpec((1, 1, 2 * td), lambda i, zs, zl, nu: (i, 0, 0),
                             memory_space=pltpu.SMEM),
                pl.BlockSpec((ROW_SLABS * td, LANES), lambda i, zs, zl, nu: (i, 0)),
            ],
            out_specs=pl.BlockSpec(memory_space=pl.ANY),
            scratch_shapes=[pltpu.VMEM((ROW_SLABS * _ZERO_ROWS, LANES), U32),
                            pltpu.SemaphoreType.DMA, pltpu.SemaphoreType.DMA],
        ),
        out_shape=jax.ShapeDtypeStruct((ROW_SLABS * n_rows, LANES), U32),
        compiler_params=pltpu.CompilerParams(dimension_semantics=("arbitrary",)),
        name="moe_dispatch",
    )(zstart, zlen, n_used, dest3, h2)


def _experts_kernel(be_ref, nu_ref, x_ref, wg_ref, wu_ref, wd_ref, y_ref, wgu_s, wd_s):
    b = pl.program_id(0)

    @pl.when((b == 0) | (be_ref[b] != be_ref[jnp.maximum(b - 1, 0)]))
    def _():
        wgu_s[:, :D_EXPERT] = wg_ref[0, 0].astype(BF16)
        wgu_s[:, D_EXPERT:] = wu_ref[0, 0].astype(BF16)
        wd_s[...] = wd_ref[0, 0].astype(BF16)

    @pl.when(b < nu_ref[0])
    def _():
        lo, hi = _unpack_rows(_load_rows(x_ref, MOE_ROWS))
        x = jnp.concatenate([lo.astype(BF16), hi.astype(BF16)], axis=1)
        au = _dot(x, wgu_s[...])
        a, u = au[:, :D_EXPERT], au[:, D_EXPERT:]
        hid = (a * jax.nn.sigmoid(a) * u).astype(BF16)
        _store_rows(y_ref, _pack_rows(_dot(hid, wd_s[...])))

    @pl.when(b >= nu_ref[0])
    def _():
        y_ref[...] = jnp.zeros_like(y_ref)


def _moe_experts(xb, blk_expert, n_used, w_gate, w_up, w_down, layer):
    n_rows = xb.shape[0] // ROW_SLABS
    n_blocks = n_rows // MOE_ROWS
    live = lambda b, be, nu: (jnp.minimum(b, nu[0] - 1), 0)
    expert = lambda b, be, nu: (layer, be[b], 0, 0)
    return pl.pallas_call(
        _experts_kernel,
        grid_spec=pltpu.PrefetchScalarGridSpec(
            num_scalar_prefetch=2,
            grid=(n_blocks,),
            in_specs=[
                pl.BlockSpec((ROW_SLABS * MOE_ROWS, LANES), live),
                pl.BlockSpec((1, 1, D_MODEL, D_EXPERT), expert),
                pl.BlockSpec((1, 1, D_MODEL, D_EXPERT), expert),
                pl.BlockSpec((1, 1, D_EXPERT, D_MODEL), expert),
            ],
            out_specs=pl.BlockSpec((ROW_SLABS * MOE_ROWS, LANES), lambda b, be, nu: (b, 0)),
            scratch_shapes=[pltpu.VMEM((D_MODEL, 2 * D_EXPERT), BF16),
                            pltpu.VMEM((D_EXPERT, D_MODEL), BF16)],
        ),
        out_shape=jax.ShapeDtypeStruct((ROW_SLABS * n_rows, LANES), U32),
        compiler_params=pltpu.CompilerParams(
            dimension_semantics=("arbitrary",), vmem_limit_bytes=VMEM_LIMIT),
        name="moe_experts",
    )(blk_expert, n_used, xb, w_gate, w_up, w_down)


def _combine_kernel(dest_ref, dnext_ref, x_ref, rf_ref, lnf_ref, yb_ref, *rest, tc, split):
    outs, (ybuf_ref, sems) = rest[:-2], rest[-2:]
    i = pl.program_id(0)
    n = pl.num_programs(0)
    slot = i % 2

    def row_copy(src_row, s, k, t):
        return pltpu.make_async_copy(yb_ref.at[_row_at(src_row)],
                                     ybuf_ref.at[s, k, _row_at(t)], sems.at[s])

    def issue(d_ref, s):
        def body(g, c):
            for u in range(ROWS_PER_ISSUE):
                t = g * ROWS_PER_ISSUE + u
                for k in range(TOP_K):
                    row_copy(d_ref[0, 0, 2 * t + k], s, k, t).start(
                        priority=(TOP_K * u + k) % DMA_QUEUES)
            return c

        lax.fori_loop(0, tc // ROWS_PER_ISSUE, body, 0)

    @pl.when(i == 0)
    def _():
        issue(dest_ref, slot)

    @pl.when(i + 1 < n)
    def _():
        issue(dnext_ref, 1 - slot)

    for k in range(TOP_K):
        pltpu.make_async_copy(yb_ref.at[pl.ds(0, tc * ROW_SLABS)], ybuf_ref.at[slot, k],
                              sems.at[slot]).wait()

    rf = rf_ref[...]
    lo0, hi0 = _unpack_rows(_load_rows(ybuf_ref.at[slot, 0], tc))
    lo1, hi1 = _unpack_rows(_load_rows(ybuf_ref.at[slot, 1], tc))
    g0, g1 = rf[:, 0:1], rf[:, 1:2]
    out = x_ref[...] + jnp.concatenate([g0 * lo0 + g1 * lo1, g0 * hi0 + g1 * hi1], axis=1)
    if split is None:
        outs[0][...] = out
    else:
        out = _rms(out, lnf_ref[...])

        @pl.when(i < split)
        def _():
            outs[0][...] = out

        @pl.when(i >= split)
        def _():
            outs[1][...] = out


def _moe_combine(x, rf, dest, yb, ln_final, tc, final_rows):
    t = x.shape[0]
    nt = t // tc
    dest3 = dest.reshape(nt, 1, 2 * tc)
    row = lambda i: (i, 0)
    if final_rows is None:
        split = None
        out_specs = pl.BlockSpec((tc, D_MODEL), row)
        out_shape = jax.ShapeDtypeStruct((t, D_MODEL), F32)
    else:
        split = final_rows[0] // tc
        out_specs = [pl.BlockSpec((tc, D_MODEL), lambda i: (jnp.minimum(i, split - 1), 0)),
                     pl.BlockSpec((tc, D_MODEL), lambda i: (jnp.maximum(i - split, 0), 0))]
        out_shape = [jax.ShapeDtypeStruct((r, D_MODEL), F32) for r in final_rows]
    return pl.pallas_call(
        functools.partial(_combine_kernel, tc=tc, split=split),
        grid=(nt,),
        in_specs=[
            pl.BlockSpec((1, 1, 2 * tc), lambda i: (i, 0, 0), memory_space=pltpu.SMEM),
            pl.BlockSpec((1, 1, 2 * tc), lambda i: (jnp.minimum(i + 1, nt - 1), 0, 0),
                         memory_space=pltpu.SMEM),
            pl.BlockSpec((tc, D_MODEL), row),
            pl.BlockSpec((tc, LANES), row),
            pl.BlockSpec((1, D_MODEL), lambda i: (0, 0)),
            pl.BlockSpec(memory_space=pl.ANY),
        ],
        out_specs=out_specs,
        out_shape=out_shape,
        scratch_shapes=[pltpu.VMEM((2, TOP_K, ROW_SLABS * tc, LANES), U32),
                        pltpu.SemaphoreType.DMA((2,))],
        compiler_params=pltpu.CompilerParams(
            dimension_semantics=("arbitrary",), vmem_limit_bytes=VMEM_LIMIT),
        name="moe_combine",
    )(dest3, dest3, x, rf, ln_final, yb)


def _rope_tables(seq):
    inv = ROPE_BASE ** (-jnp.arange(0, D_ROPE, 2, dtype=F32) / D_ROPE)
    ang = jnp.arange(seq, dtype=F32)[:, None] * inv[None, :]
    cos, sin = jnp.cos(ang), jnp.sin(ang)
    z = lambda w: jnp.zeros((seq, w), F32)
    tail = LANES - D_NOPE - D_ROPE
    return {
        "cos": jnp.concatenate([jnp.ones((seq, D_NOPE), F32), cos, cos, z(tail)], axis=1),
        "sin_a": jnp.concatenate([z(D_NOPE), -sin, z(HALF_ROPE), z(tail)], axis=1),
        "sin_b": jnp.concatenate([z(D_NOPE), z(HALF_ROPE), sin, z(tail)], axis=1),
    }


def _t5_bucket(rel):
    half = NUM_BUCKETS // 2
    max_exact = half // 2
    n = jnp.abs(rel)
    nf = jnp.maximum(n, 1).astype(F32)
    large = max_exact + (jnp.log(nf / max_exact) / math.log(MAX_DIST / max_exact)
                         * (half - max_exact)).astype(jnp.int32)
    large = jnp.minimum(large, half - 1)
    return jnp.where(rel > 0, half, 0) + jnp.where(n < max_exact, n, large)


def _window_bias(rel_bias):
    qi = jnp.arange(WB)[:, None]
    kj = jnp.arange(3 * WB)[None, :]
    rel = kj - WB - qi
    onehot = _t5_bucket(rel)[:, :, None] == jnp.arange(NUM_BUCKETS)
    bias = jnp.sum(jnp.where(onehot[..., None], rel_bias.astype(F32), 0.0), axis=2) * LOG2E
    bias = jnp.where((jnp.abs(rel) <= WINDOW)[:, :, None], bias, NEG)
    bias = jnp.transpose(bias, (2, 0, 1)).reshape(KVB, GQ * WB, 3 * WB)
    bias = jnp.swapaxes(bias, 1, 2)
    key = jnp.arange(3 * WB)[None, :, None]
    return jnp.stack([jnp.where(key < WB, NEG, bias), bias, jnp.where(key >= 2 * WB, NEG, bias)])


def _pack_layer(l, w_in, ln_mix, ln_cq, ln_ckv, w_uq, w_ukv, w_oa, sink, w_ob, w_out, ln_ffn,
                w_gr, b_gr, w_er, b_er):
    wi = w_in[l]
    o = 0
    c_q, o = wi[:, o:o + D_CQ], o + D_CQ
    c_kv, o = wi[:, o:o + D_C], o + D_C
    k_r, o = wi[:, o:o + D_ROPE], o + D_ROPE
    q_b, o = wi[:, o:o + HB * HD], o + HB * HD
    k_b, o = wi[:, o:o + KVB * HD], o + KVB * HD
    v_b, o = wi[:, o:o + KVB * HD], o + KVB * HD
    g_ab = wi[:, o:]
    zc = lambda rows, w: jnp.zeros((rows, w), F32)
    dup = lambda w: jnp.concatenate([w[:, kv * HD:(kv + 1) * HD] for kv in range(KVB) for _ in (0, 1)],
                                    axis=1)
    kr_blk = jnp.concatenate([zc(D_MODEL, D_NOPE), k_r, zc(D_MODEL, LANES - D_NOPE - D_ROPE)], axis=1)
    w_mix = jnp.concatenate([c_q, c_kv, kr_blk, q_b * (WIN_SCALE * LOG2E), dup(k_b)], axis=1)

    uq = w_uq[l].reshape(D_CQ, HA, D_NOPE + D_ROPE)
    uq = jnp.pad(uq, ((0, 0), (0, 0), (0, LANES - D_NOPE - D_ROPE))).reshape(D_CQ, HA * LANES)
    ukv = w_ukv[l].reshape(D_C, HA, D_NOPE + D_V)
    uk = jnp.pad(ukv[:, :, :D_NOPE], ((0, 0), (0, 0), (0, LANES - D_NOPE))).reshape(D_C, HA * LANES)
    uv = ukv[:, :, D_NOPE:].reshape(D_C, HA * D_V)

    w_r = jnp.concatenate([w_er[l], w_gr[l], zc(D_MODEL, LANES - N_EXPERTS - N_GROUPS)], axis=1)
    w_r_hi = w_r.astype(BF16)
    b_r = jnp.concatenate([b_er[l], b_gr[l], jnp.zeros((LANES - N_EXPERTS - N_GROUPS,), F32)])
    sink_row = jnp.repeat(sink[l].astype(F32).reshape(KVB, GQ) * LOG2E, WB, axis=1)[:, None, :]
    return {
        "ln_mix": ln_mix[l][None, :], "w_mix": w_mix.astype(BF16), "w_vbt": v_b.T.astype(BF16),
        "ln_cq": ln_cq[l][None, :], "ln_ckv": ln_ckv[l][None, :],
        "w_uq": uq.astype(BF16), "w_uk": uk.astype(BF16), "w_uvt": uv.T.astype(BF16),
        "w_g": g_ab.astype(BF16), "w_oa": w_oa[l].astype(BF16), "w_ob": w_ob[l].astype(BF16),
        "w_out": w_out[l].astype(BF16), "ln_ffn": ln_ffn[l][None, :],
        "w_r": jnp.concatenate([w_r_hi, (w_r - w_r_hi.astype(F32)).astype(BF16)], axis=1),
        "b_r": b_r[None, :],
        "sink": sink_row,
    }


def _row_tile(n, pref):
    tile = pref
    while n % tile:
        tile //= 2
    return tile


def _trunk(xs, batch, seq, tabs, bias, layers, expert_w, ln_final):
    final_rows = tuple(x.shape[0] for x in xs)
    t = sum(final_rows)
    tm = _row_tile(seq, 512)
    tq = _row_tile(seq, 512)
    td = _row_tile(t, 1024)
    tc = _row_tile(t, 512)
    n_blocks = (t * TOP_K) // MOE_ROWS + N_EXPERTS
    n_rows = n_blocks * MOE_ROWS
    for l, p in enumerate(layers):
        qa, ka, vat, qb, kb, vbt = _mixer_in(xs, p, tabs, seq, tm)
        oa = _mla_attention(qa, ka, vat, batch, seq, tq)
        ob = _window_attention(qb, kb, vbt, bias, p["sink"], batch, seq)
        x, h2, rf, ri, cnt = _mixer_out(xs, oa, ob, p, tm)
        counts = cnt[0, :N_EXPERTS].astype(jnp.int32)
        padded = ((counts + MOE_ROWS - 1) // MOE_ROWS) * MOE_ROWS
        pad_end = jnp.cumsum(padded)
        pad_start = pad_end - padded
        onehot = ri[:, 0:TOP_K, None] == jnp.arange(N_EXPERTS, dtype=jnp.int32)
        dest = (jnp.sum(jnp.where(onehot, pad_start, 0), axis=-1) + ri[:, TOP_K:2 * TOP_K]).reshape(-1)
        blk_row = jnp.arange(n_blocks, dtype=jnp.int32)[:, None] * MOE_ROWS
        blk_expert = jnp.minimum(jnp.sum((pad_end[None, :] <= blk_row).astype(jnp.int32), axis=1),
                                 N_EXPERTS - 1)
        n_used = (pad_end[-1:] // MOE_ROWS).astype(jnp.int32)
        xb = _moe_dispatch(h2, dest, (pad_start + counts).astype(jnp.int32),
                           (padded - counts).astype(jnp.int32), n_used, n_rows, td)
        yb = _moe_experts(xb, blk_expert, n_used, *expert_w, l)
        last = l == len(layers) - 1
        x = _moe_combine(x, rf, dest, yb, ln_final, tc, final_rows if last else None)
        xs = (x, None)
    return x


def kernel(x_prompt, x_sample, rel_bias, ln_mix, w_in, ln_cq, ln_ckv, w_uq, w_ukv, w_oa, sink, w_ob,
           w_out, ln_ffn, w_gr, b_gr, w_er, b_er, w_gate, w_up, w_down, ln_final):
    bp, seq, d = x_prompt.shape
    bs = x_sample.shape[0]
    assert x_sample.shape[1] == seq and seq % WB == 0 and seq >= 3 * WB
    tabs = _rope_tables(seq)
    bias = _window_bias(rel_bias)
    layers = [_pack_layer(l, w_in, ln_mix, ln_cq, ln_ckv, w_uq, w_ukv, w_oa, sink, w_ob, w_out,
                          ln_ffn, w_gr, b_gr, w_er, b_er)
              for l in range(w_in.shape[0])]
    xs = (x_prompt.reshape(bp * seq, d), x_sample.reshape(bs * seq, d))
    y_prompt, y_sample = _trunk(xs, bp + bs, seq, tabs, bias, layers, (w_gate, w_up, w_down),
                                ln_final[None, :])
    return (y_prompt.reshape(bp, seq, d), y_sample.reshape(bs, seq, d))
```

```python
import functools
import math

import jax
import jax.numpy as jnp
from jax import lax
from jax.experimental import pallas as pl
from jax.experimental.pallas import tpu as pltpu

D_MODEL = 1024
DEPTH = 2
HA = 8
D_NOPE = 64
D_ROPE = 32
D_V = 64
D_CQ = 384
D_C = 256
ROPE_BASE = 10000.0
MLA_SCALE = (D_NOPE + D_ROPE) ** -0.5
HB = 8
KVB = 2
GQ = HB // KVB
HD = 64
WINDOW = 128
WB = 128
WIN_SCALE = HD ** -0.5
NUM_BUCKETS = 32
MAX_DIST = 128
N_GROUPS = 4
EPG = 8
N_EXPERTS = N_GROUPS * EPG
TOP_K = 2
D_EXPERT = 256
EPS = 1e-6
NEG = -1e30
LOG2E = math.log2(math.e)

LANES = 128
SUBLANES = 8
HALF_ROPE = D_ROPE // 2
MOE_ROWS = 512
MLA_HEADS = 4
MLA_KEYS = 256
DMA_QUEUES = 2
ROWS_PER_ISSUE = 8
VMEM_LIMIT = 48 * 1024 * 1024

C_CQ = 0
C_CKV = C_CQ + D_CQ
C_KR = C_CKV + D_C
C_QB = C_KR + LANES
C_KB = C_QB + HB * HD
N_MIX = C_KB + 2 * KVB * HD

F32 = jnp.float32
BF16 = jnp.bfloat16
U32 = jnp.uint32
D_PACK = D_MODEL // 2
ROW_SLABS = D_PACK // LANES


def _rms(x, g):
    ms = jnp.mean(x * x, axis=-1, keepdims=True)
    return x * lax.rsqrt(ms + EPS) * g


def _dot(a, b):
    return jnp.dot(a, b, preferred_element_type=F32)


def _dot_nt(a, b):
    return lax.dot_general(a, b, (((1,), (1,)), ((), ())), preferred_element_type=F32)


def _pack_rows(x):
    n = x.shape[1] // 2
    lo = lax.bitcast_convert_type(x[:, :n].astype(BF16).astype(F32), U32)
    hi = lax.bitcast_convert_type(x[:, n:].astype(BF16).astype(F32), U32)
    return hi | (lo >> 16)


def _unpack_rows(p):
    lo = lax.bitcast_convert_type(p << 16, F32)
    hi = lax.bitcast_convert_type(p & jnp.uint32(0xFFFF0000), F32)
    return lo, hi


def _load_rows(ref, n):
    return jnp.concatenate([ref[pl.ds(c, n, stride=ROW_SLABS), :] for c in range(ROW_SLABS)], axis=1)


def _store_rows(ref, rows):
    n = rows.shape[0]
    for c in range(ROW_SLABS):
        ref[pl.ds(c, n, stride=ROW_SLABS), :] = rows[:, c * LANES:(c + 1) * LANES]


def _row_at(i):
    return pl.ds(pl.multiple_of(i * ROW_SLABS, ROW_SLABS), ROW_SLABS)


def _x_operands(xs, tile):
    xa, xb = xs
    na = xa.shape[0] // tile
    rows = xa.shape[0] + (0 if xb is None else xb.shape[0])
    specs = (pl.BlockSpec((tile, D_MODEL), lambda i, *_: (jnp.minimum(i, na - 1), 0)),
             pl.BlockSpec((tile, D_MODEL), lambda i, *_: (jnp.maximum(i - na, 0), 0)))
    return (xa, xa if xb is None else xb), specs, na, rows


def _x_tile(xa_ref, xb_ref, na):
    return jnp.where(pl.program_id(0) < na, xa_ref[...], xb_ref[...])


def _rope(t, cos, sin_a, sin_b):
    return (t * cos + pltpu.roll(t, LANES - HALF_ROPE, 1) * sin_a
            + pltpu.roll(t, HALF_ROPE, 1) * sin_b)


def _mixer_in_kernel(xa_ref, xb_ref, lnmix_ref, win_ref, wvbt_ref, lncq_ref, lnckv_ref, wuq_ref,
                     wuk_ref, wuvt_ref, cos_ref, sina_ref, sinb_ref,
                     qa_ref, ka_ref, vat_ref, qb_ref, kb_ref, vbt_ref, *, na):
    h = _rms(_x_tile(xa_ref, xb_ref, na), lnmix_ref[...]).astype(BF16)
    cq = _dot(h, win_ref[:, C_CQ:C_CKV])
    ckv = _dot(h, win_ref[:, C_CKV:C_KR])
    kr = _dot(h, win_ref[:, C_KR:C_QB])
    cqn = _rms(cq, lncq_ref[...]).astype(BF16)
    ckvn = _rms(ckv, lnckv_ref[...]).astype(BF16)
    q = _dot(cqn, wuq_ref[...])
    k = _dot(ckvn, wuk_ref[...])
    cos, sin_a, sin_b = cos_ref[...], sina_ref[...], sinb_ref[...]
    kr_rot = _rope(kr, cos, sin_a, sin_b)

    def rope_heads(first, last):
        for hd in range(first, last):
            sl = slice(hd * LANES, (hd + 1) * LANES)
            qa_ref[:, sl] = (_rope(q[:, sl], cos, sin_a, sin_b) * (MLA_SCALE * LOG2E)).astype(BF16)
            ka_ref[:, sl] = (k[:, sl] + kr_rot).astype(BF16)

    quarter = HA // 4
    qb_ref[...] = _dot(h, win_ref[:, C_QB:C_KB]).astype(BF16)
    rope_heads(0, quarter)
    kb_ref[...] = _dot(h, win_ref[:, C_KB:N_MIX]).astype(BF16)
    rope_heads(quarter, 2 * quarter)
    vbt = _dot_nt(wvbt_ref[...], h).astype(BF16)
    for c in range(vbt_ref.shape[0]):
        vbt_ref[c] = vbt[:, c * WB:(c + 1) * WB]
    rope_heads(2 * quarter, 3 * quarter)
    vat_ref[...] = _dot_nt(wuvt_ref[...], ckvn).astype(BF16)
    rope_heads(3 * quarter, HA)


def _mixer_in(xs, p, tabs, seq, tm):
    x_ops, x_specs, na, t = _x_operands(xs, tm)
    nt = t // tm
    spt = seq // tm
    row = lambda i: (i, 0)
    full = lambda i: (0, 0)
    tab = lambda i: (i % spt, 0)
    rows = lambda w: (pl.BlockSpec((tm, w), row), jax.ShapeDtypeStruct((t, w), BF16))
    outs = [
        rows(HA * LANES),
        rows(HA * LANES),
        (pl.BlockSpec((HA * D_V, tm), lambda i: (0, i)),
         jax.ShapeDtypeStruct((HA * D_V, t), BF16)),
        rows(HB * HD),
        rows(2 * KVB * HD),
        (pl.BlockSpec((tm // WB, KVB * HD, WB), lambda i: (i, 0, 0)),
         jax.ShapeDtypeStruct((t // WB, KVB * HD, WB), BF16)),
    ]
    out_specs = [o[0] for o in outs]
    out_shape = [o[1] for o in outs]
    return pl.pallas_call(
        functools.partial(_mixer_in_kernel, na=na),
        grid=(nt,),
        in_specs=[
            *x_specs,
            pl.BlockSpec((1, D_MODEL), full),
            pl.BlockSpec((D_MODEL, N_MIX), full),
            pl.BlockSpec((KVB * HD, D_MODEL), full),
            pl.BlockSpec((1, D_CQ), full),
            pl.BlockSpec((1, D_C), full),
            pl.BlockSpec((D_CQ, HA * LANES), full),
            pl.BlockSpec((D_C, HA * LANES), full),
            pl.BlockSpec((HA * D_V, D_C), full),
            pl.BlockSpec((tm, LANES), tab),
            pl.BlockSpec((tm, LANES), tab),
            pl.BlockSpec((tm, LANES), tab),
        ],
        out_specs=out_specs,
        out_shape=out_shape,
        compiler_params=pltpu.CompilerParams(
            dimension_semantics=("parallel",), vmem_limit_bytes=VMEM_LIMIT),
        name="mixer_in",
    )(*x_ops, p["ln_mix"], p["w_mix"], p["w_vbt"], p["ln_cq"], p["ln_ckv"], p["w_uq"], p["w_uk"], p["w_uvt"],
      tabs["cos"], tabs["sin_a"], tabs["sin_b"])


def _mla_kernel(q_ref, k_ref, vt_ref, o_ref):
    heads = range(MLA_HEADS)
    nch = k_ref.shape[0] // MLA_KEYS

    def scores(j, hh):
        sl = slice(hh * LANES, (hh + 1) * LANES)
        return _dot_nt(k_ref[j * MLA_KEYS:(j + 1) * MLA_KEYS, sl], q_ref[:, sl])

    ones = jnp.ones((2 * SUBLANES, MLA_KEYS), BF16)

    def weighted_values(hh, j, p):
        vt = vt_ref[hh * D_V:(hh + 1) * D_V, j * MLA_KEYS:(j + 1) * MLA_KEYS]
        return _dot(jnp.concatenate([vt, ones], axis=0), p)

    m = [None] * MLA_HEADS
    acc = [None] * MLA_HEADS

    def accumulate(hh, j, p, alpha):
        pv = weighted_values(hh, j, p)
        acc[hh] = pv if acc[hh] is None else acc[hh] * alpha + pv

    chain = [(j, hh) for j in range(nch) for hh in heads]
    pending = None
    s_next = scores(*chain[0])
    for c, (j, hh) in enumerate(chain):
        s = s_next
        if c + 1 < len(chain):
            s_next = scores(*chain[c + 1])
        if pending is not None:
            accumulate(*pending)
        cmax = jnp.max(s, axis=0, keepdims=True)
        m_new = cmax if m[hh] is None else jnp.maximum(m[hh], cmax)
        alpha = None if m[hh] is None else jnp.exp2(m[hh] - m_new)
        m[hh] = m_new
        pending = (hh, j, jnp.exp2(s - m_new).astype(BF16), alpha)
    accumulate(*pending)
    outs = [a[:D_V] / a[D_V:D_V + 1] for a in acc]
    o_ref[...] = jnp.concatenate(outs, axis=0).T.astype(BF16)


def _mla_attention(qa, ka, va, batch, seq, tq):
    t = qa.shape[0]
    nq = seq // tq
    return pl.pallas_call(
        _mla_kernel,
        grid=(batch, HA // MLA_HEADS, nq),
        in_specs=[
            pl.BlockSpec((tq, MLA_HEADS * LANES), lambda b, j, i: (b * nq + i, j)),
            pl.BlockSpec((seq, MLA_HEADS * LANES), lambda b, j, i: (b, j)),
            pl.BlockSpec((MLA_HEADS * D_V, seq), lambda b, j, i: (j, b)),
        ],
        out_specs=pl.BlockSpec((tq, MLA_HEADS * D_V), lambda b, j, i: (b * nq + i, j)),
        out_shape=jax.ShapeDtypeStruct((t, HA * D_V), BF16),
        compiler_params=pltpu.CompilerParams(
            dimension_semantics=("parallel", "parallel", "parallel"),
            vmem_limit_bytes=VMEM_LIMIT),
        name="mla_attention",
    )(qa, ka, va)


def _window_kernel(q_ref, k_ref, vt_ref, bias_ref, sink_ref, o_ref, *, seq, qblocks):
    nb = seq // WB
    lane = lax.broadcasted_iota(jnp.int32, (WB, LANES), 1)
    lo = lane < HD
    chains = [(u, kv) for u in range(qblocks) for kv in range(KVB)]

    def key_blocks(u):
        n = pl.program_id(1) * qblocks + u
        return n, (jnp.maximum(n - 1, 0), n, jnp.minimum(n + 1, nb - 1))

    def scores(u, kv):
        n, blocks = key_blocks(u)
        kd = jnp.concatenate([k_ref[pl.ds(pl.multiple_of(b * WB, WB), WB), kv * LANES:(kv + 1) * LANES]
                              for b in blocks], axis=0)
        qs = []
        for jj in range(GQ // 2):
            qp = q_ref[u * WB:(u + 1) * WB, (kv * 2 + jj) * LANES:(kv * 2 + jj + 1) * LANES]
            qs.append(jnp.where(lo, qp, jnp.zeros_like(qp)))
            qs.append(jnp.where(lo, jnp.zeros_like(qp), qp))
        q4 = jnp.concatenate(qs, axis=0)
        edge = jnp.where(n == 0, 0, jnp.where(n == nb - 1, 2, 1))
        return _dot_nt(kd, q4) + bias_ref[edge, kv]

    def probabilities(st, kv):
        sink = sink_ref[kv]
        m = jnp.maximum(jnp.max(st, axis=0, keepdims=True), sink)
        e = jnp.exp2(st - m)
        return e.astype(BF16), jnp.sum(e, axis=0, keepdims=True) + jnp.exp2(sink - m)

    def weighted_values(u, kv, e, denom):
        _, blocks = key_blocks(u)
        vt = jnp.concatenate([vt_ref[b][kv * HD:(kv + 1) * HD, :] for b in blocks], axis=1)
        ot = _dot(vt, e) / denom
        return [ot[:, g * WB:(g + 1) * WB] for g in range(GQ)]

    heads = {}
    pending = None
    s_next = scores(*chains[0])
    for c, (u, kv) in enumerate(chains):
        s_cur = s_next
        if c + 1 < len(chains):
            s_next = scores(*chains[c + 1])
        if pending is not None:
            heads[pending[:2]] = weighted_values(*pending)
        pending = (u, kv, *probabilities(s_cur, kv))
    heads[pending[:2]] = weighted_values(*pending)
    for u in range(qblocks):
        out = jnp.concatenate([h for kv in range(KVB) for h in heads[u, kv]], axis=0)
        o_ref[u * WB:(u + 1) * WB, :] = out.T.astype(BF16)


def _window_attention(qb, kb, vbt, bias, sink, batch, seq):
    t = qb.shape[0]
    nb = seq // WB
    qblocks = 8 if nb % 8 == 0 else (4 if nb % 4 == 0 else 1)
    nb //= qblocks
    return pl.pallas_call(
        functools.partial(_window_kernel, seq=seq, qblocks=qblocks),
        grid=(batch, nb),
        in_specs=[
            pl.BlockSpec((qblocks * WB, HB * HD), lambda b, n: (b * nb + n, 0)),
            pl.BlockSpec((seq, 2 * KVB * HD), lambda b, n: (b, 0)),
            pl.BlockSpec((seq // WB, KVB * HD, WB), lambda b, n: (b, 0, 0)),
            pl.BlockSpec((3, KVB, 3 * WB, GQ * WB), lambda b, n: (0, 0, 0, 0)),
            pl.BlockSpec((KVB, 1, GQ * WB), lambda b, n: (0, 0, 0)),
        ],
        out_specs=pl.BlockSpec((qblocks * WB, HB * HD), lambda b, n: (b * nb + n, 0)),
        out_shape=jax.ShapeDtypeStruct((t, HB * HD), BF16),
        compiler_params=pltpu.CompilerParams(
            dimension_semantics=("parallel", "parallel"), vmem_limit_bytes=VMEM_LIMIT),
        name="window_attention",
    )(qb, kb, vbt, bias, sink)


def _mixer_out_kernel(xa_ref, xb_ref, oa_ref, ob_ref, lnmix_ref, wg_ref, woa_ref, wob_ref, wout_ref,
                      lnffn_ref, wrh_ref, br_ref,
                      xn_ref, h2_ref, rf_ref, ri_ref, cnt_ref, carry_ref, *, na):
    i = pl.program_id(0)

    @pl.when(i == 0)
    def _():
        carry_ref[...] = jnp.zeros_like(carry_ref)

    x = _x_tile(xa_ref, xb_ref, na)
    tm = x.shape[0]
    h = _rms(x, lnmix_ref[...]).astype(BF16)
    g = _dot(h, wg_ref[...])
    ya = _dot(oa_ref[...], woa_ref[...])
    yb = _dot(ob_ref[...], wob_ref[...])
    merged = jax.nn.sigmoid(g[:, :D_MODEL]) * ya + jax.nn.sigmoid(g[:, D_MODEL:]) * yb
    xn = x + _dot(merged.astype(BF16), wout_ref[...])
    xn_ref[...] = xn
    h2 = _rms(xn, lnffn_ref[...])
    _store_rows(h2_ref, _pack_rows(h2))

    h2_hi = h2.astype(BF16)
    h2_lo = (h2 - h2_hi.astype(F32)).astype(BF16)
    hi_terms = _dot(h2_hi, wrh_ref[...])
    logits = (hi_terms[:, :LANES] + hi_terms[:, LANES:] + _dot(h2_lo, wrh_ref[:, :LANES])
              + br_ref[...])

    lane = lax.broadcasted_iota(jnp.int32, (tm, LANES), 1)
    lanef = lane.astype(F32)
    big = float(LANES)
    is_g = (lane >= N_EXPERTS) & (lane < N_EXPERTS + N_GROUPS)
    gl = jnp.where(is_g, logits, NEG)
    gmax = jnp.max(gl, axis=1, keepdims=True)
    grp = jnp.min(jnp.where(gl == gmax, lanef, big), axis=1, keepdims=True) - N_EXPERTS
    gsum = jnp.sum(jnp.where(is_g, jnp.exp(gl - gmax), 0.0), axis=1, keepdims=True)
    grp_w = 1.0 / gsum
    is_e = (lanef >= grp * EPG) & (lanef < grp * EPG + EPG)
    el = jnp.where(is_e, logits, NEG)
    v1 = jnp.max(el, axis=1, keepdims=True)
    i1 = jnp.min(jnp.where((el == v1) & is_e, lanef, big), axis=1, keepdims=True)
    a1 = lanef == i1
    el2 = jnp.where(a1, NEG, el)
    v2 = jnp.max(el2, axis=1, keepdims=True)
    i2 = jnp.min(jnp.where((el2 == v2) & is_e & (~a1), lanef, big), axis=1, keepdims=True)
    a2 = lanef == i2
    ex = jnp.exp(v2 - v1)
    p1 = 1.0 / (1.0 + ex)
    gate1 = grp_w * p1
    gate2 = grp_w * (ex * p1)

    a1f = a1.astype(F32)
    a2f = a2.astype(F32)
    cnt = a1f + a2f
    r_i = lax.broadcasted_iota(jnp.int32, (tm, tm), 0)
    c_i = lax.broadcasted_iota(jnp.int32, (tm, tm), 1)
    tri = jnp.where(r_i > c_i, 1.0, 0.0).astype(BF16)
    before = _dot(tri, cnt.astype(BF16)) + carry_ref[...]
    rank1 = jnp.sum(a1f * before, axis=1, keepdims=True)
    rank2 = jnp.sum(a2f * before, axis=1, keepdims=True)
    carry = carry_ref[...] + jnp.sum(cnt, axis=0, keepdims=True)
    carry_ref[...] = carry
    cnt_ref[...] = jnp.broadcast_to(carry, cnt_ref.shape)

    rf_ref[...] = jnp.where(lane == 0, gate1, jnp.where(lane == 1, gate2, 0.0))
    ri = jnp.where(lane == 0, i1, jnp.where(lane == 1, i2,
                                            jnp.where(lane == 2, rank1,
                                                      jnp.where(lane == 3, rank2, 0.0))))
    ri_ref[...] = ri.astype(jnp.int32)


def _mixer_out(xs, oa, ob, p, tm):
    x_ops, x_specs, na, t = _x_operands(xs, tm)
    nt = t // tm
    row = lambda i: (i, 0)
    full = lambda i: (0, 0)
    return pl.pallas_call(
        functools.partial(_mixer_out_kernel, na=na),
        grid=(nt,),
        in_specs=[
            *x_specs,
            pl.BlockSpec((tm, HA * D_V), row),
            pl.BlockSpec((tm, HB * HD), row),
            pl.BlockSpec((1, D_MODEL), full),
            pl.BlockSpec((D_MODEL, 2 * D_MODEL), full),
            pl.BlockSpec((HA * D_V, D_MODEL), full),
            pl.BlockSpec((HB * HD, D_MODEL), full),
            pl.BlockSpec((D_MODEL, D_MODEL), full),
            pl.BlockSpec((1, D_MODEL), full),
            pl.BlockSpec((D_MODEL, 2 * LANES), full),
            pl.BlockSpec((1, LANES), full),
        ],
        out_specs=[
            pl.BlockSpec((tm, D_MODEL), row),
            pl.BlockSpec((ROW_SLABS * tm, LANES), row),
            pl.BlockSpec((tm, LANES), row),
            pl.BlockSpec((tm, LANES), row),
            pl.BlockSpec((8, LANES), full),
        ],
        out_shape=[
            jax.ShapeDtypeStruct((t, D_MODEL), F32),
            jax.ShapeDtypeStruct((ROW_SLABS * t, LANES), U32),
            jax.ShapeDtypeStruct((t, LANES), F32),
            jax.ShapeDtypeStruct((t, LANES), jnp.int32),
            jax.ShapeDtypeStruct((8, LANES), F32),
        ],
        scratch_shapes=[pltpu.VMEM((1, LANES), F32)],
        compiler_params=pltpu.CompilerParams(
            dimension_semantics=("arbitrary",), vmem_limit_bytes=VMEM_LIMIT),
        name="mixer_out",
    )(*x_ops, oa, ob, p["ln_mix"], p["w_g"], p["w_oa"], p["w_ob"], p["w_out"], p["ln_ffn"],
      p["w_r"], p["b_r"])


_ZERO_ROWS = MOE_ROWS // 2
_ZERO_SIZES = tuple(_ZERO_ROWS >> s for s in range((_ZERO_ROWS // SUBLANES).bit_length()))


def _dispatch_kernel(zstart_ref, zlen_ref, nu_ref, dest_ref, h2_ref, xb_ref, zbuf_ref, sem, zsem, *, td):
    i = pl.program_id(0)

    def row_copy(t, d):
        return pltpu.make_async_copy(h2_ref.at[_row_at(t)], xb_ref.at[_row_at(d)], sem)

    def issue(g, c):
        for u in range(ROWS_PER_ISSUE):
            t = g * ROWS_PER_ISSUE + u
            for k in range(TOP_K):
                row_copy(t, dest_ref[0, 0, 2 * t + k]).start(
                    priority=(TOP_K * u + k) % DMA_QUEUES)
        return c

    lax.fori_loop(0, td // ROWS_PER_ISSUE, issue, 0)

    def zero_copy(start, size):
        first = pl.multiple_of(start * ROW_SLABS, ROW_SLABS if size == 1 else SUBLANES * ROW_SLABS)
        return pltpu.make_async_copy(zbuf_ref.at[pl.ds(0, size * ROW_SLABS)],
                                     xb_ref.at[pl.ds(first, size * ROW_SLABS)], zsem)

    def zero_pass(do):
        def per_expert(e, c):
            start = zstart_ref[e]
            head = (-start) & (SUBLANES - 1)
            for j in range(SUBLANES - 1):
                @pl.when(j < head)
                def _(row=start + j):
                    do(zero_copy(row, 1))

            start = start + head
            left = zlen_ref[e] - head
            for size in _ZERO_SIZES:
                take = (left & size) != 0

                @pl.when(take)
                def _(start=start, size=size):
                    do(zero_copy(pl.multiple_of(start, SUBLANES), size))

                start = start + jnp.where(take, size, 0)
            return c

        lax.fori_loop(0, N_EXPERTS, per_expert, 0)

        def unused_block(b, c):
            for part in range(MOE_ROWS // _ZERO_ROWS):
                row = pl.multiple_of(b * MOE_ROWS + part * _ZERO_ROWS, _ZERO_ROWS)
                do(zero_copy(row, _ZERO_ROWS))
            return c

        lax.fori_loop(nu_ref[0], xb_ref.shape[0] // (ROW_SLABS * MOE_ROWS), unused_block, 0)

    @pl.when(i == 0)
    def _():
        zbuf_ref[...] = jnp.zeros_like(zbuf_ref)
        zero_pass(lambda cp: cp.start())
        zero_pass(lambda cp: cp.wait())

    for _ in range(TOP_K):
        pltpu.make_async_copy(h2_ref, xb_ref.at[pl.ds(0, td * ROW_SLABS)], sem).wait()


def _moe_dispatch(h2, dest, zstart, zlen, n_used, n_rows, td):
    t = h2.shape[0] // ROW_SLABS
    nt = t // td
    dest3 = dest.reshape(nt, 1, 2 * td)
    return pl.pallas_call(
        functools.partial(_dispatch_kernel, td=td),
        grid_spec=pltpu.PrefetchScalarGridSpec(
            num_scalar_prefetch=3,
            grid=(nt,),
            in_specs=[
                pl.BlockSpec((1, 1, 2 * td), lambda i, zs, zl, nu: (i, 0, 0),
                             memory_space=pltpu.SMEM),
                pl.BlockSpec((ROW_SLABS * td, LANES), lambda i, zs, zl, nu: (i, 0)),
            ],
            out_specs=pl.BlockSpec(memory_space=pl.ANY),
            scratch_shapes=[pltpu.VMEM((ROW_SLABS * _ZERO_ROWS, LANES), U32),
                            pltpu.SemaphoreType.DMA, pltpu.SemaphoreType.DMA],
        ),
        out_shape=jax.ShapeDtypeStruct((ROW_SLABS * n_rows, LANES), U32),
        compiler_params=pltpu.CompilerParams(dimension_semantics=("arbitrary",)),
        name="moe_dispatch",
    )(zstart, zlen, n_used, dest3, h2)


def _experts_kernel(be_ref, nu_ref, x_ref, wg_ref, wu_ref, wd_ref, y_ref, wgu_s, wd_s):
    b = pl.program_id(0)

    @pl.when((b == 0) | (be_ref[b] != be_ref[jnp.maximum(b - 1, 0)]))
    def _():
        wgu_s[:, :D_EXPERT] = wg_ref[0, 0].astype(BF16)
        wgu_s[:, D_EXPERT:] = wu_ref[0, 0].astype(BF16)
        wd_s[...] = wd_ref[0, 0].astype(BF16)

    @pl.when(b < nu_ref[0])
    def _():
        lo, hi = _unpack_rows(_load_rows(x_ref, MOE_ROWS))
        x = jnp.concatenate([lo.astype(BF16), hi.astype(BF16)], axis=1)
        au = _dot(x, wgu_s[...])
        a, u = au[:, :D_EXPERT], au[:, D_EXPERT:]
        hid = (a * jax.nn.sigmoid(a) * u).astype(BF16)
        _store_rows(y_ref, _pack_rows(_dot(hid, wd_s[...])))

    @pl.when(b >= nu_ref[0])
    def _():
        y_ref[...] = jnp.zeros_like(y_ref)


def _moe_experts(xb, blk_expert, n_used, w_gate, w_up, w_down, layer):
    n_rows = xb.shape[0] // ROW_SLABS
    n_blocks = n_rows // MOE_ROWS
    live = lambda b, be, nu: (jnp.minimum(b, nu[0] - 1), 0)
    expert = lambda b, be, nu: (layer, be[b], 0, 0)
    return pl.pallas_call(
        _experts_kernel,
        grid_spec=pltpu.PrefetchScalarGridSpec(
            num_scalar_prefetch=2,
            grid=(n_blocks,),
            in_specs=[
                pl.BlockSpec((ROW_SLABS * MOE_ROWS, LANES), live),
                pl.BlockSpec((1, 1, D_MODEL, D_EXPERT), expert),
                pl.BlockSpec((1, 1, D_MODEL, D_EXPERT), expert),
                pl.BlockSpec((1, 1, D_EXPERT, D_MODEL), expert),
            ],
            out_specs=pl.BlockSpec((ROW_SLABS * MOE_ROWS, LANES), lambda b, be, nu: (b, 0)),
            scratch_shapes=[pltpu.VMEM((D_MODEL, 2 * D_EXPERT), BF16),
                            pltpu.VMEM((D_EXPERT, D_MODEL), BF16)],
        ),
        out_shape=jax.ShapeDtypeStruct((ROW_SLABS * n_rows, LANES), U32),
        compiler_params=pltpu.CompilerParams(
            dimension_semantics=("arbitrary",), vmem_limit_bytes=VMEM_LIMIT),
        name="moe_experts",
    )(blk_expert, n_used, xb, w_gate, w_up, w_down)


def _combine_kernel(dest_ref, dnext_ref, x_ref, rf_ref, lnf_ref, yb_ref, *rest, tc, split):
    outs, (ybuf_ref, sems) = rest[:-2], rest[-2:]
    i = pl.program_id(0)
    n = pl.num_programs(0)
    slot = i % 2

    def row_copy(src_row, s, k, t):
        return pltpu.make_async_copy(yb_ref.at[_row_at(src_row)],
                                     ybuf_ref.at[s, k, _row_at(t)], sems.at[s])

    def issue(d_ref, s):
        def body(g, c):
            for u in range(ROWS_PER_ISSUE):
                t = g * ROWS_PER_ISSUE + u
                for k in range(TOP_K):
                    row_copy(d_ref[0, 0, 2 * t + k], s, k, t).start(
                        priority=(TOP_K * u + k) % DMA_QUEUES)
            return c

        lax.fori_loop(0, tc // ROWS_PER_ISSUE, body, 0)

    @pl.when(i == 0)
    def _():
        issue(dest_ref, slot)

    @pl.when(i + 1 < n)
    def _():
        issue(dnext_ref, 1 - slot)

    for k in range(TOP_K):
        pltpu.make_async_copy(yb_ref.at[pl.ds(0, tc * ROW_SLABS)], ybuf_ref.at[slot, k],
                              sems.at[slot]).wait()

    rf = rf_ref[...]
    lo0, hi0 = _unpack_rows(_load_rows(ybuf_ref.at[slot, 0], tc))
    lo1, hi1 = _unpack_rows(_load_rows(ybuf_ref.at[slot, 1], tc))
    g0, g1 = rf[:, 0:1], rf[:, 1:2]
    out = x_ref[...] + jnp.concatenate([g0 * lo0 + g1 * lo1, g0 * hi0 + g1 * hi1], axis=1)
    if split is None:
        outs[0][...] = out
    else:
        out = _rms(out, lnf_ref[...])

        @pl.when(i < split)
        def _():
            outs[0][...] = out

        @pl.when(i >= split)
        def _():
            outs[1][...] = out


def _moe_combine(x, rf, dest, yb, ln_final, tc, final_rows):
    t = x.shape[0]
    nt = t // tc
    dest3 = dest.reshape(nt, 1, 2 * tc)
    row = lambda i: (i, 0)
    if final_rows is None:
        split = None
        out_specs = pl.BlockSpec((tc, D_MODEL), row)
        out_shape = jax.ShapeDtypeStruct((t, D_MODEL), F32)
    else:
        split = final_rows[0] // tc
        out_specs = [pl.BlockSpec((tc, D_MODEL), lambda i: (jnp.minimum(i, split - 1), 0)),
                     pl.BlockSpec((tc, D_MODEL), lambda i: (jnp.maximum(i - split, 0), 0))]
        out_shape = [jax.ShapeDtypeStruct((r, D_MODEL), F32) for r in final_rows]
    return pl.pallas_call(
        functools.partial(_combine_kernel, tc=tc, split=split),
        grid=(nt,),
        in_specs=[
            pl.BlockSpec((1, 1, 2 * tc), lambda i: (i, 0, 0), memory_space=pltpu.SMEM),
            pl.BlockSpec((1, 1, 2 * tc), lambda i: (jnp.minimum(i + 1, nt - 1), 0, 0),
                         memory_space=pltpu.SMEM),
            pl.BlockSpec((tc, D_MODEL), row),
            pl.BlockSpec((tc, LANES), row),
            pl.BlockSpec((1, D_MODEL), lambda i: (0, 0)),
            pl.BlockSpec(memory_space=pl.ANY),
        ],
        out_specs=out_specs,
        out_shape=out_shape,
        scratch_shapes=[pltpu.VMEM((2, TOP_K, ROW_SLABS * tc, LANES), U32),
                        pltpu.SemaphoreType.DMA((2,))],
        compiler_params=pltpu.CompilerParams(
            dimension_semantics=("arbitrary",), vmem_limit_bytes=VMEM_LIMIT),
        name="moe_combine",
    )(dest3, dest3, x, rf, ln_final, yb)


def _rope_tables(seq):
    inv = ROPE_BASE ** (-jnp.arange(0, D_ROPE, 2, dtype=F32) / D_ROPE)
    ang = jnp.arange(seq, dtype=F32)[:, None] * inv[None, :]
    cos, sin = jnp.cos(ang), jnp.sin(ang)
    z = lambda w: jnp.zeros((seq, w), F32)
    tail = LANES - D_NOPE - D_ROPE
    return {
        "cos": jnp.concatenate([jnp.ones((seq, D_NOPE), F32), cos, cos, z(tail)], axis=1),
        "sin_a": jnp.concatenate([z(D_NOPE), -sin, z(HALF_ROPE), z(tail)], axis=1),
        "sin_b": jnp.concatenate([z(D_NOPE), z(HALF_ROPE), sin, z(tail)], axis=1),
    }


def _t5_bucket(rel):
    half = NUM_BUCKETS // 2
    max_exact = half // 2
    n = jnp.abs(rel)
    nf = jnp.maximum(n, 1).astype(F32)
    large = max_exact + (jnp.log(nf / max_exact) / math.log(MAX_DIST / max_exact)
                         * (half - max_exact)).astype(jnp.int32)
    large = jnp.minimum(large, half - 1)
    return jnp.where(rel > 0, half, 0) + jnp.where(n < max_exact, n, large)


def _window_bias(rel_bias):
    qi = jnp.arange(WB)[:, None]
    kj = jnp.arange(3 * WB)[None, :]
    rel = kj - WB - qi
    onehot = _t5_bucket(rel)[:, :, None] == jnp.arange(NUM_BUCKETS)
    bias = jnp.sum(jnp.where(onehot[..., None], rel_bias.astype(F32), 0.0), axis=2) * LOG2E
    bias = jnp.where((jnp.abs(rel) <= WINDOW)[:, :, None], bias, NEG)
    bias = jnp.transpose(bias, (2, 0, 1)).reshape(KVB, GQ * WB, 3 * WB)
    bias = jnp.swapaxes(bias, 1, 2)
    key = jnp.arange(3 * WB)[None, :, None]
    return jnp.stack([jnp.where(key < WB, NEG, bias), bias, jnp.where(key >= 2 * WB, NEG, bias)])


def _pack_layer(l, w_in, ln_mix, ln_cq, ln_ckv, w_uq, w_ukv, w_oa, sink, w_ob, w_out, ln_ffn,
                w_gr, b_gr, w_er, b_er):
    wi = w_in[l]
    o = 0
    c_q, o = wi[:, o:o + D_CQ], o + D_CQ
    c_kv, o = wi[:, o:o + D_C], o + D_C
    k_r, o = wi[:, o:o + D_ROPE], o + D_ROPE
    q_b, o = wi[:, o:o + HB * HD], o + HB * HD
    k_b, o = wi[:, o:o + KVB * HD], o + KVB * HD
    v_b, o = wi[:, o:o + KVB * HD], o + KVB * HD
    g_ab = wi[:, o:]
    zc = lambda rows, w: jnp.zeros((rows, w), F32)
    dup = lambda w: jnp.concatenate([w[:, kv * HD:(kv + 1) * HD] for kv in range(KVB) for _ in (0, 1)],
                                    axis=1)
    kr_blk = jnp.concatenate([zc(D_MODEL, D_NOPE), k_r, zc(D_MODEL, LANES - D_NOPE - D_ROPE)], axis=1)
    w_mix = jnp.concatenate([c_q, c_kv, kr_blk, q_b * (WIN_SCALE * LOG2E), dup(k_b)], axis=1)

    uq = w_uq[l].reshape(D_CQ, HA, D_NOPE + D_ROPE)
    uq = jnp.pad(uq, ((0, 0), (0, 0), (0, LANES - D_NOPE - D_ROPE))).reshape(D_CQ, HA * LANES)
    ukv = w_ukv[l].reshape(D_C, HA, D_NOPE + D_V)
    uk = jnp.pad(ukv[:, :, :D_NOPE], ((0, 0), (0, 0), (0, LANES - D_NOPE))).reshape(D_C, HA * LANES)
    uv = ukv[:, :, D_NOPE:].reshape(D_C, HA * D_V)

    w_r = jnp.concatenate([w_er[l], w_gr[l], zc(D_MODEL, LANES - N_EXPERTS - N_GROUPS)], axis=1)
    w_r_hi = w_r.astype(BF16)
    b_r = jnp.concatenate([b_er[l], b_gr[l], jnp.zeros((LANES - N_EXPERTS - N_GROUPS,), F32)])
    sink_row = jnp.repeat(sink[l].astype(F32).reshape(KVB, GQ) * LOG2E, WB, axis=1)[:, None, :]
    return {
        "ln_mix": ln_mix[l][None, :], "w_mix": w_mix.astype(BF16), "w_vbt": v_b.T.astype(BF16),
        "ln_cq": ln_cq[l][None, :], "ln_ckv": ln_ckv[l][None, :],
        "w_uq": uq.astype(BF16), "w_uk": uk.astype(BF16), "w_uvt": uv.T.astype(BF16),
        "w_g": g_ab.astype(BF16), "w_oa": w_oa[l].astype(BF16), "w_ob": w_ob[l].astype(BF16),
        "w_out": w_out[l].astype(BF16), "ln_ffn": ln_ffn[l][None, :],
        "w_r": jnp.concatenate([w_r_hi, (w_r - w_r_hi.astype(F32)).astype(BF16)], axis=1),
        "b_r": b_r[None, :],
        "sink": sink_row,
    }


def _row_tile(n, pref):
    tile = pref
    while n % tile:
        tile //= 2
    return tile


def _trunk(xs, batch, seq, tabs, bias, layers, expert_w, ln_final):
    final_rows = tuple(x.shape[0] for x in xs)
    t = sum(final_rows)
    tm = _row_tile(seq, 512)
    tq = _row_tile(seq, 512)
    td = _row_tile(t, 1024)
    tc = _row_tile(t, 512)
    n_blocks = (t * TOP_K) // MOE_ROWS + N_EXPERTS
    n_rows = n_blocks * MOE_ROWS
    for l, p in enumerate(layers):
        qa, ka, vat, qb, kb, vbt = _mixer_in(xs, p, tabs, seq, tm)
        oa = _mla_attention(qa, ka, vat, batch, seq, tq)
        ob = _window_attention(qb, kb, vbt, bias, p["sink"], batch, seq)
        x, h2, rf, ri, cnt = _mixer_out(xs, oa, ob, p, tm)
        counts = cnt[0, :N_EXPERTS].astype(jnp.int32)
        padded = ((counts + MOE_ROWS - 1) // MOE_ROWS) * MOE_ROWS
        pad_end = jnp.cumsum(padded)
        pad_start = pad_end - padded
        onehot = ri[:, 0:TOP_K, None] == jnp.arange(N_EXPERTS, dtype=jnp.int32)
        dest = (jnp.sum(jnp.where(onehot, pad_start, 0), axis=-1) + ri[:, TOP_K:2 * TOP_K]).reshape(-1)
        blk_row = jnp.arange(n_blocks, dtype=jnp.int32)[:, None] * MOE_ROWS
        blk_expert = jnp.minimum(jnp.sum((pad_end[None, :] <= blk_row).astype(jnp.int32), axis=1),
                                 N_EXPERTS - 1)
        n_used = (pad_end[-1:] // MOE_ROWS).astype(jnp.int32)
        xb = _moe_dispatch(h2, dest, (pad_start + counts).astype(jnp.int32),
                           (padded - counts).astype(jnp.int32), n_used, n_rows, td)
        yb = _moe_experts(xb, blk_expert, n_used, *expert_w, l)
        last = l == len(layers) - 1
        x = _moe_combine(x, rf, dest, yb, ln_final, tc, final_rows if last else None)
        xs = (x, None)
    return x


def kernel(x_prompt, x_sample, rel_bias, ln_mix, w_in, ln_cq, ln_ckv, w_uq, w_ukv, w_oa, sink, w_ob,
           w_out, ln_ffn, w_gr, b_gr, w_er, b_er, w_gate, w_up, w_down, ln_final):
    bp, seq, d = x_prompt.shape
    bs = x_sample.shape[0]
    assert x_sample.shape[1] == seq and seq % WB == 0 and seq >= 3 * WB
    tabs = _rope_tables(seq)
    bias = _window_bias(rel_bias)
    layers = [_pack_layer(l, w_in, ln_mix, ln_cq, ln_ckv, w_uq, w_ukv, w_oa, sink, w_ob, w_out,
                          ln_ffn, w_gr, b_gr, w_er, b_er)
              for l in range(w_in.shape[0])]
    xs = (x_prompt.reshape(bp * seq, d), x_sample.reshape(bs * seq, d))
    y_prompt, y_sample = _trunk(xs, bp + bs, seq, tabs, bias, layers, (w_gate, w_up, w_down),
                                ln_final[None, :])
    return (y_prompt.reshape(bp, seq, d), y_sample.reshape(bs, seq, d))
```
